```python
import jax
import jax.numpy as jnp
from jax import lax
import numpy as np

D_MODEL = 1024
BATCH = 2
SEQ = 16384
DEPTH = 4
DEC_BATCH = 16
DEC_SEQ = 64
PAST_LEN = 4096

CHUNK = 64
N_EVEN = (DEPTH + 1) // 2
N_ODD = DEPTH // 2
N_SUB = 3
EPS = 1e-6
NEG = -1e30

MLA_HEADS = 8
MLA_NOPE = 64
MLA_ROPE = 32
MLA_V = 64
MLA_Q_LORA = 768
MLA_KV_LORA = 256
MLA_QBLOCK = 128
MLA_SCALE = (MLA_NOPE + MLA_ROPE) ** -0.5
ROPE_BASE = 10000.0
MLA_COLS = MLA_Q_LORA + MLA_KV_LORA + MLA_ROPE

RW_HEADS = 8
RW_N = 64
RW_C = RW_HEADS * RW_N
RW_DECAY_LORA = 64
RW_A_LORA = 64
RW_G_LORA = 128
RW_GN_EPS = 64e-5
RW_SPLITS = [RW_C, RW_C + RW_DECAY_LORA, 2 * RW_C + RW_DECAY_LORA, 3 * RW_C + RW_DECAY_LORA,
             3 * RW_C + RW_DECAY_LORA + RW_A_LORA]
RW_COLS = 3 * RW_C + RW_DECAY_LORA + RW_A_LORA + RW_G_LORA
EVEN_IN = MLA_COLS + RW_COLS
EVEN_MIX = MLA_HEADS * MLA_V + RW_C

SW_HEADS = 16
SW_KV_HEADS = 4
SW_GROUP = SW_HEADS // SW_KV_HEADS
SW_HD = 64
WINDOW = 128
WIN_CHUNKS = WINDOW // CHUNK
ODD_MIX = SW_HEADS * SW_HD
ODD_IN = ODD_MIX + 2 * SW_KV_HEADS * SW_HD

D_FF = 2816

kernel_name = 'hybrid_streaming_mla_rwkv7_swa_step'


def rms_norm(x, g):
    xf = x.astype(jnp.float32)
    y = xf * lax.rsqrt(jnp.mean(xf * xf, axis=-1, keepdims=True) + EPS)
    return (y * g.astype(jnp.float32)).astype(x.dtype)


def modulate(x, g, shift, scale):
    return rms_norm(x, g) * (1 + scale[:, None, :]) + shift[:, None, :]


def swiglu(h, w_in, w_out):
    gate, up = jnp.split(h @ w_in, 2, axis=-1)
    return (jax.nn.silu(gate) * up) @ w_out


def rope(x, pos):
    half = x.shape[-1] // 2
    freqs = ROPE_BASE ** (-jnp.arange(half, dtype=jnp.float32) / half)
    ang = pos[:, None] * freqs[None, :]
    ang = ang.reshape((ang.shape[0],) + (1,) * (x.ndim - 3) + (half,))
    cos, sin = jnp.cos(ang), jnp.sin(ang)
    xf = x.astype(jnp.float32)
    x1, x2 = xf[..., :half], xf[..., half:]
    return jnp.concatenate([x1 * cos - x2 * sin, x1 * sin + x2 * cos], axis=-1).astype(x.dtype)


def mla_expand(ckv, w_ukv):
    kv = jnp.einsum('bsl,lhe->bshe', ckv, w_ukv)
    return kv[..., :MLA_NOPE], kv[..., MLA_NOPE:]


def mla_attend(qn, qr, kn, kr, v, mask):
    s = (jnp.einsum('bqhd,bshd->bhqs', qn, kn) + jnp.einsum('bqhr,bsr->bhqs', qr, kr)).astype(jnp.float32) * MLA_SCALE
    if mask is not None:
        s = jnp.where(mask, s, NEG)
    p = jax.nn.softmax(s, axis=-1).astype(v.dtype)
    return jnp.einsum('bhqs,bshd->bqhd', p, v)


def mla_prompt(qn, qr, kn, kr, v):
    B, T = qn.shape[0], qn.shape[1]
    nb = T // MLA_QBLOCK
    key_chunk = jnp.arange(T) // CHUNK

    def to_blocks(t):
        return jnp.moveaxis(t.reshape((B, nb, MLA_QBLOCK) + t.shape[2:]), 1, 0)

    def block(args):
        bi, qn_b, qr_b = args
        q_chunk = (bi * MLA_QBLOCK + jnp.arange(MLA_QBLOCK)) // CHUNK
        mask = key_chunk[None, :] <= q_chunk[:, None]
        return mla_attend(qn_b, qr_b, kn, kr, v, mask)

    o = lax.map(block, (jnp.arange(nb), to_blocks(qn), to_blocks(qr)))
    return jnp.moveaxis(o, 0, 1).reshape(B, T, MLA_HEADS, MLA_V)


def wkv7_scan(r, w, k, v, a, b, s0):
    def step(S, inp):
        r_t, w_t, k_t, v_t, a_t, b_t = inp
        sa = jnp.einsum('bhij,bhj->bhi', S, a_t)
        S = S * w_t[:, :, None, :] + sa[..., None] * b_t[:, :, None, :] + v_t[..., None] * k_t[:, :, None, :]
        return S, jnp.einsum('bhij,bhj->bhi', S, r_t)

    xs = tuple(jnp.moveaxis(t, 1, 0) for t in (r, w, k, v, a, b))
    sT, ys = lax.scan(step, s0, xs)
    return jnp.moveaxis(ys, 0, 1), sT


def rwkv7(pr, sh0, s0, W, j):
    B, T, _ = pr.shape
    prev = jnp.concatenate([sh0[:, None].astype(pr.dtype), pr[:, :-1]], axis=1)
    pm = pr + (prev - pr) * W['rw_mu'][j]
    r, w_in, k, v, a_in, g_in = jnp.split(pm, RW_SPLITS, axis=-1)
    w = -jax.nn.softplus(-(W['rw_w0'][j] + jnp.tanh(w_in) @ W['rw_w2'][j])) - 0.5
    a = jax.nn.sigmoid(W['rw_a0'][j] + a_in @ W['rw_a2'][j])
    g = jax.nn.sigmoid(g_in) @ W['rw_g2'][j]

    def heads(t):
        return t.reshape(B, T, RW_HEADS, RW_N)

    kk = heads(k * W['rw_k_k'][j]).astype(jnp.float32)
    kk = kk / jnp.maximum(jnp.sqrt(jnp.sum(kk * kk, axis=-1, keepdims=True)), 1e-12)
    k = k * (1 + (a - 1) * W['rw_k_a'][j])
    rh, kh, vh, ah = [heads(t).astype(jnp.float32) for t in (r, k, v, a)]
    decay = jnp.exp(-jnp.exp(heads(w).astype(jnp.float32)))
    y, sT = wkv7_scan(rh, decay, kh, vh, -kk, kk * ah, s0.astype(jnp.float32))
    mu = jnp.mean(y, axis=-1, keepdims=True)
    var = jnp.mean(jnp.square(y - mu), axis=-1, keepdims=True)
    yn = ((y - mu) * lax.rsqrt(var + RW_GN_EPS)).reshape(B, T, RW_C) * W['rw_ln_w'][j] + W['rw_ln_b'][j]
    bonus = (jnp.sum(rh * kh * W['rw_r_k'][j], axis=-1, keepdims=True) * vh).reshape(B, T, RW_C)
    out = ((yn + bonus) * g).astype(pr.dtype)
    return out, sT.astype(s0.dtype), pr[:, -1]


def even_mixer(h, start, past, W, j):
    B, T, _ = h.shape
    proj = h @ W['even_w_in'][j]
    cq = proj[..., :MLA_Q_LORA]
    ckv = proj[..., MLA_Q_LORA:MLA_Q_LORA + MLA_KV_LORA]
    kr = proj[..., MLA_Q_LORA + MLA_KV_LORA:MLA_COLS]
    prw = proj[..., MLA_COLS:]
    pos = (start + jnp.arange(T)).astype(jnp.float32)
    q = jnp.einsum('btl,lhe->bthe', rms_norm(cq, W['mla_q_norm'][j]), W['mla_w_uq'][j])
    qn, qr = q[..., :MLA_NOPE], rope(q[..., MLA_NOPE:], pos)
    ckv = rms_norm(ckv, W['mla_kv_norm'][j])
    kr = rope(kr, pos)
    if past is None:
        kn, v = mla_expand(ckv, W['mla_w_ukv'][j])
        att = mla_prompt(qn, qr, kn, kr, v)
        s0 = jnp.zeros((B, RW_HEADS, RW_N, RW_N), h.dtype)
        sh0 = jnp.zeros((B, RW_COLS), h.dtype)
    else:
        ckv_past, kr_past, s0, sh0 = past
        kn, v = mla_expand(jnp.concatenate([ckv_past, ckv], axis=1), W['mla_w_ukv'][j])
        att = mla_attend(qn, qr, kn, jnp.concatenate([kr_past, kr], axis=1), v, None)
    y_rw, sT, shT = rwkv7(prw, sh0, s0, W, j)
    out = jnp.concatenate([att.reshape(B, T, MLA_HEADS * MLA_V), y_rw], axis=-1) @ W['even_w_out'][j]
    return out, (ckv, kr, sT, shT)


def alibi_slopes():
    hh = jnp.arange(1, SW_HEADS + 1, dtype=jnp.float32)
    return (2.0 ** (-8.0 * hh / SW_HEADS)).reshape(SW_KV_HEADS, SW_GROUP)


def sink_softmax(s, sinks):
    sk = sinks.astype(jnp.float32)[:, :, None, None]
    m = jnp.maximum(jnp.max(s, axis=-1, keepdims=True), sk)
    e = jnp.exp(s - m)
    return e / (jnp.sum(e, axis=-1, keepdims=True) + jnp.exp(sk - m))


def swa_prompt(q, k, v, sinks, slopes):
    B, T, HK, G, HD = q.shape
    nc = T // CHUNK
    pad = WIN_CHUNKS * CHUNK
    band = pad + CHUNK

    def bands(t):
        tp = jnp.pad(t, ((0, 0), (pad, 0), (0, 0), (0, 0))).reshape(B, nc + WIN_CHUNKS, CHUNK, HK, HD)
        return jnp.concatenate([tp[:, i:i + nc] for i in range(WIN_CHUNKS + 1)], axis=2)

    kb, vb = bands(k), bands(v)
    qb = q.reshape(B, nc, CHUNK, HK, G, HD)
    s = jnp.einsum('bnqkgd,bnskd->bnkgqs', qb, kb).astype(jnp.float32) * SW_HD ** -0.5
    qi = jnp.arange(CHUNK)
    kj = jnp.arange(band)
    dist = jnp.abs(qi[:, None] + pad - kj[None, :]).astype(jnp.float32)
    kpos = jnp.arange(nc)[:, None] * CHUNK - pad + kj[None, :]
    valid = (kpos >= 0)[:, None, None, None, :]
    s = jnp.where(valid, s - slopes[:, :, None, None] * dist, NEG)
    p = sink_softmax(s, sinks).astype(v.dtype)
    o = jnp.einsum('bnkgqs,bnskd->bnqkgd', p, vb)
    return o.reshape(B, T, HK, G, HD)


def swa_sample(q, k, v, start, n_past, sinks, slopes):
    T = q.shape[1]
    qpos = start + jnp.arange(T)
    kpos = start - n_past + jnp.arange(n_past + T)
    dist = jnp.abs(qpos[:, None] - kpos[None, :]).astype(jnp.float32)
    s = jnp.einsum('bqkgd,bskd->bkgqs', q, k).astype(jnp.float32) * SW_HD ** -0.5
    s = s - slopes[:, :, None, None] * dist
    p = sink_softmax(s, sinks).astype(v.dtype)
    return jnp.einsum('bkgqs,bskd->bqkgd', p, v)


def odd_mixer(h, start, past, W, j):
    B, T, _ = h.shape
    qkv = h @ W['odd_w_qkv'][j] + W['odd_b_qkv'][j]
    q = qkv[..., :ODD_MIX].reshape(B, T, SW_KV_HEADS, SW_GROUP, SW_HD)
    k = qkv[..., ODD_MIX:ODD_MIX + SW_KV_HEADS * SW_HD].reshape(B, T, SW_KV_HEADS, SW_HD)
    v = qkv[..., ODD_MIX + SW_KV_HEADS * SW_HD:].reshape(B, T, SW_KV_HEADS, SW_HD)
    slopes = alibi_slopes()
    sinks = W['swa_sinks'][j].reshape(SW_KV_HEADS, SW_GROUP)
    if past is None:
        o = swa_prompt(q, k, v, sinks, slopes)
        keep = min(WINDOW, T)
        k_new, v_new = k[:, T - keep:], v[:, T - keep:]
    else:
        k_past, v_past = past
        n_past = k_past.shape[1]
        k_all = jnp.concatenate([k_past, k], axis=1)
        v_all = jnp.concatenate([v_past, v], axis=1)
        o = swa_sample(q, k_all, v_all, start, n_past, sinks, slopes)
        k_new, v_new = k_all[:, T:], v_all[:, T:]
    out = o.reshape(B, T, ODD_MIX) @ W['odd_w_out'][j]
    return out, (k_new, v_new)


def trunk(x, c, start, past, W):
    B = x.shape[0]
    cs = jax.nn.silu(c)
    new_even, new_odd = [], []
    for i in range(DEPTH):
        mods = (cs @ W['w_ada'][i] + W['b_ada'][i]).reshape(B, 3 * N_SUB, D_MODEL)
        sh1, sc1, g1, sh2, sc2, g2, sh3, sc3, g3 = [mods[:, n] for n in range(3 * N_SUB)]
        x = x + 0.5 * g1[:, None] * swiglu(modulate(x, W['norm_g'][i, 0], sh1, sc1),
                                           W['ffn_w_in'][i, 0], W['ffn_w_out'][i, 0])
        h = modulate(x, W['norm_g'][i, 1], sh2, sc2)
        j = i // 2
        if i % 2 == 0:
            pst = None if past is None else (past[0][j], past[1][j], past[2][j], past[3][j])
            m, st = even_mixer(h, start, pst, W, j)
            new_even.append(st)
        else:
            pst = None if past is None else (past[4][j], past[5][j])
            m, st = odd_mixer(h, start, pst, W, j)
            new_odd.append(st)
        x = x + g2[:, None] * m
        x = x + 0.5 * g3[:, None] * swiglu(modulate(x, W['norm_g'][i, 2], sh3, sc3),
                                           W['ffn_w_in'][i, 1], W['ffn_w_out'][i, 1])
    even_states = [jnp.stack([st[n] for st in new_even]) for n in range(4)]
    odd_states = [jnp.stack([st[n] for st in new_odd]) for n in range(2)]
    return rms_norm(x, W['final_norm_g']), even_states + odd_states


def setup_inputs(seed: int = 0) -> dict:
    key = jax.random.key(seed)
    ks = iter(jax.random.split(key, 48))

    def nrm(shape, s):
        return jax.random.normal(next(ks), shape, jnp.float32) * s

    keep = min(WINDOW, PAST_LEN)
    d = {}
    d['x_prompt'] = nrm((BATCH, SEQ, D_MODEL), 1.0)
    d['x_sample'] = nrm((DEC_BATCH, DEC_SEQ, D_MODEL), 1.0)
    d['cache_mla_ckv'] = nrm((N_EVEN, DEC_BATCH, PAST_LEN, MLA_KV_LORA), 1.0)
    d['cache_mla_krope'] = nrm((N_EVEN, DEC_BATCH, PAST_LEN, MLA_ROPE), 1.0)
    d['state_rwkv'] = nrm((N_EVEN, DEC_BATCH, RW_HEADS, RW_N, RW_N), 0.5)
    d['state_rwkv_shift'] = nrm((N_EVEN, DEC_BATCH, RW_COLS), 1.0)
    d['cache_swa_k'] = nrm((N_ODD, DEC_BATCH, keep, SW_KV_HEADS, SW_HD), 1.0)
    d['cache_swa_v'] = nrm((N_ODD, DEC_BATCH, keep, SW_KV_HEADS, SW_HD), 1.0)
    d['c_prompt'] = nrm((BATCH, D_MODEL), 1.0)
    d['c_sample'] = nrm((DEC_BATCH, D_MODEL), 1.0)
    d['w_ada'] = nrm((DEPTH, D_MODEL, 3 * N_SUB * D_MODEL), 0.5 * D_MODEL ** -0.5)
    d['b_ada'] = nrm((DEPTH, 3 * N_SUB * D_MODEL), 0.02)
    d['norm_g'] = 1.0 + nrm((DEPTH, N_SUB, D_MODEL), 0.02)
    d['ffn_w_in'] = nrm((DEPTH, 2, D_MODEL, 2 * D_FF), D_MODEL ** -0.5)
    d['ffn_w_out'] = nrm((DEPTH, 2, D_FF, D_MODEL), D_FF ** -0.5)
    d['even_w_in'] = nrm((N_EVEN, D_MODEL, EVEN_IN), D_MODEL ** -0.5)
    d['even_w_out'] = nrm((N_EVEN, EVEN_MIX, D_MODEL), EVEN_MIX ** -0.5)
    d['mla_q_norm'] = 1.0 + nrm((N_EVEN, MLA_Q_LORA), 0.02)
    d['mla_kv_norm'] = 1.0 + nrm((N_EVEN, MLA_KV_LORA), 0.02)
    d['mla_w_uq'] = nrm((N_EVEN, MLA_Q_LORA, MLA_HEADS, MLA_NOPE + MLA_ROPE), MLA_Q_LORA ** -0.5)
    d['mla_w_ukv'] = nrm((N_EVEN, MLA_KV_LORA, MLA_HEADS, MLA_NOPE + MLA_V), MLA_KV_LORA ** -0.5)
    d['rw_mu'] = jax.random.uniform(next(ks), (N_EVEN, RW_COLS), jnp.float32)
    d['rw_w0'] = -1.0 + nrm((N_EVEN, RW_C), 0.5)
    d['rw_w2'] = nrm((N_EVEN, RW_DECAY_LORA, RW_C), 0.5 * RW_DECAY_LORA ** -0.5)
    d['rw_a0'] = nrm((N_EVEN, RW_C), 0.5)
    d['rw_a2'] = nrm((N_EVEN, RW_A_LORA, RW_C), 0.5 * RW_A_LORA ** -0.5)
    d['rw_g2'] = nrm((N_EVEN, RW_G_LORA, RW_C), RW_G_LORA ** -0.5)
    d['rw_k_k'] = 1.0 + nrm((N_EVEN, RW_C), 0.1)
    d['rw_k_a'] = 1.0 + nrm((N_EVEN, RW_C), 0.1)
    d['rw_r_k'] = nrm((N_EVEN, RW_HEADS, RW_N), 0.1)
    d['rw_ln_w'] = 1.0 + nrm((N_EVEN, RW_C), 0.02)
    d['rw_ln_b'] = nrm((N_EVEN, RW_C), 0.02)
    d['odd_w_qkv'] = nrm((N_ODD, D_MODEL, ODD_IN), D_MODEL ** -0.5)
    d['odd_b_qkv'] = nrm((N_ODD, ODD_IN), 0.02)
    d['odd_w_out'] = nrm((N_ODD, ODD_MIX, D_MODEL), ODD_MIX ** -0.5)
    d['swa_sinks'] = nrm((N_ODD, SW_HEADS), 0.5)
    d['final_norm_g'] = 1.0 + nrm((D_MODEL,), 0.02)
    return d


def reference(x_prompt, x_sample, cache_mla_ckv, cache_mla_krope, state_rwkv, state_rwkv_shift,
              cache_swa_k, cache_swa_v, c_prompt, c_sample, w_ada, b_ada, norm_g, ffn_w_in, ffn_w_out,
              even_w_in, even_w_out, mla_q_norm, mla_kv_norm, mla_w_uq, mla_w_ukv, rw_mu, rw_w0, rw_w2,
              rw_a0, rw_a2, rw_g2, rw_k_k, rw_k_a, rw_r_k, rw_ln_w, rw_ln_b, odd_w_qkv, odd_b_qkv,
              odd_w_out, swa_sinks, final_norm_g):
    W = dict(w_ada=w_ada, b_ada=b_ada, norm_g=norm_g, ffn_w_in=ffn_w_in, ffn_w_out=ffn_w_out,
             even_w_in=even_w_in, even_w_out=even_w_out, mla_q_norm=mla_q_norm, mla_kv_norm=mla_kv_norm,
             mla_w_uq=mla_w_uq, mla_w_ukv=mla_w_ukv, rw_mu=rw_mu, rw_w0=rw_w0, rw_w2=rw_w2, rw_a0=rw_a0,
             rw_a2=rw_a2, rw_g2=rw_g2, rw_k_k=rw_k_k, rw_k_a=rw_k_a, rw_r_k=rw_r_k, rw_ln_w=rw_ln_w,
             rw_ln_b=rw_ln_b, odd_w_qkv=odd_w_qkv, odd_b_qkv=odd_b_qkv, odd_w_out=odd_w_out,
             swa_sinks=swa_sinks, final_norm_g=final_norm_g)
    y_prompt, sp = trunk(x_prompt, c_prompt, 0, None, W)
    past = (cache_mla_ckv, cache_mla_krope, state_rwkv, state_rwkv_shift, cache_swa_k, cache_swa_v)
    y_sample, ss = trunk(x_sample, c_sample, cache_mla_ckv.shape[2], past, W)
    return (y_prompt, y_sample, sp[0], sp[1], sp[2], sp[3], sp[4], sp[5],
            ss[0], ss[1], ss[2], ss[3], ss[4], ss[5])
```

```python
import functools
import math

import jax
import jax.numpy as jnp
import numpy as np
from jax import lax
from jax.experimental import pallas as pl
from jax.experimental.pallas import tpu as pltpu

F32 = jnp.float32
BF16 = jnp.bfloat16

CHUNK = 64
EPS = 1e-6
NEG = -1e30
N_SUB = 3
MLA_HEADS = 8
MLA_NOPE = 64
MLA_ROPE = 32
MLA_V = 64
MLA_Q_LORA = 768
MLA_KV_LORA = 256
MLA_SCALE = (MLA_NOPE + MLA_ROPE) ** -0.5
ROPE_BASE = 10000.0
RW_HEADS = 8
RW_N = 64
RW_C = RW_HEADS * RW_N
RW_GN_EPS = 64e-5
RW_COLS = 3 * RW_C + 64 + 64 + 128
SW_HEADS = 16
SW_KV_HEADS = 4
SW_GROUP = 4
SW_HD = 64
WINDOW = 128

LANES = 128
VMEM_LIMIT = 56 * 1024 * 1024


def _cparams(sem):
    return pltpu.CompilerParams(dimension_semantics=sem, vmem_limit_bytes=VMEM_LIMIT)


def _const_spec(shape):
    nd = len(shape)
    return pl.BlockSpec(shape, lambda *_: (0,) * nd, pipeline_mode=pl.Buffered(1))


def _dot(a, b):
    return jnp.dot(a, b, preferred_element_type=F32)


def _dot_nt(a, b):
    return lax.dot_general(a, b, (((1,), (1,)), ((), ())), preferred_element_type=F32)


def _split2(x):
    hi = x.astype(BF16)
    lo = (x - hi.astype(F32)).astype(BF16)
    return hi, lo


def _mm3(a, b, nt=False):
    d = _dot_nt if nt else _dot
    ah, al = _split2(a)
    bh, bl = _split2(b)
    return (d(ah, bl) + d(al, bh)) + d(ah, bh)


def _norm_mod(x, g, sh, sc, nb):
    y = x * lax.rsqrt(jnp.mean(x * x, axis=-1, keepdims=True) + EPS) * g
    if nb == 1:
        return y * (1.0 + sc) + sh
    tm, d = x.shape
    y3 = y.reshape(nb, tm // nb, d)
    return (y3 * (1.0 + sc[:, None, :]) + sh[:, None, :]).reshape(tm, d)


def _gate_rows(g, y, nb):
    if nb == 1:
        return g * y
    tm, d = y.shape
    return (y.reshape(nb, tm // nb, d) * g[:, None, :]).reshape(tm, d)


def _row_tiling(n_rows, rows_per_seq, pref):
    if rows_per_seq >= pref:
        assert rows_per_seq % pref == 0
        return pref, 1
    tm = min(pref, n_rows)
    assert tm % rows_per_seq == 0 and n_rows % tm == 0
    return tm, tm // rows_per_seq


def _mods_spec(tm, nb, rows_per_seq):
    if nb == 1:
        tiles_per_seq = rows_per_seq // tm
        return pl.BlockSpec((1, 3 * N_SUB, 1024), lambda i: (i // tiles_per_seq, 0, 0))
    return pl.BlockSpec((nb, 3 * N_SUB, 1024), lambda i: (i, 0, 0))


def _ada_kernel(c_ref, w_ref, b_ref, o_ref):
    c = c_ref[...]
    cs = (c * jax.nn.sigmoid(c)).astype(BF16)
    o_ref[0] = _dot(cs, w_ref[0].astype(BF16)) + b_ref[0]


def _ada(c_all, w_ada, b_ada):
    depth, d, n = w_ada.shape
    bp = c_all.shape[0]
    tn = n // 4
    return pl.pallas_call(
        _ada_kernel,
        grid=(depth, n // tn),
        in_specs=[pl.BlockSpec((bp, d), lambda l, j: (0, 0)),
                  pl.BlockSpec((1, d, tn), lambda l, j: (l, 0, j)),
                  pl.BlockSpec((1, 1, tn), lambda l, j: (l, 0, j))],
        out_specs=pl.BlockSpec((1, bp, tn), lambda l, j: (l, 0, j)),
        out_shape=jax.ShapeDtypeStruct((depth, bp, n), F32),
        compiler_params=_cparams(("parallel", "parallel")),
    )(c_all, w_ada, b_ada.reshape(depth, 1, n))


FF_CHUNK = 256


def _ffn_kernel(x_ref, mods_ref, ng_ref, win_ref, wout_ref, *rest, sub, nb, d_ff, final):
    if final:
        fg_ref, o_ref, a_scr = rest
    else:
        o_ref, a_scr = rest
    x = x_ref[...]
    sh = mods_ref[:, 3 * sub, :]
    sc = mods_ref[:, 3 * sub + 1, :]
    gt = mods_ref[:, 3 * sub + 2, :]
    h = _norm_mod(x, ng_ref[...], sh, sc, nb).astype(BF16)
    for c in range(d_ff // FF_CHUNK):
        lo = c * FF_CHUNK
        g = _dot(h, win_ref[:, lo:lo + FF_CHUNK])
        u = _dot(h, win_ref[:, d_ff + lo:d_ff + lo + FF_CHUNK])
        a_scr[:, lo:lo + FF_CHUNK] = (g * jax.nn.sigmoid(g) * u).astype(BF16)
    y = _dot(a_scr[...], wout_ref[...])
    out = x + _gate_rows(0.5 * gt, y, nb)
    if final:
        out = out * lax.rsqrt(jnp.mean(out * out, axis=-1, keepdims=True) + EPS) * fg_ref[...]
    o_ref[...] = out


def _ffn(x, mods, norm_g, w_in, w_out, sub, rows_per_seq, final_g=None):
    n, d = x.shape
    d_ff = w_out.shape[0]
    tm, nb = _row_tiling(n, rows_per_seq, 512)
    final = final_g is not None
    in_specs = [pl.BlockSpec((tm, d), lambda i: (i, 0)),
                _mods_spec(tm, nb, rows_per_seq),
                _const_spec((1, d)), _const_spec(w_in.shape), _const_spec(w_out.shape)]
    args = [x, mods, norm_g.reshape(1, d), w_in, w_out]
    if final:
        in_specs.append(_const_spec((1, d)))
        args.append(final_g.reshape(1, d))
    return pl.pallas_call(
        functools.partial(_ffn_kernel, sub=sub, nb=nb, d_ff=d_ff, final=final),
        grid=(n // tm,),
        in_specs=in_specs,
        out_specs=pl.BlockSpec((tm, d), lambda i: (i, 0)),
        out_shape=jax.ShapeDtypeStruct((n, d), F32),
        scratch_shapes=[pltpu.VMEM((tm, d_ff), BF16)],
        compiler_params=_cparams(("parallel",)),
    )(*args)


EVEN_W = 768 + 256 + 1792 + 128
_RW_OFF = 1056
_RW_PERM = np.concatenate([np.arange(0, 512), np.arange(576, 1088), np.arange(1088, 1600),
                           np.arange(512, 576), np.arange(1600, 1664), np.arange(1664, 1792)])
_RW_INV = np.argsort(_RW_PERM)
_EVEN_PERM = np.concatenate([np.arange(0, 1024), _RW_OFF + _RW_PERM, np.arange(1024, 1056)])


def _rope_slot(v, c, s1, s2):
    w = v.shape[-1]
    return v * c + pltpu.roll(v, w - 16, axis=1) * s1 + pltpu.roll(v, 16, axis=1) * s2


def _even_in_kernel(x_ref, mods_ref, ng_ref, w_ref, qn_ref, kvn_ref, wuq_ref,
                    cq_ref, s1q_ref, s2q_ref, ck_ref, s1k_ref, s2k_ref,
                    q_out, ckv_out, kr_out, prw_out, *, nb):
    x = x_ref[...]
    h = _norm_mod(x, ng_ref[...], mods_ref[:, 3, :], mods_ref[:, 4, :], nb).astype(BF16)
    cq = _dot(h, w_ref[:, 0:768])
    cqn = (cq * lax.rsqrt(jnp.mean(cq * cq, axis=-1, keepdims=True) + EPS) * qn_ref[...]).astype(BF16)
    q = _dot(cqn, wuq_ref[...])
    rep = lambda t: jnp.concatenate([t] * MLA_HEADS, axis=1)
    q = _rope_slot(q, rep(cq_ref[...]), rep(s1q_ref[...]), rep(s2q_ref[...]))
    q_out[...] = (q * MLA_SCALE).astype(BF16)
    ckv = _dot(h, w_ref[:, 768:1024])
    ckv_out[...] = ckv * lax.rsqrt(jnp.mean(ckv * ckv, axis=-1, keepdims=True) + EPS) * kvn_ref[...]
    prw_out[...] = _dot(h, w_ref[:, 1024:2816])
    krs = _dot(h, w_ref[:, 2816:2944])
    krs = _rope_slot(krs, ck_ref[...], s1k_ref[...], s2k_ref[...])
    kr_out[...] = krs[:, 0:MLA_ROPE]


def _even_in(x, mods, norm_g, w_perm, q_norm, kv_norm, wuq_slot, tabs, rows_per_seq):
    n, d = x.shape
    tm, nb = _row_tiling(n, rows_per_seq, 512)
    ttab = tabs[0].shape[0]
    ntab = ttab // tm
    tab_spec = pl.BlockSpec((tm, LANES), lambda i: (i % ntab, 0))
    row = lambda w: pl.BlockSpec((tm, w), lambda i: (i, 0))
    return pl.pallas_call(
        functools.partial(_even_in_kernel, nb=nb),
        grid=(n // tm,),
        in_specs=[row(d), _mods_spec(tm, nb, rows_per_seq), _const_spec((1, d)),
                  _const_spec(w_perm.shape), _const_spec((1, 768)), _const_spec((1, 256)),
                  _const_spec(wuq_slot.shape)] + [tab_spec] * 6,
        out_specs=[row(1024), row(256), row(MLA_ROPE), row(RW_COLS)],
        out_shape=[jax.ShapeDtypeStruct((n, 1024), BF16), jax.ShapeDtypeStruct((n, 256), F32),
                   jax.ShapeDtypeStruct((n, MLA_ROPE), F32), jax.ShapeDtypeStruct((n, RW_COLS), F32)],
        compiler_params=_cparams(("parallel",)),
    )(x, mods, norm_g.reshape(1, d), w_perm, q_norm.reshape(1, 768), kv_norm.reshape(1, 256),
      wuq_slot, *tabs)


def _rope_tables(pos, tile_to):
    half = MLA_ROPE // 2
    freqs = ROPE_BASE ** (-jnp.arange(half, dtype=F32) / half)
    ang = pos.astype(F32)[:, None] * freqs[None, :]
    cos, sin = jnp.cos(ang), jnp.sin(ang)
    t = pos.shape[0]
    z = lambda w: jnp.zeros((t, w), F32)
    o = lambda w: jnp.ones((t, w), F32)
    cq = jnp.concatenate([o(64), cos, cos, z(32)], axis=1)
    s1q = jnp.concatenate([z(64), -sin, z(48)], axis=1)
    s2q = jnp.concatenate([z(80), sin, z(32)], axis=1)
    ck = jnp.concatenate([cos, cos, z(96)], axis=1)
    s1k = jnp.concatenate([-sin, z(112)], axis=1)
    s2k = jnp.concatenate([z(16), sin, z(96)], axis=1)
    tabs = [cq, s1q, s2q, ck, s1k, s2k]
    if tile_to > t:
        tabs = [jnp.tile(a, (tile_to // t, 1)) for a in tabs]
    return tabs


def _kv_expand_kernel(ckv_ref, kr_ref, wk_ref, sel_ref, wv_ref, k_out, v_out):
    c = ckv_ref[...].astype(BF16)
    k = _dot(c, wk_ref[...]) + _dot(kr_ref[...].astype(BF16), sel_ref[...])
    k_out[...] = k.astype(BF16)
    v_out[...] = _dot(c, wv_ref[...]).astype(BF16)


def _kv_expand(ckv, kr, wk_slot, sel, wv):
    n = ckv.shape[0]
    tm = min(1024, n)
    assert n % tm == 0
    row = lambda w: pl.BlockSpec((tm, w), lambda i: (i, 0))
    return pl.pallas_call(
        _kv_expand_kernel,
        grid=(n // tm,),
        in_specs=[row(256), row(MLA_ROPE), _const_spec(wk_slot.shape), _const_spec(sel.shape),
                  _const_spec(wv.shape)],
        out_specs=[row(1024), row(512)],
        out_shape=[jax.ShapeDtypeStruct((n, 1024), BF16), jax.ShapeDtypeStruct((n, 512), BF16)],
        compiler_params=_cparams(("parallel",)),
    )(ckv, kr, wk_slot, sel, wv)


def _mla_attn_kernel(qi_ref, ki_ref, fl_ref, q_ref, k_ref, v_ref, o_ref, m_scr, l_scr, acc_scr,
                     *, tq, tk, q_off):
    p_id = pl.program_id(1)
    flags = fl_ref[p_id]
    first = (flags & 1) != 0
    last = (flags & 2) != 0
    masked = (flags & 4) != 0

    @pl.when(first)
    def _():
        m_scr[...] = jnp.full(m_scr.shape, NEG, F32)
        l_scr[...] = jnp.zeros(l_scr.shape, F32)
        acc_scr[...] = jnp.zeros(acc_scr.shape, F32)

    def body(use_mask):
        if use_mask:
            qpos = q_off + qi_ref[p_id] * tq + lax.broadcasted_iota(jnp.int32, (tq, tk), 0)
            kpos = ki_ref[p_id] * tk + lax.broadcasted_iota(jnp.int32, (tq, tk), 1)
            vis = (kpos // CHUNK) <= (qpos // CHUNK)
        for h in range(MLA_HEADS):
            sl = slice(h * LANES, (h + 1) * LANES)
            s = _dot_nt(q_ref[0, :, sl], k_ref[0, :, sl])
            if use_mask:
                s = jnp.where(vis, s, NEG)
            m_prev = m_scr[h]
            m_new = jnp.maximum(m_prev, jnp.max(s, axis=-1, keepdims=True))
            alpha = jnp.exp(m_prev - m_new)
            p = jnp.exp(s - m_new[:, 0:1])
            l_scr[h] = alpha * l_scr[h] + jnp.sum(p, axis=-1, keepdims=True)
            m_scr[h] = m_new
            vp = v_ref[0, :, (h // 2) * LANES:(h // 2 + 1) * LANES]
            acc_scr[h] = alpha * acc_scr[h] + _dot(p.astype(BF16), vp)

    @pl.when(masked)
    def _():
        body(True)

    @pl.when(jnp.logical_not(masked))
    def _():
        body(False)

    @pl.when(last)
    def _():
        lane = lax.broadcasted_iota(jnp.int32, (tq, LANES), 1)
        for pr in range(MLA_HEADS // 2):
            oe = acc_scr[2 * pr] / l_scr[2 * pr]
            oo = acc_scr[2 * pr + 1] / l_scr[2 * pr + 1]
            o_ref[0, :, pr * LANES:(pr + 1) * LANES] = jnp.where(lane < 64, oe, oo).astype(BF16)


def _mla_attn(q, k, v, pairs, tq, tk, q_off):
    b, t_q, _ = q.shape
    qi, ki, fl = pairs
    grid_spec = pltpu.PrefetchScalarGridSpec(
        num_scalar_prefetch=3,
        grid=(b, qi.shape[0]),
        in_specs=[pl.BlockSpec((1, tq, 1024), lambda bb, p, qi, ki, fl: (bb, qi[p], 0)),
                  pl.BlockSpec((1, tk, 1024), lambda bb, p, qi, ki, fl: (bb, ki[p], 0)),
                  pl.BlockSpec((1, tk, 512), lambda bb, p, qi, ki, fl: (bb, ki[p], 0))],
        out_specs=pl.BlockSpec((1, tq, 512), lambda bb, p, qi, ki, fl: (bb, qi[p], 0)),
        scratch_shapes=[pltpu.VMEM((MLA_HEADS, tq, LANES), F32),
                        pltpu.VMEM((MLA_HEADS, tq, LANES), F32),
                        pltpu.VMEM((MLA_HEADS, tq, LANES), F32)])
    return pl.pallas_call(
        functools.partial(_mla_attn_kernel, tq=tq, tk=tk, q_off=q_off),
        grid_spec=grid_spec,
        out_shape=jax.ShapeDtypeStruct((b, t_q, 512), BF16),
        compiler_params=_cparams(("parallel", "arbitrary")),
    )(qi, ki, fl, q, k, v)


def _causal_pairs(nq):
    qi, ki, fl = [], [], []
    for a in range(nq):
        for c in range(a + 1):
            qi.append(a)
            ki.append(c)
            fl.append((1 if c == 0 else 0) | (6 if c == a else 0))
    return tuple(jnp.asarray(np.array(z, np.int32)) for z in (qi, ki, fl))


def _full_pairs(nk):
    qi = [0] * nk
    ki = list(range(nk))
    fl = [(1 if c == 0 else 0) | (6 if c == nk - 1 else 0) for c in range(nk)]
    return tuple(jnp.asarray(np.array(z, np.int32)) for z in (qi, ki, fl))


def _pair_sum(x, ones_bd):
    hi, lo = _split2(x)
    return _dot(hi, ones_bd) + _dot(lo, ones_bd)


def _rwkv_kernel(pr_ref, sh0_ref, s0_ref, mu_ref, w0_ref, wl_ref, a0_ref, al_ref, g2_ref,
                 kk_ref, ka_ref, rk_ref, lnw_ref, lnb_ref, y_ref, st_ref,
                 prev_scr, s_scr, rt_scr, at_scr, bt_scr, kt_scr, bv_scr, k2_scr, v_scr,
                 cum_scr, yc_scr, *, tt):
    t_id = pl.program_id(1)
    n_pairs = RW_HEADS // 2

    @pl.when(t_id == 0)
    def _():
        s_scr[...] = s0_ref[0]
        prev_scr[0:1, :] = sh0_ref[0]

    pr = pr_ref[0]
    row = lax.broadcasted_iota(jnp.int32, (tt, 1), 0)
    prev = jnp.where(row == 0, prev_scr[0:1, :], pltpu.roll(pr, 1, axis=0))
    prev_scr[0:1, :] = pr[tt - 1:tt, :]
    pm = pr + (prev - pr) * mu_ref[...]
    r = pm[:, 0:512]
    k = pm[:, 512:1024]
    v = pm[:, 1024:1536]
    wa = pm[:, 1536:1664]
    g_in = pm[:, 1664:1792]
    z = w0_ref[...] + _dot(jnp.tanh(wa).astype(BF16), wl_ref[...])
    nz = -z
    w = -(jnp.maximum(nz, 0.0) + jnp.log1p(jnp.exp(-jnp.abs(nz)))) - 0.5
    ld = -jnp.exp(w)
    a_sig = jax.nn.sigmoid(a0_ref[...] + _dot(wa.astype(BF16), al_ref[...]))
    g = _dot(jax.nn.sigmoid(g_in).astype(BF16), g2_ref[...])

    li = lax.broadcasted_iota(jnp.int32, (LANES, LANES), 0)
    lj = lax.broadcasted_iota(jnp.int32, (LANES, LANES), 1)
    ones_bd = jnp.where((li // RW_N) == (lj // RW_N), 1.0, 0.0).astype(BF16)

    def head_sum(x):
        return jnp.concatenate([_pair_sum(x[:, p * LANES:(p + 1) * LANES], ones_bd)
                                for p in range(n_pairs)], axis=1)

    kk = k * kk_ref[...]
    kk = kk / jnp.maximum(jnp.sqrt(head_sum(kk * kk)), 1e-12)
    k2 = k * (1.0 + (a_sig - 1.0) * ka_ref[...])
    bonus = head_sum(r * k2 * rk_ref[...]) * v

    ti = lax.broadcasted_iota(jnp.int32, (tt, tt), 0)
    tj = lax.broadcasted_iota(jnp.int32, (tt, tt), 1)
    tri = jnp.where(((ti // CHUNK) == (tj // CHUNK)) & (tj <= ti), 1.0, 0.0).astype(BF16)
    l1 = ld.astype(BF16)
    rem = ld - l1.astype(F32)
    l2 = rem.astype(BF16)
    l3 = (rem - l2.astype(F32)).astype(BF16)
    cum = (_dot(tri, l3) + _dot(tri, l2)) + _dot(tri, l1)
    winv = jnp.exp(-cum)
    rt_scr[...] = r * jnp.exp(cum)
    at_scr[...] = -kk * jnp.exp(cum - ld)
    bv = kk * a_sig
    bt_scr[...] = bv * winv
    kt_scr[...] = k2 * winv
    bv_scr[...] = bv
    k2_scr[...] = k2
    v_scr[...] = v
    cum_scr[...] = cum

    lane = lax.broadcasted_iota(jnp.int32, (CHUNK, LANES), 1)
    even = lane < RW_N
    ri = li % RW_N
    rj = lj % RW_N
    strict = rj < ri
    incl = rj <= ri
    eye = jnp.where(li == lj, 1.0, 0.0).astype(F32)

    def stack(x):
        return jnp.concatenate([jnp.where(even, x, 0.0), jnp.where(even, 0.0, x)], axis=0)

    def chunk_body(c, carry):
        rows = pl.ds(pl.multiple_of(c * CHUNK, CHUNK), CHUNK)
        for p in range(n_pairs):
            sl = slice(p * LANES, (p + 1) * LANES)
            cum_c = cum_scr[rows, sl]
            cum_l = cum_c[CHUNK - 1:CHUNK, :]
            w2 = jnp.exp(cum_l - cum_c)
            rs = stack(rt_scr[rows, sl])
            as_ = stack(at_scr[rows, sl])
            bs = stack(bt_scr[rows, sl])
            ks = stack(kt_scr[rows, sl])
            vs = stack(v_scr[rows, sl])
            b2s = stack(bv_scr[rows, sl] * w2)
            k2s = stack(k2_scr[rows, sl] * w2)
            n_m = jnp.where(strict, _mm3(as_, bs, nt=True), 0.0)
            mk = jnp.where(strict, _mm3(as_, ks, nt=True), 0.0)
            cb = jnp.where(incl, _mm3(rs, bs, nt=True), 0.0)
            ck = jnp.where(incl, _mm3(rs, ks, nt=True), 0.0)
            pw = n_m
            tinv = eye + n_m
            for _ in range(5):
                pw = _mm3(pw, pw)
                tinv = tinv + _mm3(tinv, pw)
            s = s_scr[p]
            x = _mm3(as_, s, nt=True) + _mm3(mk, vs)
            u = _mm3(tinv, x)
            ys = _mm3(rs, s, nt=True) + _mm3(cb, u) + _mm3(ck, vs)
            yc_scr[rows, sl] = ys[0:CHUNK] + ys[CHUNK:2 * CHUNK]
            s_scr[p] = s * jnp.exp(cum_l) + _mm3(u.T, b2s) + _mm3(vs.T, k2s)
        return carry

    lax.fori_loop(0, tt // CHUNK, chunk_body, 0)

    y = yc_scr[...]
    mean = head_sum(y) * (1.0 / RW_N)
    dlt = y - mean
    var = head_sum(dlt * dlt) * (1.0 / RW_N)
    yn = dlt * lax.rsqrt(var + RW_GN_EPS) * lnw_ref[...] + lnb_ref[...]
    y_ref[0] = ((yn + bonus) * g).astype(BF16)

    @pl.when(t_id == pl.num_programs(1) - 1)
    def _():
        st_ref[0] = s_scr[...]


def _rwkv(prw, sh0, s0_bd, wts):
    b, t, _ = prw.shape
    tt = min(256, t)
    assert t % tt == 0 and tt % CHUNK == 0
    c512 = _const_spec((1, RW_C))
    scr = lambda: pltpu.VMEM((tt, RW_C), F32)
    return pl.pallas_call(
        functools.partial(_rwkv_kernel, tt=tt),
        grid=(b, t // tt),
        in_specs=[pl.BlockSpec((1, tt, RW_COLS), lambda bb, i: (bb, i, 0)),
                  pl.BlockSpec((1, 1, RW_COLS), lambda bb, i: (bb, 0, 0)),
                  pl.BlockSpec((1, 4, LANES, LANES), lambda bb, i: (bb, 0, 0, 0)),
                  _const_spec((1, RW_COLS)), c512, _const_spec((LANES, RW_C)), c512,
                  _const_spec((LANES, RW_C)), _const_spec((LANES, RW_C)),
                  c512, c512, c512, c512, c512],
        out_specs=[pl.BlockSpec((1, tt, RW_C), lambda bb, i: (bb, i, 0)),
                   pl.BlockSpec((1, 4, LANES, LANES), lambda bb, i: (bb, 0, 0, 0))],
        out_shape=[jax.ShapeDtypeStruct((b, t, RW_C), BF16),
                   jax.ShapeDtypeStruct((b, 4, LANES, LANES), F32)],
        scratch_shapes=[pltpu.VMEM((8, RW_COLS), F32), pltpu.VMEM((4, LANES, LANES), F32)]
                       + [scr() for _ in range(9)],
        compiler_params=_cparams(("parallel", "arbitrary")),
    )(prw, sh0, s0_bd, *wts)


def _state_to_bd(s):
    b = s.shape[0]
    s = s.reshape(b, 4, 2, RW_N, RW_N)
    z = jnp.zeros_like(s[:, :, 0])
    top = jnp.concatenate([s[:, :, 0], z], axis=-1)
    bot = jnp.concatenate([z, s[:, :, 1]], axis=-1)
    return jnp.concatenate([top, bot], axis=-2)


def _state_from_bd(s):
    b = s.shape[0]
    return jnp.stack([s[:, :, :RW_N, :RW_N], s[:, :, RW_N:, RW_N:]], axis=2).reshape(b, RW_HEADS, RW_N, RW_N)


def _out_proj_kernel(*refs, n_in, nb):
    x_ref, mods_ref = refs[0], refs[1]
    a_refs = refs[2:2 + n_in]
    w_refs = refs[2 + n_in:2 + 2 * n_in]
    o_ref = refs[2 + 2 * n_in]
    y = _dot(a_refs[0][...], w_refs[0][...])
    for a, w in zip(a_refs[1:], w_refs[1:]):
        y = y + _dot(a[...], w[...])
    o_ref[...] = x_ref[...] + _gate_rows(mods_ref[:, 5, :], y, nb)


def _out_proj(x, mods, acts, ws, rows_per_seq):
    n, d = x.shape
    tm, nb = _row_tiling(n, rows_per_seq, 512)
    row = lambda w: pl.BlockSpec((tm, w), lambda i: (i, 0))
    return pl.pallas_call(
        functools.partial(_out_proj_kernel, n_in=len(acts), nb=nb),
        grid=(n // tm,),
        in_specs=[row(d), _mods_spec(tm, nb, rows_per_seq)] + [row(a.shape[1]) for a in acts]
                 + [_const_spec(w.shape) for w in ws],
        out_specs=row(d),
        out_shape=jax.ShapeDtypeStruct((n, d), F32),
        compiler_params=_cparams(("parallel",)),
    )(x, mods, *acts, *ws)


_ODD_PERM = np.concatenate(
    [np.arange(0, 1024)]
    + [np.tile(1024 + g * 64 + np.arange(64), 2) for g in range(SW_KV_HEADS)]
    + [np.tile(1280 + g * 64 + np.arange(64), 2) for g in range(SW_KV_HEADS)])


def _odd_in_kernel(x_ref, mods_ref, ng_ref, w_ref, b_ref, q_out, k_out, v_out, *, nb):
    h = _norm_mod(x_ref[...], ng_ref[...], mods_ref[:, 3, :], mods_ref[:, 4, :], nb).astype(BF16)
    q = _dot(h, w_ref[:, 0:1024]) + b_ref[:, 0:1024]
    q_out[...] = (q * (SW_HD ** -0.5)).astype(BF16)
    k_out[...] = _dot(h, w_ref[:, 1024:1536]) + b_ref[:, 1024:1536]
    v_out[...] = _dot(h, w_ref[:, 1536:2048]) + b_ref[:, 1536:2048]


def _odd_in(x, mods, norm_g, w_perm, b_perm, rows_per_seq):
    n, d = x.shape
    tm, nb = _row_tiling(n, rows_per_seq, 512)
    row = lambda w: pl.BlockSpec((tm, w), lambda i: (i, 0))
    return pl.pallas_call(
        functools.partial(_odd_in_kernel, nb=nb),
        grid=(n // tm,),
        in_specs=[row(d), _mods_spec(tm, nb, rows_per_seq), _const_spec((1, d)),
                  _const_spec(w_perm.shape), _const_spec((1, 2048))],
        out_specs=[row(1024), row(512), row(512)],
        out_shape=[jax.ShapeDtypeStruct((n, 1024), BF16), jax.ShapeDtypeStruct((n, 512), F32),
                   jax.ShapeDtypeStruct((n, 512), F32)],
        compiler_params=_cparams(("parallel",)),
    )(x, mods, norm_g.reshape(1, d), w_perm, b_perm.reshape(1, 2048))


def _swa_kernel(sinks_ref, q_ref, kp_ref, kc_ref, vp_ref, vc_ref, o_ref, *, tq, mask_first_prev):
    nk = WINDOW + tq
    qi = lax.broadcasted_iota(jnp.int32, (tq, nk), 0)
    kj = lax.broadcasted_iota(jnp.int32, (tq, nk), 1)
    kc = kj // CHUNK - WINDOW // CHUNK
    qc = qi // CHUNK
    vis = (kc <= qc) & (kc >= qc - WINDOW // CHUNK)
    if mask_first_prev:
        vis = vis & ((kj >= WINDOW) | (pl.program_id(1) > 0))
    dist = jnp.abs(qi + WINDOW - kj).astype(F32)
    keys = jnp.concatenate([kp_ref[0], kc_ref[0]], axis=0).astype(BF16)
    vals = jnp.concatenate([vp_ref[0], vc_ref[0]], axis=0).astype(BF16)
    lane = lax.broadcasted_iota(jnp.int32, (tq, LANES), 1)
    low = lane < SW_HD
    outs = []
    for h in range(SW_HEADS):
        g = h // SW_GROUP
        slope = 2.0 ** (-8.0 * (h + 1) / SW_HEADS)
        qp = q_ref[0, :, (h // 2) * LANES:(h // 2 + 1) * LANES]
        qh = jnp.where(low if h % 2 == 0 else jnp.logical_not(low), qp, jnp.zeros_like(qp))
        s = _dot_nt(qh, keys[:, g * LANES:(g + 1) * LANES])
        s = jnp.where(vis, s - slope * dist, NEG)
        sk = sinks_ref[h]
        m = jnp.maximum(jnp.max(s, axis=-1, keepdims=True), sk)
        e = jnp.exp(s - m)
        p = e / (jnp.sum(e, axis=-1, keepdims=True) + jnp.exp(sk - m))
        outs.append(_dot(p.astype(BF16), vals[:, g * LANES:(g + 1) * LANES]))
    for pr in range(SW_HEADS // 2):
        o_ref[0, :, pr * LANES:(pr + 1) * LANES] = jnp.where(low, outs[2 * pr], outs[2 * pr + 1]).astype(BF16)


def _swa(q, k_prev, k_cur, v_prev, v_cur, sinks, tq, same_array):
    b, t, _ = q.shape
    nt = t // tq
    per = tq // WINDOW
    if same_array:
        prev_map = lambda bb, i: (bb, jnp.maximum(i * per - 1, 0), 0)
    else:
        prev_map = lambda bb, i: (bb, 0, 0)
    cur = lambda w: pl.BlockSpec((1, tq, w), lambda bb, i: (bb, i, 0))
    prev = pl.BlockSpec((1, WINDOW, 512), prev_map)
    return pl.pallas_call(
        functools.partial(_swa_kernel, tq=tq, mask_first_prev=same_array),
        grid=(b, nt),
        in_specs=[pl.BlockSpec(memory_space=pltpu.SMEM), cur(1024), prev, cur(512), prev, cur(512)],
        out_specs=cur(1024),
        out_shape=jax.ShapeDtypeStruct((b, t, 1024), BF16),
        compiler_params=_cparams(("parallel", "parallel")),
    )(sinks, q, k_prev, k_cur, v_prev, v_cur)


def _undup(a):
    return a.reshape(a.shape[:-1] + (SW_KV_HEADS, 2, SW_HD))[..., 0, :]


def _dup(a):
    return jnp.concatenate([a, a], axis=-1).reshape(a.shape[:-2] + (512,))


def _prep_weights(p):
    depth = p['w_ada'].shape[0]
    n_even, n_odd = (depth + 1) // 2, depth // 2
    w = {}
    w['ffn_in'] = p['ffn_w_in'].astype(BF16)
    w['ffn_out'] = p['ffn_w_out'].astype(BF16)
    w['even_in'] = jnp.concatenate(
        [p['even_w_in'][:, :, _EVEN_PERM], jnp.zeros((n_even, 1024, 96), F32)], axis=2).astype(BF16)
    w['wuq'] = jnp.pad(p['mla_w_uq'], ((0, 0), (0, 0), (0, 0), (0, 32))).reshape(n_even, 768, 1024).astype(BF16)
    w['wk'] = jnp.pad(p['mla_w_ukv'][..., :MLA_NOPE], ((0, 0), (0, 0), (0, 0), (0, 64))
                      ).reshape(n_even, 256, 1024).astype(BF16)
    w['wv'] = p['mla_w_ukv'][..., MLA_NOPE:].reshape(n_even, 256, 512).astype(BF16)
    sel = np.zeros((MLA_ROPE, 1024), np.float32)
    for h in range(MLA_HEADS):
        sel[np.arange(MLA_ROPE), h * LANES + MLA_NOPE + np.arange(MLA_ROPE)] = 1.0
    w['sel'] = jnp.asarray(sel).astype(BF16)
    z64 = jnp.zeros((n_even, 64, RW_C), F32)
    w['wl'] = jnp.concatenate([p['rw_w2'], z64], axis=1).astype(BF16)
    w['al'] = jnp.concatenate([z64, p['rw_a2']], axis=1).astype(BF16)
    w['g2'] = p['rw_g2'].astype(BF16)
    w['mu'] = p['rw_mu'][:, _RW_PERM]
    w['even_out'] = p['even_w_out'].astype(BF16)
    w['odd_in'] = p['odd_w_qkv'][:, :, _ODD_PERM].astype(BF16)
    w['odd_b'] = p['odd_b_qkv'][:, _ODD_PERM]
    w['odd_out'] = p['odd_w_out'].astype(BF16)
    return w


def _trunk(x3, mods_all, start, past, p, w):
    b, t, d = x3.shape
    n = b * t
    depth = mods_all.shape[0]
    x = x3.reshape(n, d)
    rows = t
    tm_even, _ = _row_tiling(n, rows, 512)
    pos = start + jnp.arange(t)
    tabs = _rope_tables(pos, tm_even)
    even_states, odd_states = [], []
    for i in range(depth):
        mods = mods_all[i]
        j = i // 2
        x = _ffn(x, mods, p['norm_g'][i, 0], w['ffn_in'][i, 0], w['ffn_out'][i, 0], 0, rows)
        if i % 2 == 0:
            q, ckv, kr, prw = _even_in(x, mods, p['norm_g'][i, 1], w['even_in'][j], p['mla_q_norm'][j],
                                       p['mla_kv_norm'][j], w['wuq'][j], tabs, rows)
            if past is None:
                kx, vx = _kv_expand(ckv, kr, w['wk'][j], w['sel'], w['wv'][j])
                tq = min(512, t)
                att = _mla_attn(q.reshape(b, t, 1024), kx.reshape(b, t, 1024), vx.reshape(b, t, 512),
                                _causal_pairs(t // tq), tq, tq, 0)
                s0 = jnp.zeros((b, RW_HEADS, RW_N, RW_N), F32)
                sh0 = jnp.zeros((b, RW_COLS), F32)
            else:
                ckv_past, kr_past, s0, sh0 = past[0][j], past[1][j], past[2][j], past[3][j]
                n_past = ckv_past.shape[1]
                tk = 512
                t_all = -(-(n_past + t) // tk) * tk
                pad = t_all - n_past - t
                ckv_all = jnp.concatenate([ckv_past, ckv.reshape(b, t, 256),
                                           jnp.zeros((b, pad, 256), F32)], axis=1)
                kr_all = jnp.concatenate([kr_past, kr.reshape(b, t, MLA_ROPE),
                                          jnp.zeros((b, pad, MLA_ROPE), F32)], axis=1)
                kx, vx = _kv_expand(ckv_all.reshape(b * t_all, 256), kr_all.reshape(b * t_all, MLA_ROPE),
                                    w['wk'][j], w['sel'], w['wv'][j])
                att = _mla_attn(q.reshape(b, t, 1024), kx.reshape(b, t_all, 1024), vx.reshape(b, t_all, 512),
                                _full_pairs(t_all // tk), t, tk, start)
            rw_wts = (w['mu'][j].reshape(1, RW_COLS), p['rw_w0'][j].reshape(1, RW_C), w['wl'][j],
                      p['rw_a0'][j].reshape(1, RW_C), w['al'][j], w['g2'][j],
                      p['rw_k_k'][j].reshape(1, RW_C), p['rw_k_a'][j].reshape(1, RW_C),
                      p['rw_r_k'][j].reshape(1, RW_C), p['rw_ln_w'][j].reshape(1, RW_C),
                      p['rw_ln_b'][j].reshape(1, RW_C))
            prw3 = prw.reshape(b, t, RW_COLS)
            y_rw, s_bd = _rwkv(prw3, sh0[:, _RW_PERM].reshape(b, 1, RW_COLS), _state_to_bd(s0), rw_wts)
            x = _out_proj(x, mods, [att.reshape(n, 512), y_rw.reshape(n, RW_C)],
                          [w['even_out'][j][:512], w['even_out'][j][512:]], rows)
            even_states.append((ckv.reshape(b, t, 256), kr.reshape(b, t, MLA_ROPE), _state_from_bd(s_bd),
                                prw3[:, t - 1, :][:, _RW_INV]))
        else:
            q, kd, vd = _odd_in(x, mods, p['norm_g'][i, 1], w['odd_in'][j], w['odd_b'][j], rows)
            q3, kd3, vd3 = q.reshape(b, t, 1024), kd.reshape(b, t, 512), vd.reshape(b, t, 512)
            if past is None:
                tq = min(256, t)
                o = _swa(q3, kd3, kd3, vd3, vd3, p['swa_sinks'][j], tq, True)
                keep = min(WINDOW, t)
                k_new, v_new = _undup(kd3[:, t - keep:]), _undup(vd3[:, t - keep:])
            else:
                k_past, v_past = past[4][j], past[5][j]
                o = _swa(q3, _dup(k_past), kd3, _dup(v_past), vd3, p['swa_sinks'][j], t, False)
                k_new = jnp.concatenate([k_past, _undup(kd3)], axis=1)[:, t:]
                v_new = jnp.concatenate([v_past, _undup(vd3)], axis=1)[:, t:]
            x = _out_proj(x, mods, [o.reshape(n, 1024)], [w['odd_out'][j]], rows)
            odd_states.append((k_new, v_new))
        fg = p['final_norm_g'] if i == depth - 1 else None
        x = _ffn(x, mods, p['norm_g'][i, 2], w['ffn_in'][i, 1], w['ffn_out'][i, 1], 2, rows, final_g=fg)
    es = [jnp.stack([st[k] for st in even_states]) for k in range(4)]
    os_ = [jnp.stack([st[k] for st in odd_states]) for k in range(2)]
    return x.reshape(b, t, d), es + os_


def kernel(x_prompt, x_sample, cache_mla_ckv, cache_mla_krope, state_rwkv, state_rwkv_shift, cache_swa_k, cache_swa_v, c_prompt, c_sample, w_ada, b_ada, norm_g, ffn_w_in, ffn_w_out, even_w_in, even_w_out, mla_q_norm, mla_kv_norm, mla_w_uq, mla_w_ukv, rw_mu, rw_w0, rw_w2, rw_a0, rw_a2, rw_g2, rw_k_k, rw_k_a, rw_r_k, rw_ln_w, rw_ln_b, odd_w_qkv, odd_b_qkv, odd_w_out, swa_sinks, final_norm_g):
    p = dict(w_ada=w_ada, b_ada=b_ada, norm_g=norm_g, ffn_w_in=ffn_w_in, ffn_w_out=ffn_w_out,
             even_w_in=even_w_in, even_w_out=even_w_out, mla_q_norm=mla_q_norm, mla_kv_norm=mla_kv_norm,
             mla_w_uq=mla_w_uq, mla_w_ukv=mla_w_ukv, rw_mu=rw_mu, rw_w0=rw_w0, rw_w2=rw_w2, rw_a0=rw_a0,
             rw_a2=rw_a2, rw_g2=rw_g2, rw_k_k=rw_k_k, rw_k_a=rw_k_a, rw_r_k=rw_r_k, rw_ln_w=rw_ln_w,
             rw_ln_b=rw_ln_b, odd_w_qkv=odd_w_qkv, odd_b_qkv=odd_b_qkv, odd_w_out=odd_w_out,
             swa_sinks=swa_sinks, final_norm_g=final_norm_g)
    w = _prep_weights(p)
    depth = w_ada.shape[0]
    bp, bs = c_prompt.shape[0], c_sample.shape[0]
    d = c_prompt.shape[1]
    b_pad = -(-(bp + bs) // 8) * 8
    c_all = jnp.concatenate([c_prompt, c_sample, jnp.zeros((b_pad - bp - bs, d), F32)], axis=0)
    mods = _ada(c_all, w_ada, b_ada).reshape(depth, b_pad, 3 * N_SUB, d)
    y_prompt, sp = _trunk(x_prompt, mods[:, :bp], 0, None, p, w)
    past = (cache_mla_ckv, cache_mla_krope, state_rwkv, state_rwkv_shift, cache_swa_k, cache_swa_v)
    y_sample, ss = _trunk(x_sample, mods[:, bp:bp + bs], cache_mla_ckv.shape[2], past, p, w)
    return (y_prompt, y_sample, sp[0], sp[1], sp[2], sp[3], sp[4], sp[5],
            ss[0], ss[1], ss[2], ss[3], ss[4], ss[5])
```

```python
import functools
import math

import jax
import jax.numpy as jnp
import numpy as np
from jax import lax
from jax.experimental import pallas as pl
from jax.experimental.pallas import tpu as pltpu

F32 = jnp.float32
BF16 = jnp.bfloat16

CHUNK = 64
EPS = 1e-6
NEG = -1e30
N_SUB = 3
MLA_HEADS = 8
MLA_NOPE = 64
MLA_ROPE = 32
MLA_V = 64
MLA_Q_LORA = 768
MLA_KV_LORA = 256
MLA_SCALE = (MLA_NOPE + MLA_ROPE) ** -0.5
LOG2E = math.log2(math.e)
ROPE_BASE = 10000.0
RW_HEADS = 8
RW_N = 64
RW_C = RW_HEADS * RW_N
RW_GN_EPS = 64e-5
RW_COLS = 3 * RW_C + 64 + 64 + 128
SW_HEADS = 16
SW_KV_HEADS = 4
SW_GROUP = 4
SW_HD = 64
WINDOW = 128

LANES = 128
VMEM_LIMIT = 56 * 1024 * 1024


def _cparams(sem):
    return pltpu.CompilerParams(dimension_semantics=sem, vmem_limit_bytes=VMEM_LIMIT)


def _const_spec(shape):
    nd = len(shape)
    return pl.BlockSpec(shape, lambda *_: (0,) * nd, pipeline_mode=pl.Buffered(1))


def _dot(a, b):
    return jnp.dot(a, b, preferred_element_type=F32)


def _dot_nt(a, b):
    return lax.dot_general(a, b, (((1,), (1,)), ((), ())), preferred_element_type=F32)


def _split2(x):
    hi = x.astype(BF16)
    lo = (x - hi.astype(F32)).astype(BF16)
    return hi, lo


def _mm3(a, b, nt=False):
    d = _dot_nt if nt else _dot
    ah, al = _split2(a)
    bh, bl = _split2(b)
    return (d(ah, bl) + d(al, bh)) + d(ah, bh)


def _norm_mod(x, g, sh, sc, nb):
    y = x * lax.rsqrt(jnp.mean(x * x, axis=-1, keepdims=True) + EPS) * g
    if nb == 1:
        return y * (1.0 + sc) + sh
    tm, d = x.shape
    y3 = y.reshape(nb, tm // nb, d)
    return (y3 * (1.0 + sc[:, None, :]) + sh[:, None, :]).reshape(tm, d)


def _gate_rows(g, y, nb):
    if nb == 1:
        return g * y
    tm, d = y.shape
    return (y.reshape(nb, tm // nb, d) * g[:, None, :]).reshape(tm, d)


def _row_tiling(n_rows, rows_per_seq, pref):
    if rows_per_seq >= pref:
        assert rows_per_seq % pref == 0
        return pref, 1
    tm = min(pref, n_rows)
    assert tm % rows_per_seq == 0 and n_rows % tm == 0
    return tm, tm // rows_per_seq


def _mods_spec(tm, nb, rows_per_seq):
    if nb == 1:
        tiles_per_seq = rows_per_seq // tm
        return pl.BlockSpec((1, 3 * N_SUB, 1024), lambda i: (i // tiles_per_seq, 0, 0))
    return pl.BlockSpec((nb, 3 * N_SUB, 1024), lambda i: (i, 0, 0))


def _ada_kernel(c_ref, w_ref, b_ref, o_ref):
    c = c_ref[...]
    cs = (c * jax.nn.sigmoid(c)).astype(BF16)
    o_ref[0] = _dot(cs, w_ref[0].astype(BF16)) + b_ref[0]


def _ada(c_all, w_ada, b_ada):
    depth, d, n = w_ada.shape
    bp = c_all.shape[0]
    tn = n // 4
    return pl.pallas_call(
        _ada_kernel,
        grid=(depth, n // tn),
        in_specs=[pl.BlockSpec((bp, d), lambda l, j: (0, 0)),
                  pl.BlockSpec((1, d, tn), lambda l, j: (l, 0, j)),
                  pl.BlockSpec((1, 1, tn), lambda l, j: (l, 0, j))],
        out_specs=pl.BlockSpec((1, bp, tn), lambda l, j: (l, 0, j)),
        out_shape=jax.ShapeDtypeStruct((depth, bp, n), F32),
        compiler_params=_cparams(("parallel", "parallel")),
    )(c_all, w_ada, b_ada.reshape(depth, 1, n))


FF_CHUNK = 256


def _ffn_kernel(x_ref, mods_ref, ng_ref, win_ref, wout_ref, *rest, sub, nb, d_ff, final):
    if final:
        fg_ref, o_ref, a_scr = rest
    else:
        o_ref, a_scr = rest
    x = x_ref[...]
    sh = mods_ref[:, 3 * sub, :]
    sc = mods_ref[:, 3 * sub + 1, :]
    gt = mods_ref[:, 3 * sub + 2, :]
    h = _norm_mod(x, ng_ref[...], sh, sc, nb).astype(BF16)
    for c in range(d_ff // FF_CHUNK):
        lo = c * FF_CHUNK
        g = _dot(h, win_ref[:, lo:lo + FF_CHUNK])
        u = _dot(h, win_ref[:, d_ff + lo:d_ff + lo + FF_CHUNK])
        a_scr[:, lo:lo + FF_CHUNK] = (g * jax.nn.sigmoid(g) * u).astype(BF16)
    y = _dot(a_scr[...], wout_ref[...])
    out = x + _gate_rows(0.5 * gt, y, nb)
    if final:
        out = out * lax.rsqrt(jnp.mean(out * out, axis=-1, keepdims=True) + EPS) * fg_ref[...]
    o_ref[...] = out


def _ffn(x, mods, norm_g, w_in, w_out, sub, rows_per_seq, final_g=None):
    n, d = x.shape
    d_ff = w_out.shape[0]
    tm, nb = _row_tiling(n, rows_per_seq, 512)
    final = final_g is not None
    in_specs = [pl.BlockSpec((tm, d), lambda i: (i, 0)),
                _mods_spec(tm, nb, rows_per_seq),
                _const_spec((1, d)), _const_spec(w_in.shape), _const_spec(w_out.shape)]
    args = [x, mods, norm_g.reshape(1, d), w_in, w_out]
    if final:
        in_specs.append(_const_spec((1, d)))
        args.append(final_g.reshape(1, d))
    return pl.pallas_call(
        functools.partial(_ffn_kernel, sub=sub, nb=nb, d_ff=d_ff, final=final),
        grid=(n // tm,),
        in_specs=in_specs,
        out_specs=pl.BlockSpec((tm, d), lambda i: (i, 0)),
        out_shape=jax.ShapeDtypeStruct((n, d), F32),
        scratch_shapes=[pltpu.VMEM((tm, d_ff), BF16)],
        compiler_params=_cparams(("parallel",)),
    )(*args)


EVEN_W = 768 + 256 + 1792 + 128
_RW_OFF = 1056
_RW_SEGS = [(0, 512), (576, 1088), (1088, 1600), (512, 576), (1600, 1664), (1664, 1792)]
_RW_INV_SEGS = [(0, 512), (1536, 1600), (512, 1024), (1024, 1536), (1600, 1664), (1664, 1792)]


def _rw_permute(a):
    return jnp.concatenate([a[..., s:e] for s, e in _RW_SEGS], axis=-1)


def _rw_unpermute(a):
    return jnp.concatenate([a[..., s:e] for s, e in _RW_INV_SEGS], axis=-1)


def _rope_slot(v, c, s1, s2):
    w = v.shape[-1]
    return v * c + pltpu.roll(v, w - 16, axis=1) * s1 + pltpu.roll(v, 16, axis=1) * s2


def _even_in_kernel(x_ref, mods_ref, ng_ref, w_ref, qn_ref, kvn_ref, wuq_ref,
                    cq_ref, s1q_ref, s2q_ref, ck_ref, s1k_ref, s2k_ref,
                    q_out, ckv_out, kr_out, prw_out, *, nb):
    x = x_ref[...]
    h = _norm_mod(x, ng_ref[...], mods_ref[:, 3, :], mods_ref[:, 4, :], nb).astype(BF16)
    cq = _dot(h, w_ref[:, 0:768])
    cqn = (cq * lax.rsqrt(jnp.mean(cq * cq, axis=-1, keepdims=True) + EPS) * qn_ref[...]).astype(BF16)
    q = _dot(cqn, wuq_ref[...])
    rep = lambda t: jnp.concatenate([t] * MLA_HEADS, axis=1)
    q = _rope_slot(q, rep(cq_ref[...]), rep(s1q_ref[...]), rep(s2q_ref[...]))
    q_out[...] = (q * (MLA_SCALE * LOG2E)).astype(BF16)
    ckv = _dot(h, w_ref[:, 768:1024])
    ckv_out[...] = ckv * lax.rsqrt(jnp.mean(ckv * ckv, axis=-1, keepdims=True) + EPS) * kvn_ref[...]
    prw_out[...] = _dot(h, w_ref[:, 1024:2816])
    krs = _dot(h, w_ref[:, 2816:2944])
    krs = _rope_slot(krs, ck_ref[...], s1k_ref[...], s2k_ref[...])
    kr_out[...] = krs[:, 0:MLA_ROPE]


def _even_in(x, mods, norm_g, w_perm, q_norm, kv_norm, wuq_slot, tabs, rows_per_seq):
    n, d = x.shape
    tm, nb = _row_tiling(n, rows_per_seq, 512)
    ttab = tabs[0].shape[0]
    ntab = ttab // tm
    tab_spec = pl.BlockSpec((tm, LANES), lambda i: (i % ntab, 0))
    row = lambda w: pl.BlockSpec((tm, w), lambda i: (i, 0))
    return pl.pallas_call(
        functools.partial(_even_in_kernel, nb=nb),
        grid=(n // tm,),
        in_specs=[row(d), _mods_spec(tm, nb, rows_per_seq), _const_spec((1, d)),
                  _const_spec(w_perm.shape), _const_spec((1, 768)), _const_spec((1, 256)),
                  _const_spec(wuq_slot.shape)] + [tab_spec] * 6,
        out_specs=[row(1024), row(256), row(MLA_ROPE), row(RW_COLS)],
        out_shape=[jax.ShapeDtypeStruct((n, 1024), BF16), jax.ShapeDtypeStruct((n, 256), F32),
                   jax.ShapeDtypeStruct((n, MLA_ROPE), F32), jax.ShapeDtypeStruct((n, RW_COLS), F32)],
        compiler_params=_cparams(("parallel",)),
    )(x, mods, norm_g.reshape(1, d), w_perm, q_norm.reshape(1, 768), kv_norm.reshape(1, 256),
      wuq_slot, *tabs)


def _rope_tables(pos, tile_to):
    half = MLA_ROPE // 2
    freqs = ROPE_BASE ** (-jnp.arange(half, dtype=F32) / half)
    ang = pos.astype(F32)[:, None] * freqs[None, :]
    cos, sin = jnp.cos(ang), jnp.sin(ang)
    t = pos.shape[0]
    z = lambda w: jnp.zeros((t, w), F32)
    o = lambda w: jnp.ones((t, w), F32)
    cq = jnp.concatenate([o(64), cos, cos, z(32)], axis=1)
    s1q = jnp.concatenate([z(64), -sin, z(48)], axis=1)
    s2q = jnp.concatenate([z(80), sin, z(32)], axis=1)
    ck = jnp.concatenate([cos, cos, z(96)], axis=1)
    s1k = jnp.concatenate([-sin, z(112)], axis=1)
    s2k = jnp.concatenate([z(16), sin, z(96)], axis=1)
    tabs = [cq, s1q, s2q, ck, s1k, s2k]
    if tile_to > t:
        tabs = [jnp.tile(a, (tile_to // t, 1)) for a in tabs]
    return tabs


def _kv_expand_kernel(ckv_ref, kr_ref, wk_ref, sel_ref, wv_ref, one_ref, k_out, v_out):
    c = ckv_ref[...].astype(BF16)
    k = _dot(c, wk_ref[...]) + _dot(kr_ref[...].astype(BF16), sel_ref[...])
    k_out[...] = k.astype(BF16)
    v_out[...] = (_dot(c, wv_ref[...]) + one_ref[...]).astype(BF16)


def _kv_expand(ckv, kr, wk_slot, sel, wv_slot, one_slot):
    n = ckv.shape[0]
    tm = min(1024, n)
    assert n % tm == 0
    row = lambda w: pl.BlockSpec((tm, w), lambda i: (i, 0))
    return pl.pallas_call(
        _kv_expand_kernel,
        grid=(n // tm,),
        in_specs=[row(256), row(MLA_ROPE), _const_spec(wk_slot.shape), _const_spec(sel.shape),
                  _const_spec(wv_slot.shape), _const_spec(one_slot.shape)],
        out_specs=[row(1024), row(1024)],
        out_shape=[jax.ShapeDtypeStruct((n, 1024), BF16), jax.ShapeDtypeStruct((n, 1024), BF16)],
        compiler_params=_cparams(("parallel",)),
    )(ckv, kr, wk_slot, sel, wv_slot, one_slot)


def _mla_attn_kernel(qi_ref, ki_ref, fl_ref, q_ref, k_ref, v_ref, o_ref, m_scr, acc_scr,
                     *, tq, tk, q_off):
    p_id = pl.program_id(1)
    flags = fl_ref[p_id]
    first = (flags & 1) != 0
    last = (flags & 2) != 0
    masked = (flags & 4) != 0

    @pl.when(first)
    def _():
        m_scr[...] = jnp.full(m_scr.shape, NEG, F32)
        acc_scr[...] = jnp.zeros(acc_scr.shape, F32)

    rb = min(MLA_ROW_BLOCK, tq)
    sb = min(MLA_SUB_BLOCK, rb)
    n_lt = tk // LANES

    insts = [(h, r0) for h in range(MLA_HEADS) for r0 in range(0, tq, rb)]

    def scores(h, r0):
        sl = slice(h * LANES, (h + 1) * LANES)
        return _dot_nt(q_ref[0, r0:r0 + rb, sl], k_ref[0, :, sl])

    def body(use_mask):
        if use_mask:
            q_chunk0 = (q_off + qi_ref[p_id] * tq) // CHUNK
            k_chunk0 = (ki_ref[p_id] * tk) // CHUNK
            kc = k_chunk0 + lax.broadcasted_iota(jnp.int32, (sb, tk), 1) // CHUNK
        s_next = scores(*insts[0])
        for idx, (h, r0) in enumerate(insts):
            s = s_next
            if idx + 1 < len(insts):
                s_next = scores(*insts[idx + 1])
            ps, alphas = [], []
            for r1 in range(0, rb, sb):
                rows = slice(r0 + r1, r0 + r1 + sb)
                s_r = s[r1:r1 + sb]
                if use_mask:
                    s_r = jnp.where(kc <= q_chunk0 + (r0 + r1) // CHUNK, s_r, NEG)
                mx = s_r[:, 0:LANES]
                for j in range(1, n_lt):
                    mx = jnp.maximum(mx, s_r[:, j * LANES:(j + 1) * LANES])
                m_prev = m_scr[h, rows, :]
                m_new = jnp.maximum(m_prev, jnp.max(mx, axis=-1, keepdims=True))
                m_scr[h, rows, :] = m_new
                alphas.append(jnp.exp2(m_prev - m_new))
                ps.append(jnp.exp2(s_r - jnp.concatenate([m_new] * n_lt, axis=1)).astype(BF16))
            p = jnp.concatenate(ps, axis=0) if len(ps) > 1 else ps[0]
            alpha = jnp.concatenate(alphas, axis=0) if len(alphas) > 1 else alphas[0]
            v_h = v_ref[0, :, h * LANES:(h + 1) * LANES]
            acc_scr[h, r0:r0 + rb, :] = alpha * acc_scr[h, r0:r0 + rb, :] + _dot(p, v_h)

    @pl.when(masked)
    def _():
        body(True)

    @pl.when(jnp.logical_not(masked))
    def _():
        body(False)

    @pl.when(last)
    def _():
        lane = lax.broadcasted_iota(jnp.int32, (tq, LANES), 1)
        for pr in range(MLA_HEADS // 2):
            ae = acc_scr[2 * pr]
            ao = acc_scr[2 * pr + 1]
            oe = ae / ae[:, MLA_V:MLA_V + 1]
            oo = ao / ao[:, MLA_V:MLA_V + 1]
            o_ref[0, :, pr * LANES:(pr + 1) * LANES] = jnp.where(
                lane < MLA_V, oe, pltpu.roll(oo, MLA_V, axis=1)).astype(BF16)


MLA_ROW_BLOCK = 256
MLA_SUB_BLOCK = 32


def _mla_attn(q, k, v, pairs, tq, tk, q_off):
    b, t_q, _ = q.shape
    qi, ki, fl = pairs
    grid_spec = pltpu.PrefetchScalarGridSpec(
        num_scalar_prefetch=3,
        grid=(b, qi.shape[0]),
        in_specs=[pl.BlockSpec((1, tq, 1024), lambda bb, p, qi, ki, fl: (bb, qi[p], 0)),
                  pl.BlockSpec((1, tk, 1024), lambda bb, p, qi, ki, fl: (bb, ki[p], 0)),
                  pl.BlockSpec((1, tk, 1024), lambda bb, p, qi, ki, fl: (bb, ki[p], 0))],
        out_specs=pl.BlockSpec((1, tq, 512), lambda bb, p, qi, ki, fl: (bb, qi[p], 0)),
        scratch_shapes=[pltpu.VMEM((MLA_HEADS, tq, LANES), F32),
                        pltpu.VMEM((MLA_HEADS, tq, LANES), F32)])
    return pl.pallas_call(
        functools.partial(_mla_attn_kernel, tq=tq, tk=tk, q_off=q_off),
        grid_spec=grid_spec,
        out_shape=jax.ShapeDtypeStruct((b, t_q, 512), BF16),
        compiler_params=_cparams(("parallel", "arbitrary")),
    )(qi, ki, fl, q, k, v)


def _causal_pairs(nq):
    qi, ki, fl = [], [], []
    for a in range(nq):
        for c in range(a + 1):
            qi.append(a)
            ki.append(c)
            fl.append((1 if c == 0 else 0) | (6 if c == a else 0))
    return tuple(jnp.asarray(np.array(z, np.int32)) for z in (qi, ki, fl))


def _full_pairs(nk):
    qi = [0] * nk
    ki = list(range(nk))
    fl = [(1 if c == 0 else 0) | (6 if c == nk - 1 else 0) for c in range(nk)]
    return tuple(jnp.asarray(np.array(z, np.int32)) for z in (qi, ki, fl))


def _pair_sum(x, ones_bd):
    hi, lo = _split2(x)
    return _dot(hi, ones_bd) + _dot(lo, ones_bd)


def _rwkv_kernel(pr_ref, sh0_ref, s0_ref, mu_ref, w0_ref, wl_ref, a0_ref, al_ref, g2_ref,
                 kk_ref, ka_ref, rk_ref, lnw_ref, lnb_ref, y_ref, st_ref,
                 prev_scr, s_scr, rt_scr, at_scr, bt_scr, kt_scr, bv_scr, k2_scr, v_scr,
                 cum_scr, yc_scr, *, tt):
    t_id = pl.program_id(1)
    n_pairs = RW_HEADS // 2

    @pl.when(t_id == 0)
    def _():
        s_scr[...] = s0_ref[0]
        prev_scr[0:1, :] = sh0_ref[0]

    pr = pr_ref[0]
    row = lax.broadcasted_iota(jnp.int32, (tt, 1), 0)
    prev = jnp.where(row == 0, prev_scr[0:1, :], pltpu.roll(pr, 1, axis=0))
    prev_scr[0:1, :] = pr[tt - 1:tt, :]
    pm = pr + (prev - pr) * mu_ref[...]
    r = pm[:, 0:512]
    k = pm[:, 512:1024]
    v = pm[:, 1024:1536]
    wa = pm[:, 1536:1664]
    g_in = pm[:, 1664:1792]
    z = w0_ref[...] + _dot(jnp.tanh(wa).astype(BF16), wl_ref[...])
    nz = -z
    w = -(jnp.maximum(nz, 0.0) + jnp.log1p(jnp.exp(-jnp.abs(nz)))) - 0.5
    ld = -jnp.exp(w)
    a_sig = jax.nn.sigmoid(a0_ref[...] + _dot(wa.astype(BF16), al_ref[...]))
    g = _dot(jax.nn.sigmoid(g_in).astype(BF16), g2_ref[...])

    li = lax.broadcasted_iota(jnp.int32, (LANES, LANES), 0)
    lj = lax.broadcasted_iota(jnp.int32, (LANES, LANES), 1)
    ones_bd = jnp.where((li // RW_N) == (lj // RW_N), 1.0, 0.0).astype(BF16)

    def head_sum(x):
        return jnp.concatenate([_pair_sum(x[:, p * LANES:(p + 1) * LANES], ones_bd)
                                for p in range(n_pairs)], axis=1)

    kk = k * kk_ref[...]
    kk = kk / jnp.maximum(jnp.sqrt(head_sum(kk * kk)), 1e-12)
    k2 = k * (1.0 + (a_sig - 1.0) * ka_ref[...])
    bonus = head_sum(r * k2 * rk_ref[...]) * v

    ti = lax.broadcasted_iota(jnp.int32, (tt, tt), 0)
    tj = lax.broadcasted_iota(jnp.int32, (tt, tt), 1)
    tri = jnp.where(((ti // CHUNK) == (tj // CHUNK)) & (tj <= ti), 1.0, 0.0).astype(BF16)
    l1 = ld.astype(BF16)
    rem = ld - l1.astype(F32)
    l2 = rem.astype(BF16)
    l3 = (rem - l2.astype(F32)).astype(BF16)
    cum = (_dot(tri, l3) + _dot(tri, l2)) + _dot(tri, l1)
    winv = jnp.exp(-cum)
    rt_scr[...] = r * jnp.exp(cum)
    at_scr[...] = -kk * jnp.exp(cum - ld)
    bv = kk * a_sig
    bt_scr[...] = bv * winv
    kt_scr[...] = k2 * winv
    bv_scr[...] = bv
    k2_scr[...] = k2
    v_scr[...] = v
    cum_scr[...] = cum

    lane = lax.broadcasted_iota(jnp.int32, (CHUNK, LANES), 1)
    even = lane < RW_N
    strict = (lj % RW_N) < (li % RW_N)
    incl = (lj % RW_N) <= (li % RW_N)
    eye = jnp.where(li == lj, 1.0, 0.0).astype(F32)
    pairs = range(n_pairs)

    def same_block(m):
        return (li // m) == (lj // m)

    def stack_f32(x):
        return jnp.concatenate([jnp.where(even, x, 0.0), jnp.where(even, 0.0, x)], axis=0)

    def stack(x):
        return stack_f32(x).astype(BF16)

    def unit_lower_inverse(n_list):
        n8 = [jnp.where(same_block(8), n, 0.0) for n in n_list]
        t = [eye + a for a in n8]
        p2 = [_mm3(a, a) for a in n8]
        t = [a + _mm3(a, b) for a, b in zip(t, p2)]
        p4 = [_mm3(a, a) for a in p2]
        t = [a + _mm3(a, b) for a, b in zip(t, p4)]
        for m in (8, 16, 32):
            off = same_block(2 * m) & jnp.logical_not(same_block(m))
            x = [_mm3(jnp.where(off, n, 0.0), a) for n, a in zip(n_list, t)]
            t = [a + _mm3(a, b) for a, b in zip(t, x)]
        return t

    def chunk_body(c, carry):
        rows = pl.ds(pl.multiple_of(c * CHUNK, CHUNK), CHUNK)
        sls = [slice(p * LANES, (p + 1) * LANES) for p in pairs]
        cum_c = [cum_scr[rows, sl] for sl in sls]
        cum_l = [a[CHUNK - 1:CHUNK, :] for a in cum_c]
        w2 = [jnp.exp(a - b) for a, b in zip(cum_l, cum_c)]
        rs = [stack(rt_scr[rows, sl]) for sl in sls]
        as_ = [stack(at_scr[rows, sl]) for sl in sls]
        bs = [stack(bt_scr[rows, sl]) for sl in sls]
        ks = [stack(kt_scr[rows, sl]) for sl in sls]
        vs = [stack(v_scr[rows, sl]) for sl in sls]
        b2s = [stack(bv_scr[rows, sl] * w) for sl, w in zip(sls, w2)]
        k2s = [stack(k2_scr[rows, sl] * w) for sl, w in zip(sls, w2)]
        n_m = [jnp.where(strict, _dot_nt(a, b), 0.0) for a, b in zip(as_, bs)]
        mk = [jnp.where(strict, _dot_nt(a, b), 0.0).astype(BF16) for a, b in zip(as_, ks)]
        cb = [jnp.where(incl, _dot_nt(a, b), 0.0).astype(BF16) for a, b in zip(rs, bs)]
        ck = [jnp.where(incl, _dot_nt(a, b), 0.0).astype(BF16) for a, b in zip(rs, ks)]
        mv = [_dot(a, b) for a, b in zip(mk, vs)]
        cv = [_dot(a, b) for a, b in zip(ck, vs)]
        vst = [stack_f32(v_scr[rows, sl]).T.astype(BF16) for sl in sls]
        vk = [_dot(a, b) for a, b in zip(vst, k2s)]
        tinv = [a.astype(BF16) for a in unit_lower_inverse(n_m)]
        s = [s_scr[p] for p in pairs]
        sb = [a.astype(BF16) for a in s]
        x = [_dot_nt(a, b) + c_ for a, b, c_ in zip(as_, sb, mv)]
        u = [_dot(a, b.astype(BF16)) for a, b in zip(tinv, x)]
        ub = [a.astype(BF16) for a in u]
        ys = [_dot_nt(a, b) + _dot(c_, d) + e for a, b, c_, d, e in zip(rs, sb, cb, ub, cv)]
        for p in pairs:
            yc_scr[rows, sls[p]] = ys[p][0:CHUNK] + ys[p][CHUNK:2 * CHUNK]
            s_scr[p] = s[p] * jnp.exp(cum_l[p]) + _dot(u[p].T.astype(BF16), b2s[p]) + vk[p]
        return carry

    lax.fori_loop(0, tt // CHUNK, chunk_body, 0)

    y = yc_scr[...]
    mean = head_sum(y) * (1.0 / RW_N)
    dlt = y - mean
    var = head_sum(dlt * dlt) * (1.0 / RW_N)
    yn = dlt * lax.rsqrt(var + RW_GN_EPS) * lnw_ref[...] + lnb_ref[...]
    y_ref[0] = ((yn + bonus) * g).astype(BF16)

    @pl.when(t_id == pl.num_programs(1) - 1)
    def _():
        st_ref[0] = s_scr[...]


def _rwkv(prw, sh0, s0_bd, wts):
    b, t, _ = prw.shape
    tt = min(256, t)
    assert t % tt == 0 and tt % CHUNK == 0
    c512 = _const_spec((1, RW_C))
    scr = lambda: pltpu.VMEM((tt, RW_C), F32)
    return pl.pallas_call(
        functools.partial(_rwkv_kernel, tt=tt),
        grid=(b, t // tt),
        in_specs=[pl.BlockSpec((1, tt, RW_COLS), lambda bb, i: (bb, i, 0)),
                  pl.BlockSpec((1, 1, RW_COLS), lambda bb, i: (bb, 0, 0)),
                  pl.BlockSpec((1, 4, LANES, LANES), lambda bb, i: (bb, 0, 0, 0)),
                  _const_spec((1, RW_COLS)), c512, _const_spec((LANES, RW_C)), c512,
                  _const_spec((LANES, RW_C)), _const_spec((LANES, RW_C)),
                  c512, c512, c512, c512, c512],
        out_specs=[pl.BlockSpec((1, tt, RW_C), lambda bb, i: (bb, i, 0)),
                   pl.BlockSpec((1, 4, LANES, LANES), lambda bb, i: (bb, 0, 0, 0))],
        out_shape=[jax.ShapeDtypeStruct((b, t, RW_C), BF16),
                   jax.ShapeDtypeStruct((b, 4, LANES, LANES), F32)],
        scratch_shapes=[pltpu.VMEM((8, RW_COLS), F32), pltpu.VMEM((4, LANES, LANES), F32)]
                       + [scr() for _ in range(9)],
        compiler_params=_cparams(("parallel", "arbitrary")),
    )(prw, sh0, s0_bd, *wts)


def _state_to_bd(s):
    b = s.shape[0]
    s = s.reshape(b, 4, 2, RW_N, RW_N)
    z = jnp.zeros_like(s[:, :, 0])
    top = jnp.concatenate([s[:, :, 0], z], axis=-1)
    bot = jnp.concatenate([z, s[:, :, 1]], axis=-1)
    return jnp.concatenate([top, bot], axis=-2)


def _state_from_bd(s):
    b = s.shape[0]
    return jnp.stack([s[:, :, :RW_N, :RW_N], s[:, :, RW_N:, RW_N:]], axis=2).reshape(b, RW_HEADS, RW_N, RW_N)


def _out_proj_kernel(*refs, n_in, nb):
    x_ref, mods_ref = refs[0], refs[1]
    a_refs = refs[2:2 + n_in]
    w_refs = refs[2 + n_in:2 + 2 * n_in]
    o_ref = refs[2 + 2 * n_in]
    y = _dot(a_refs[0][...], w_refs[0][...])
    for a, w in zip(a_refs[1:], w_refs[1:]):
        y = y + _dot(a[...], w[...])
    o_ref[...] = x_ref[...] + _gate_rows(mods_ref[:, 5, :], y, nb)


def _out_proj(x, mods, acts, ws, rows_per_seq):
    n, d = x.shape
    tm, nb = _row_tiling(n, rows_per_seq, 512)
    row = lambda w: pl.BlockSpec((tm, w), lambda i: (i, 0))
    return pl.pallas_call(
        functools.partial(_out_proj_kernel, n_in=len(acts), nb=nb),
        grid=(n // tm,),
        in_specs=[row(d), _mods_spec(tm, nb, rows_per_seq)] + [row(a.shape[1]) for a in acts]
                 + [_const_spec(w.shape) for w in ws],
        out_specs=row(d),
        out_shape=jax.ShapeDtypeStruct((n, d), F32),
        compiler_params=_cparams(("parallel",)),
    )(x, mods, *acts, *ws)


_ODD_SEGS = ([(1024 + g * 64, 1088 + g * 64) for g in range(SW_KV_HEADS) for _ in range(2)]
             + [(1280 + g * 64, 1344 + g * 64) for g in range(SW_KV_HEADS) for _ in range(2)])


def _odd_in_kernel(x_ref, mods_ref, ng_ref, w_ref, b_ref, q_out, k_out, v_out, *, nb):
    h = _norm_mod(x_ref[...], ng_ref[...], mods_ref[:, 3, :], mods_ref[:, 4, :], nb).astype(BF16)
    q = _dot(h, w_ref[:, 0:1024]) + b_ref[:, 0:1024]
    q_out[...] = (q * (SW_HD ** -0.5)).astype(BF16)
    k_out[...] = _dot(h, w_ref[:, 1024:1536]) + b_ref[:, 1024:1536]
    v_out[...] = _dot(h, w_ref[:, 1536:2048]) + b_ref[:, 1536:2048]


def _odd_in(x, mods, norm_g, w_perm, b_perm, rows_per_seq):
    n, d = x.shape
    tm, nb = _row_tiling(n, rows_per_seq, 512)
    row = lambda w: pl.BlockSpec((tm, w), lambda i: (i, 0))
    return pl.pallas_call(
        functools.partial(_odd_in_kernel, nb=nb),
        grid=(n // tm,),
        in_specs=[row(d), _mods_spec(tm, nb, rows_per_seq), _const_spec((1, d)),
                  _const_spec(w_perm.shape), _const_spec((1, 2048))],
        out_specs=[row(1024), row(512), row(512)],
        out_shape=[jax.ShapeDtypeStruct((n, 1024), BF16), jax.ShapeDtypeStruct((n, 512), F32),
                   jax.ShapeDtypeStruct((n, 512), F32)],
        compiler_params=_cparams(("parallel",)),
    )(x, mods, norm_g.reshape(1, d), w_perm, b_perm.reshape(1, 2048))


def _swa_kernel(sinks_ref, q_ref, kp_ref, kc_ref, vp_ref, vc_ref, o_ref, *, tq, mask_first_prev):
    nk = WINDOW + tq
    qi = lax.broadcasted_iota(jnp.int32, (tq, nk), 0)
    kj = lax.broadcasted_iota(jnp.int32, (tq, nk), 1)
    kc = kj // CHUNK - WINDOW // CHUNK
    qc = qi // CHUNK
    vis = (kc <= qc) & (kc >= qc - WINDOW // CHUNK)
    if mask_first_prev:
        vis = vis & ((kj >= WINDOW) | (pl.program_id(1) > 0))
    dist = jnp.abs(qi + WINDOW - kj).astype(F32)
    keys = jnp.concatenate([kp_ref[0], kc_ref[0]], axis=0).astype(BF16)
    vals = jnp.concatenate([vp_ref[0], vc_ref[0]], axis=0).astype(BF16)
    lane = lax.broadcasted_iota(jnp.int32, (tq, LANES), 1)
    low = lane < SW_HD
    outs = []
    for h in range(SW_HEADS):
        g = h // SW_GROUP
        slope = 2.0 ** (-8.0 * (h + 1) / SW_HEADS)
        qp = q_ref[0, :, (h // 2) * LANES:(h // 2 + 1) * LANES]
        qh = jnp.where(low if h % 2 == 0 else jnp.logical_not(low), qp, jnp.zeros_like(qp))
        s = _dot_nt(qh, keys[:, g * LANES:(g + 1) * LANES])
        s = jnp.where(vis, s - slope * dist, NEG)
        sk = sinks_ref[h]
        m = jnp.maximum(jnp.max(s, axis=-1, keepdims=True), sk)
        e = jnp.exp(s - m)
        p = e / (jnp.sum(e, axis=-1, keepdims=True) + jnp.exp(sk - m))
        outs.append(_dot(p.astype(BF16), vals[:, g * LANES:(g + 1) * LANES]))
    for pr in range(SW_HEADS // 2):
        o_ref[0, :, pr * LANES:(pr + 1) * LANES] = jnp.where(low, outs[2 * pr], outs[2 * pr + 1]).astype(BF16)


def _swa(q, k_prev, k_cur, v_prev, v_cur, sinks, tq, same_array):
    b, t, _ = q.shape
    nt = t // tq
    per = tq // WINDOW
    if same_array:
        prev_map = lambda bb, i: (bb, jnp.maximum(i * per - 1, 0), 0)
    else:
        prev_map = lambda bb, i: (bb, 0, 0)
    cur = lambda w: pl.BlockSpec((1, tq, w), lambda bb, i: (bb, i, 0))
    prev = pl.BlockSpec((1, WINDOW, 512), prev_map)
    return pl.pallas_call(
        functools.partial(_swa_kernel, tq=tq, mask_first_prev=same_array),
        grid=(b, nt),
        in_specs=[pl.BlockSpec(memory_space=pltpu.SMEM), cur(1024), prev, cur(512), prev, cur(512)],
        out_specs=cur(1024),
        out_shape=jax.ShapeDtypeStruct((b, t, 1024), BF16),
        compiler_params=_cparams(("parallel", "parallel")),
    )(sinks, q, k_prev, k_cur, v_prev, v_cur)


def _undup(a):
    return a.reshape(a.shape[:-1] + (SW_KV_HEADS, 2, SW_HD))[..., 0, :]


def _dup(a):
    return jnp.concatenate([a, a], axis=-1).reshape(a.shape[:-2] + (512,))


def _prep_weights(p):
    depth = p['w_ada'].shape[0]
    n_even, n_odd = (depth + 1) // 2, depth // 2
    w = {}
    w['ffn_in'] = p['ffn_w_in'].astype(BF16)
    w['ffn_out'] = p['ffn_w_out'].astype(BF16)
    wi = p['even_w_in'].astype(BF16)
    w['even_in'] = jnp.concatenate(
        [wi[:, :, 0:1024]] + [wi[:, :, _RW_OFF + a:_RW_OFF + b] for a, b in _RW_SEGS]
        + [wi[:, :, 1024:1056], jnp.zeros((n_even, 1024, 96), BF16)], axis=2)
    w['wuq'] = jnp.pad(p['mla_w_uq'], ((0, 0), (0, 0), (0, 0), (0, 32))).reshape(n_even, 768, 1024).astype(BF16)
    w['wk'] = jnp.pad(p['mla_w_ukv'][..., :MLA_NOPE], ((0, 0), (0, 0), (0, 0), (0, 64))
                      ).reshape(n_even, 256, 1024).astype(BF16)
    w['wv'] = jnp.pad(p['mla_w_ukv'][..., MLA_NOPE:], ((0, 0), (0, 0), (0, 0), (0, 64))
                      ).reshape(n_even, 256, 1024).astype(BF16)
    sel = np.zeros((MLA_ROPE, 1024), np.float32)
    one = np.zeros((1, 1024), np.float32)
    for h in range(MLA_HEADS):
        sel[np.arange(MLA_ROPE), h * LANES + MLA_NOPE + np.arange(MLA_ROPE)] = 1.0
        one[0, h * LANES + MLA_V] = 1.0
    w['sel'] = jnp.asarray(sel).astype(BF16)
    w['one'] = jnp.asarray(one)
    z64 = jnp.zeros((n_even, 64, RW_C), F32)
    w['wl'] = jnp.concatenate([p['rw_w2'], z64], axis=1).astype(BF16)
    w['al'] = jnp.concatenate([z64, p['rw_a2']], axis=1).astype(BF16)
    w['g2'] = p['rw_g2'].astype(BF16)
    w['mu'] = _rw_permute(p['rw_mu'])
    w['even_out'] = p['even_w_out'].astype(BF16)
    wo = p['odd_w_qkv'].astype(BF16)
    w['odd_in'] = jnp.concatenate([wo[:, :, 0:1024]] + [wo[:, :, a:b] for a, b in _ODD_SEGS], axis=2)
    bo = p['odd_b_qkv']
    w['odd_b'] = jnp.concatenate([bo[:, 0:1024]] + [bo[:, a:b] for a, b in _ODD_SEGS], axis=1)
    w['odd_out'] = p['odd_w_out'].astype(BF16)
    return w


def _trunk(x3, mods_all, start, past, p, w):
    b, t, d = x3.shape
    n = b * t
    depth = mods_all.shape[0]
    x = x3.reshape(n, d)
    rows = t
    tm_even, _ = _row_tiling(n, rows, 512)
    pos = start + jnp.arange(t)
    tabs = _rope_tables(pos, tm_even)
    even_states, odd_states = [], []
    for i in range(depth):
        mods = mods_all[i]
        j = i // 2
        x = _ffn(x, mods, p['norm_g'][i, 0], w['ffn_in'][i, 0], w['ffn_out'][i, 0], 0, rows)
        if i % 2 == 0:
            q, ckv, kr, prw = _even_in(x, mods, p['norm_g'][i, 1], w['even_in'][j], p['mla_q_norm'][j],
                                       p['mla_kv_norm'][j], w['wuq'][j], tabs, rows)
            if past is None:
                kx, vx = _kv_expand(ckv, kr, w['wk'][j], w['sel'], w['wv'][j], w['one'])
                tq = min(512, t)
                att = _mla_attn(q.reshape(b, t, 1024), kx.reshape(b, t, 1024), vx.reshape(b, t, 1024),
                                _causal_pairs(t // tq), tq, tq, 0)
                s0 = jnp.zeros((b, RW_HEADS, RW_N, RW_N), F32)
                sh0 = jnp.zeros((b, RW_COLS), F32)
            else:
                ckv_past, kr_past, s0, sh0 = past[0][j], past[1][j], past[2][j], past[3][j]
                n_past = ckv_past.shape[1]
                tk = 512
                t_all = -(-(n_past + t) // tk) * tk
                pad = t_all - n_past - t
                ckv_all = jnp.concatenate([ckv_past, ckv.reshape(b, t, 256),
                                           jnp.zeros((b, pad, 256), F32)], axis=1)
                kr_all = jnp.concatenate([kr_past, kr.reshape(b, t, MLA_ROPE),
                                          jnp.zeros((b, pad, MLA_ROPE), F32)], axis=1)
                kx, vx = _kv_expand(ckv_all.reshape(b * t_all, 256), kr_all.reshape(b * t_all, MLA_ROPE),
                                    w['wk'][j], w['sel'], w['wv'][j], w['one'])
                att = _mla_attn(q.reshape(b, t, 1024), kx.reshape(b, t_all, 1024), vx.reshape(b, t_all, 1024),
                                _full_pairs(t_all // tk), t, tk, start)
            rw_wts = (w['mu'][j].reshape(1, RW_COLS), p['rw_w0'][j].reshape(1, RW_C), w['wl'][j],
                      p['rw_a0'][j].reshape(1, RW_C), w['al'][j], w['g2'][j],
                      p['rw_k_k'][j].reshape(1, RW_C), p['rw_k_a'][j].reshape(1, RW_C),
                      p['rw_r_k'][j].reshape(1, RW_C), p['rw_ln_w'][j].reshape(1, RW_C),
                      p['rw_ln_b'][j].reshape(1, RW_C))
            prw3 = prw.reshape(b, t, RW_COLS)
            y_rw, s_bd = _rwkv(prw3, _rw_permute(sh0).reshape(b, 1, RW_COLS), _state_to_bd(s0), rw_wts)
            x = _out_proj(x, mods, [att.reshape(n, 512), y_rw.reshape(n, RW_C)],
                          [w['even_out'][j][:512], w['even_out'][j][512:]], rows)
            even_states.append((ckv.reshape(b, t, 256), kr.reshape(b, t, MLA_ROPE), _state_from_bd(s_bd),
                                _rw_unpermute(prw3[:, t - 1, :])))
        else:
            q, kd, vd = _odd_in(x, mods, p['norm_g'][i, 1], w['odd_in'][j], w['odd_b'][j], rows)
            q3, kd3, vd3 = q.reshape(b, t, 1024), kd.reshape(b, t, 512), vd.reshape(b, t, 512)
            if past is None:
                tq = min(256, t)
                o = _swa(q3, kd3, kd3, vd3, vd3, p['swa_sinks'][j], tq, True)
                keep = min(WINDOW, t)
                k_new, v_new = _undup(kd3[:, t - keep:]), _undup(vd3[:, t - keep:])
            else:
                k_past, v_past = past[4][j], past[5][j]
                o = _swa(q3, _dup(k_past), kd3, _dup(v_past), vd3, p['swa_sinks'][j], t, False)
                k_new = jnp.concatenate([k_past, _undup(kd3)], axis=1)[:, t:]
                v_new = jnp.concatenate([v_past, _undup(vd3)], axis=1)[:, t:]
            x = _out_proj(x, mods, [o.reshape(n, 1024)], [w['odd_out'][j]], rows)
            odd_states.append((k_new, v_new))
        fg = p['final_norm_g'] if i == depth - 1 else None
        x = _ffn(x, mods, p['norm_g'][i, 2], w['ffn_in'][i, 1], w['ffn_out'][i, 1], 2, rows, final_g=fg)
    es = [jnp.stack([st[k] for st in even_states]) for k in range(4)]
    os_ = [jnp.stack([st[k] for st in odd_states]) for k in range(2)]
    return x.reshape(b, t, d), es + os_


def kernel(x_prompt, x_sample, cache_mla_ckv, cache_mla_krope, state_rwkv, state_rwkv_shift, cache_swa_k, cache_swa_v, c_prompt, c_sample, w_ada, b_ada, norm_g, ffn_w_in, ffn_w_out, even_w_in, even_w_out, mla_q_norm, mla_kv_norm, mla_w_uq, mla_w_ukv, rw_mu, rw_w0, rw_w2, rw_a0, rw_a2, rw_g2, rw_k_k, rw_k_a, rw_r_k, rw_ln_w, rw_ln_b, odd_w_qkv, odd_b_qkv, odd_w_out, swa_sinks, final_norm_g):
    p = dict(w_ada=w_ada, b_ada=b_ada, norm_g=norm_g, ffn_w_in=ffn_w_in, ffn_w_out=ffn_w_out,
             even_w_in=even_w_in, even_w_out=even_w_out, mla_q_norm=mla_q_norm, mla_kv_norm=mla_kv_norm,
             mla_w_uq=mla_w_uq, mla_w_ukv=mla_w_ukv, rw_mu=rw_mu, rw_w0=rw_w0, rw_w2=rw_w2, rw_a0=rw_a0,
             rw_a2=rw_a2, rw_g2=rw_g2, rw_k_k=rw_k_k, rw_k_a=rw_k_a, rw_r_k=rw_r_k, rw_ln_w=rw_ln_w,
             rw_ln_b=rw_ln_b, odd_w_qkv=odd_w_qkv, odd_b_qkv=odd_b_qkv, odd_w_out=odd_w_out,
             swa_sinks=swa_sinks, final_norm_g=final_norm_g)
    w = _prep_weights(p)
    depth = w_ada.shape[0]
    bp, bs = c_prompt.shape[0], c_sample.shape[0]
    d = c_prompt.shape[1]
    b_pad = -(-(bp + bs) // 8) * 8
    c_all = jnp.concatenate([c_prompt, c_sample, jnp.zeros((b_pad - bp - bs, d), F32)], axis=0)
    mods = _ada(c_all, w_ada, b_ada).reshape(depth, b_pad, 3 * N_SUB, d)
    y_prompt, sp = _trunk(x_prompt, mods[:, :bp], 0, None, p, w)
    past = (cache_mla_ckv, cache_mla_krope, state_rwkv, state_rwkv_shift, cache_swa_k, cache_swa_v)
    y_sample, ss = _trunk(x_sample, mods[:, bp:bp + bs], cache_mla_ckv.shape[2], past, p, w)
    return (y_prompt, y_sample, sp[0], sp[1], sp[2], sp[3], sp[4], sp[5],
            ss[0], ss[1], ss[2], ss[3], ss[4], ss[5])
```

```python
import functools
import math

import jax
import jax.numpy as jnp
import numpy as np
from jax import lax
from jax.experimental import pallas as pl
from jax.experimental.pallas import tpu as pltpu

F32 = jnp.float32
BF16 = jnp.bfloat16

CHUNK = 64
EPS = 1e-6
NEG = -1e30
N_SUB = 3
MLA_HEADS = 8
MLA_NOPE = 64
MLA_ROPE = 32
MLA_V = 64
MLA_Q_LORA = 768
MLA_KV_LORA = 256
MLA_SCALE = (MLA_NOPE + MLA_ROPE) ** -0.5
LOG2E = math.log2(math.e)
ROPE_BASE = 10000.0
RW_HEADS = 8
RW_N = 64
RW_C = RW_HEADS * RW_N
RW_GN_EPS = 64e-5
RW_COLS = 3 * RW_C + 64 + 64 + 128
SW_HEADS = 16
SW_KV_HEADS = 4
SW_GROUP = 4
SW_HD = 64
WINDOW = 128

LANES = 128
VMEM_LIMIT = 56 * 1024 * 1024


def _cparams(sem):
    return pltpu.CompilerParams(dimension_semantics=sem, vmem_limit_bytes=VMEM_LIMIT)


def _const_spec(shape):
    nd = len(shape)
    return pl.BlockSpec(shape, lambda *_: (0,) * nd, pipeline_mode=pl.Buffered(1))


def _dot(a, b):
    return jnp.dot(a, b, preferred_element_type=F32)


def _dot_nt(a, b):
    return lax.dot_general(a, b, (((1,), (1,)), ((), ())), preferred_element_type=F32)


def _split2(x):
    hi = x.astype(BF16)
    lo = (x - hi.astype(F32)).astype(BF16)
    return hi, lo


def _norm_mod(x, g, sh, sc, nb):
    y = x * lax.rsqrt(jnp.mean(x * x, axis=-1, keepdims=True) + EPS) * g
    if nb == 1:
        return y * (1.0 + sc) + sh
    tm, d = x.shape
    y3 = y.reshape(nb, tm // nb, d)
    return (y3 * (1.0 + sc[:, None, :]) + sh[:, None, :]).reshape(tm, d)


def _gate_rows(g, y, nb):
    if nb == 1:
        return g * y
    tm, d = y.shape
    return (y.reshape(nb, tm // nb, d) * g[:, None, :]).reshape(tm, d)


def _row_tiling(n_rows, rows_per_seq, pref):
    if rows_per_seq >= pref:
        assert rows_per_seq % pref == 0
        return pref, 1
    tm = min(pref, n_rows)
    assert tm % rows_per_seq == 0 and n_rows % tm == 0
    return tm, tm // rows_per_seq


def _mods_spec(tm, nb, rows_per_seq):
    if nb == 1:
        tiles_per_seq = rows_per_seq // tm
        return pl.BlockSpec((1, 3 * N_SUB, 1024), lambda i: (i // tiles_per_seq, 0, 0))
    return pl.BlockSpec((nb, 3 * N_SUB, 1024), lambda i: (i, 0, 0))


def _ada_kernel(c_ref, w_ref, b_ref, o_ref):
    c = c_ref[...]
    cs = (c * jax.nn.sigmoid(c)).astype(BF16)
    o_ref[0] = _dot(cs, w_ref[0].astype(BF16)) + b_ref[0]


def _ada(c_all, w_ada, b_ada):
    depth, d, n = w_ada.shape
    bp = c_all.shape[0]
    tn = n // 4
    return pl.pallas_call(
        _ada_kernel,
        grid=(depth, n // tn),
        in_specs=[pl.BlockSpec((bp, d), lambda l, j: (0, 0)),
                  pl.BlockSpec((1, d, tn), lambda l, j: (l, 0, j)),
                  pl.BlockSpec((1, 1, tn), lambda l, j: (l, 0, j))],
        out_specs=pl.BlockSpec((1, bp, tn), lambda l, j: (l, 0, j)),
        out_shape=jax.ShapeDtypeStruct((depth, bp, n), F32),
        compiler_params=_cparams(("parallel", "parallel")),
    )(c_all, w_ada, b_ada.reshape(depth, 1, n))


FF_CHUNK = 256


def _ffn_kernel(x_ref, mods_ref, ng_ref, win_ref, wout_ref, *rest, sub, nb, d_ff, final):
    if final:
        fg_ref, o_ref, a_scr = rest
    else:
        o_ref, a_scr = rest
    x = x_ref[...]
    sh = mods_ref[:, 3 * sub, :]
    sc = mods_ref[:, 3 * sub + 1, :]
    gt = mods_ref[:, 3 * sub + 2, :]
    h = _norm_mod(x, ng_ref[...], sh, sc, nb).astype(BF16)
    for c in range(d_ff // FF_CHUNK):
        lo = c * FF_CHUNK
        g = _dot(h, win_ref[:, lo:lo + FF_CHUNK])
        u = _dot(h, win_ref[:, d_ff + lo:d_ff + lo + FF_CHUNK])
        a_scr[:, lo:lo + FF_CHUNK] = (g * jax.nn.sigmoid(g) * u).astype(BF16)
    y = _dot(a_scr[...], wout_ref[...])
    out = x + _gate_rows(0.5 * gt, y, nb)
    if final:
        out = out * lax.rsqrt(jnp.mean(out * out, axis=-1, keepdims=True) + EPS) * fg_ref[...]
    o_ref[...] = out


def _ffn(x, mods, norm_g, w_in, w_out, sub, rows_per_seq, final_g=None):
    n, d = x.shape
    d_ff = w_out.shape[0]
    tm, nb = _row_tiling(n, rows_per_seq, 512)
    final = final_g is not None
    in_specs = [pl.BlockSpec((tm, d), lambda i: (i, 0)),
                _mods_spec(tm, nb, rows_per_seq),
                _const_spec((1, d)), _const_spec(w_in.shape), _const_spec(w_out.shape)]
    args = [x, mods, norm_g.reshape(1, d), w_in, w_out]
    if final:
        in_specs.append(_const_spec((1, d)))
        args.append(final_g.reshape(1, d))
    return pl.pallas_call(
        functools.partial(_ffn_kernel, sub=sub, nb=nb, d_ff=d_ff, final=final),
        grid=(n // tm,),
        in_specs=in_specs,
        out_specs=pl.BlockSpec((tm, d), lambda i: (i, 0)),
        out_shape=jax.ShapeDtypeStruct((n, d), F32),
        scratch_shapes=[pltpu.VMEM((tm, d_ff), BF16)],
        compiler_params=_cparams(("parallel",)),
    )(*args)


EVEN_W = 768 + 256 + 1792 + 128
_RW_OFF = 1056
_RW_SEGS = [(0, 512), (576, 1088), (1088, 1600), (512, 576), (1600, 1664), (1664, 1792)]
_RW_INV_SEGS = [(0, 512), (1536, 1600), (512, 1024), (1024, 1536), (1600, 1664), (1664, 1792)]


def _rw_permute(a):
    return jnp.concatenate([a[..., s:e] for s, e in _RW_SEGS], axis=-1)


def _rw_unpermute(a):
    return jnp.concatenate([a[..., s:e] for s, e in _RW_INV_SEGS], axis=-1)


def _rope_slot(v, c, s1, s2):
    w = v.shape[-1]
    return v * c + pltpu.roll(v, w - 16, axis=1) * s1 + pltpu.roll(v, 16, axis=1) * s2


def _even_in_kernel(x_ref, mods_ref, ng_ref, w_ref, qn_ref, kvn_ref, wuq_ref,
                    cq_ref, s1q_ref, s2q_ref, ck_ref, s1k_ref, s2k_ref,
                    q_out, ckv_out, kr_out, prw_out, *, nb):
    x = x_ref[...]
    h = _norm_mod(x, ng_ref[...], mods_ref[:, 3, :], mods_ref[:, 4, :], nb).astype(BF16)
    cq = _dot(h, w_ref[:, 0:768])
    cqn = (cq * lax.rsqrt(jnp.mean(cq * cq, axis=-1, keepdims=True) + EPS) * qn_ref[...]).astype(BF16)
    q = _dot(cqn, wuq_ref[...])
    rep = lambda t: jnp.concatenate([t] * MLA_HEADS, axis=1)
    q = _rope_slot(q, rep(cq_ref[...]), rep(s1q_ref[...]), rep(s2q_ref[...]))
    q_out[...] = (q * (MLA_SCALE * LOG2E)).astype(BF16)
    ckv = _dot(h, w_ref[:, 768:1024])
    ckv_out[...] = ckv * lax.rsqrt(jnp.mean(ckv * ckv, axis=-1, keepdims=True) + EPS) * kvn_ref[...]
    prw_out[...] = _dot(h, w_ref[:, 1024:2816])
    krs = _dot(h, w_ref[:, 2816:2944])
    krs = _rope_slot(krs, ck_ref[...], s1k_ref[...], s2k_ref[...])
    kr_out[...] = krs[:, 0:MLA_ROPE]


def _even_in(x, mods, norm_g, w_perm, q_norm, kv_norm, wuq_slot, tabs, rows_per_seq):
    n, d = x.shape
    tm, nb = _row_tiling(n, rows_per_seq, 512)
    ttab = tabs[0].shape[0]
    ntab = ttab // tm
    tab_spec = pl.BlockSpec((tm, LANES), lambda i: (i % ntab, 0))
    row = lambda w: pl.BlockSpec((tm, w), lambda i: (i, 0))
    return pl.pallas_call(
        functools.partial(_even_in_kernel, nb=nb),
        grid=(n // tm,),
        in_specs=[row(d), _mods_spec(tm, nb, rows_per_seq), _const_spec((1, d)),
                  _const_spec(w_perm.shape), _const_spec((1, 768)), _const_spec((1, 256)),
                  _const_spec(wuq_slot.shape)] + [tab_spec] * 6,
        out_specs=[row(1024), row(256), row(MLA_ROPE), row(RW_COLS)],
        out_shape=[jax.ShapeDtypeStruct((n, 1024), BF16), jax.ShapeDtypeStruct((n, 256), F32),
                   jax.ShapeDtypeStruct((n, MLA_ROPE), F32), jax.ShapeDtypeStruct((n, RW_COLS), F32)],
        compiler_params=_cparams(("parallel",)),
    )(x, mods, norm_g.reshape(1, d), w_perm, q_norm.reshape(1, 768), kv_norm.reshape(1, 256),
      wuq_slot, *tabs)


def _rope_tables(pos, tile_to):
    half = MLA_ROPE // 2
    freqs = ROPE_BASE ** (-jnp.arange(half, dtype=F32) / half)
    ang = pos.astype(F32)[:, None] * freqs[None, :]
    cos, sin = jnp.cos(ang), jnp.sin(ang)
    t = pos.shape[0]
    z = lambda w: jnp.zeros((t, w), F32)
    o = lambda w: jnp.ones((t, w), F32)
    cq = jnp.concatenate([o(64), cos, cos, z(32)], axis=1)
    s1q = jnp.concatenate([z(64), -sin, z(48)], axis=1)
    s2q = jnp.concatenate([z(80), sin, z(32)], axis=1)
    ck = jnp.concatenate([cos, cos, z(96)], axis=1)
    s1k = jnp.concatenate([-sin, z(112)], axis=1)
    s2k = jnp.concatenate([z(16), sin, z(96)], axis=1)
    tabs = [cq, s1q, s2q, ck, s1k, s2k]
    if tile_to > t:
        tabs = [jnp.tile(a, (tile_to // t, 1)) for a in tabs]
    return tabs


def _kv_expand_kernel(ckv_ref, kr_ref, wk_ref, sel_ref, wv_ref, one_ref, k_out, v_out):
    c = ckv_ref[...].astype(BF16)
    k = _dot(c, wk_ref[...]) + _dot(kr_ref[...].astype(BF16), sel_ref[...])
    k_out[...] = k.astype(BF16)
    v_out[...] = (_dot(c, wv_ref[...]) + one_ref[...]).astype(BF16)


def _kv_expand(ckv, kr, wk_slot, sel, wv_slot, one_slot):
    n = ckv.shape[0]
    tm = min(1024, n)
    assert n % tm == 0
    row = lambda w: pl.BlockSpec((tm, w), lambda i: (i, 0))
    return pl.pallas_call(
        _kv_expand_kernel,
        grid=(n // tm,),
        in_specs=[row(256), row(MLA_ROPE), _const_spec(wk_slot.shape), _const_spec(sel.shape),
                  _const_spec(wv_slot.shape), _const_spec(one_slot.shape)],
        out_specs=[row(1024), row(1024)],
        out_shape=[jax.ShapeDtypeStruct((n, 1024), BF16), jax.ShapeDtypeStruct((n, 1024), BF16)],
        compiler_params=_cparams(("parallel",)),
    )(ckv, kr, wk_slot, sel, wv_slot, one_slot)


def _mla_attn_kernel(qi_ref, ki_ref, fl_ref, q_ref, k_ref, v_ref, o_ref, m_scr, acc_scr,
                     *, tq, tk, q_off):
    p_id = pl.program_id(1)
    flags = fl_ref[p_id]
    first = (flags & 1) != 0
    last = (flags & 2) != 0
    masked = (flags & 4) != 0

    @pl.when(first)
    def _():
        m_scr[...] = jnp.full(m_scr.shape, NEG, F32)
        acc_scr[...] = jnp.zeros(acc_scr.shape, F32)

    rb = min(MLA_ROW_BLOCK, tq)
    sb = min(MLA_SUB_BLOCK, rb)
    n_lt = tk // LANES

    insts = [(h, r0) for h in range(MLA_HEADS) for r0 in range(0, tq, rb)]

    def scores(h, r0):
        sl = slice(h * LANES, (h + 1) * LANES)
        return _dot_nt(q_ref[0, r0:r0 + rb, sl], k_ref[0, :, sl])

    def body(use_mask):
        if use_mask:
            q_chunk0 = (q_off + qi_ref[p_id] * tq) // CHUNK
            k_chunk0 = (ki_ref[p_id] * tk) // CHUNK
            kc = k_chunk0 + lax.broadcasted_iota(jnp.int32, (sb, tk), 1) // CHUNK
        s_next = scores(*insts[0])
        for idx, (h, r0) in enumerate(insts):
            s = s_next
            if idx + 1 < len(insts):
                s_next = scores(*insts[idx + 1])
            ps, alphas = [], []
            for r1 in range(0, rb, sb):
                rows = slice(r0 + r1, r0 + r1 + sb)
                s_r = s[r1:r1 + sb]
                if use_mask:
                    s_r = jnp.where(kc <= q_chunk0 + (r0 + r1) // CHUNK, s_r, NEG)
                mx = s_r[:, 0:LANES]
                for j in range(1, n_lt):
                    mx = jnp.maximum(mx, s_r[:, j * LANES:(j + 1) * LANES])
                m_prev = m_scr[h, rows, :]
                m_new = jnp.maximum(m_prev, jnp.max(mx, axis=-1, keepdims=True))
                m_scr[h, rows, :] = m_new
                alphas.append(jnp.exp2(m_prev - m_new))
                ps.append(jnp.exp2(s_r - jnp.concatenate([m_new] * n_lt, axis=1)).astype(BF16))
            p = jnp.concatenate(ps, axis=0) if len(ps) > 1 else ps[0]
            alpha = jnp.concatenate(alphas, axis=0) if len(alphas) > 1 else alphas[0]
            v_h = v_ref[0, :, h * LANES:(h + 1) * LANES]
            acc_scr[h, r0:r0 + rb, :] = alpha * acc_scr[h, r0:r0 + rb, :] + _dot(p, v_h)

    @pl.when(masked)
    def _():
        body(True)

    @pl.when(jnp.logical_not(masked))
    def _():
        body(False)

    @pl.when(last)
    def _():
        lane = lax.broadcasted_iota(jnp.int32, (tq, LANES), 1)
        for pr in range(MLA_HEADS // 2):
            ae = acc_scr[2 * pr]
            ao = acc_scr[2 * pr + 1]
            oe = ae / ae[:, MLA_V:MLA_V + 1]
            oo = ao / ao[:, MLA_V:MLA_V + 1]
            o_ref[0, :, pr * LANES:(pr + 1) * LANES] = jnp.where(
                lane < MLA_V, oe, pltpu.roll(oo, MLA_V, axis=1)).astype(BF16)


MLA_Q_TILE = 1024
MLA_KV_TILE = 512
MLA_ROW_BLOCK = 256
MLA_SUB_BLOCK = 32


def _mla_attn(q, k, v, pairs, tq, tk, q_off):
    b, t_q, _ = q.shape
    qi, ki, fl = pairs
    grid_spec = pltpu.PrefetchScalarGridSpec(
        num_scalar_prefetch=3,
        grid=(b, qi.shape[0]),
        in_specs=[pl.BlockSpec((1, tq, 1024), lambda bb, p, qi, ki, fl: (bb, qi[p], 0)),
                  pl.BlockSpec((1, tk, 1024), lambda bb, p, qi, ki, fl: (bb, ki[p], 0)),
                  pl.BlockSpec((1, tk, 1024), lambda bb, p, qi, ki, fl: (bb, ki[p], 0))],
        out_specs=pl.BlockSpec((1, tq, 512), lambda bb, p, qi, ki, fl: (bb, qi[p], 0)),
        scratch_shapes=[pltpu.VMEM((MLA_HEADS, tq, LANES), F32),
                        pltpu.VMEM((MLA_HEADS, tq, LANES), F32)])
    return pl.pallas_call(
        functools.partial(_mla_attn_kernel, tq=tq, tk=tk, q_off=q_off),
        grid_spec=grid_spec,
        out_shape=jax.ShapeDtypeStruct((b, t_q, 512), BF16),
        compiler_params=_cparams(("parallel", "arbitrary")),
    )(qi, ki, fl, q, k, v)


def _causal_pairs(nq, ratio):
    qi, ki, fl = [], [], []
    for a in range(nq):
        n_kv = (a + 1) * ratio
        for c in range(n_kv):
            qi.append(a)
            ki.append(c)
            fl.append((1 if c == 0 else 0) | (2 if c == n_kv - 1 else 0) | (4 if c >= a * ratio else 0))
    return tuple(jnp.asarray(np.array(z, np.int32)) for z in (qi, ki, fl))


def _full_pairs(nk):
    qi = [0] * nk
    ki = list(range(nk))
    fl = [(1 if c == 0 else 0) | (6 if c == nk - 1 else 0) for c in range(nk)]
    return tuple(jnp.asarray(np.array(z, np.int32)) for z in (qi, ki, fl))


RW_CHUNK_GROUP = 2
RW_TIME_TILE = 256


def _pair_sum(x, ones_bd):
    hi, lo = _split2(x)
    return _dot(hi, ones_bd) + _dot(lo, ones_bd)


def _rwkv_kernel(pr_ref, sh0_ref, s0_ref, mu_ref, w0_ref, wl_ref, a0_ref, al_ref, g2_ref,
                 kk_ref, ka_ref, rk_ref, lnw_ref, lnb_ref, y_ref, st_ref,
                 prev_scr, s_scr, rt_scr, at_scr, bt_scr, kt_scr, bv_scr, k2_scr, v_scr,
                 cum_scr, yc_scr, *, tt):
    t_id = pl.program_id(1)
    n_pairs = RW_HEADS // 2

    @pl.when(t_id == 0)
    def _():
        s_scr[...] = s0_ref[0]
        prev_scr[0:1, :] = sh0_ref[0]

    pr = pr_ref[0]
    row = lax.broadcasted_iota(jnp.int32, (tt, 1), 0)
    prev = jnp.where(row == 0, prev_scr[0:1, :], pltpu.roll(pr, 1, axis=0))
    prev_scr[0:1, :] = pr[tt - 1:tt, :]
    pm = pr + (prev - pr) * mu_ref[...]
    r = pm[:, 0:512]
    k = pm[:, 512:1024]
    v = pm[:, 1024:1536]
    wa = pm[:, 1536:1664]
    g_in = pm[:, 1664:1792]
    z = w0_ref[...] + _dot(jnp.tanh(wa).astype(BF16), wl_ref[...])
    nz = -z
    w = -(jnp.maximum(nz, 0.0) + jnp.log(1.0 + jnp.exp(-jnp.abs(nz)))) - 0.5
    ld = -jnp.exp(w)
    a_sig = jax.nn.sigmoid(a0_ref[...] + _dot(wa.astype(BF16), al_ref[...]))
    g = _dot(jax.nn.sigmoid(g_in).astype(BF16), g2_ref[...])

    li = lax.broadcasted_iota(jnp.int32, (LANES, LANES), 0)
    lj = lax.broadcasted_iota(jnp.int32, (LANES, LANES), 1)
    ones_bd = jnp.where((li // RW_N) == (lj // RW_N), 1.0, 0.0).astype(BF16)

    def head_sum(x):
        return jnp.concatenate([_pair_sum(x[:, p * LANES:(p + 1) * LANES], ones_bd)
                                for p in range(n_pairs)], axis=1)

    kk = k * kk_ref[...]
    kk = kk * lax.rsqrt(jnp.maximum(head_sum(kk * kk), 1e-24))
    k2 = k * (1.0 + (a_sig - 1.0) * ka_ref[...])
    bonus = head_sum(r * k2 * rk_ref[...]) * v

    ti = lax.broadcasted_iota(jnp.int32, (tt, tt), 0)
    tj = lax.broadcasted_iota(jnp.int32, (tt, tt), 1)
    tri = jnp.where(((ti // CHUNK) == (tj // CHUNK)) & (tj <= ti), 1.0, 0.0).astype(BF16)
    l1 = ld.astype(BF16)
    rem = ld - l1.astype(F32)
    l2 = rem.astype(BF16)
    l3 = (rem - l2.astype(F32)).astype(BF16)
    cum = (_dot(tri, l3) + _dot(tri, l2)) + _dot(tri, l1)
    winv = jnp.exp(-cum)
    rt_scr[...] = r * jnp.exp(cum)
    at_scr[...] = -kk * jnp.exp(cum - ld)
    bv = kk * a_sig
    bt_scr[...] = bv * winv
    kt_scr[...] = k2 * winv
    bv_scr[...] = bv
    k2_scr[...] = k2
    v_scr[...] = v
    cum_scr[...] = cum

    lane = lax.broadcasted_iota(jnp.int32, (CHUNK, LANES), 1)
    even = lane < RW_N
    strict = (lj % RW_N) < (li % RW_N)
    incl = (lj % RW_N) <= (li % RW_N)
    eye = jnp.where(li == lj, 1.0, 0.0).astype(F32)
    pairs = range(n_pairs)

    def same_block(m):
        return (li // m) == (lj // m)

    def stack_f32(x):
        return jnp.concatenate([jnp.where(even, x, 0.0), jnp.where(even, 0.0, x)], axis=0)

    def stack(x):
        return stack_f32(x).astype(BF16)

    mm = lambda a, b: _dot(a.astype(BF16), b.astype(BF16))
    rows_of = lambda c: slice(c * CHUNK, (c + 1) * CHUNK)
    lanes_of = lambda p: slice(p * LANES, (p + 1) * LANES)

    def independent_phase(chunks, res):
        insts = [(c, p) for c in chunks for p in pairs]
        load = lambda scr: [scr[rows_of(c), lanes_of(p)] for c, p in insts]
        cum_c = load(cum_scr)
        cum_l = [a[CHUNK - 1:CHUNK, :] for a in cum_c]
        w2 = [jnp.exp(a - b) for a, b in zip(cum_l, cum_c)]
        rs = [stack(a) for a in load(rt_scr)]
        as_ = [stack(a) for a in load(at_scr)]
        bs = [stack(a) for a in load(bt_scr)]
        ks = [stack(a) for a in load(kt_scr)]
        v_f = [stack_f32(a) for a in load(v_scr)]
        vs = [a.astype(BF16) for a in v_f]
        b2s = [stack(a * w) for a, w in zip(load(bv_scr), w2)]
        k2s = [stack(a * w) for a, w in zip(load(k2_scr), w2)]
        yield
        n_m = [jnp.where(strict, _dot_nt(a, b), 0.0) for a, b in zip(as_, bs)]
        mk = [jnp.where(strict, _dot_nt(a, b), 0.0).astype(BF16) for a, b in zip(as_, ks)]
        cb = [jnp.where(incl, _dot_nt(a, b), 0.0).astype(BF16) for a, b in zip(rs, bs)]
        ck = [jnp.where(incl, _dot_nt(a, b), 0.0).astype(BF16) for a, b in zip(rs, ks)]
        yield
        n8 = [jnp.where(same_block(8), n, 0.0) for n in n_m]
        t = [eye + a for a in n8]
        p2 = [mm(a, a) for a in n8]
        mv = [_dot(a, b) for a, b in zip(mk, vs)]
        yield
        t = [a + mm(a, b) for a, b in zip(t, p2)]
        p4 = [mm(a, a) for a in p2]
        cv = [_dot(a, b) for a, b in zip(ck, vs)]
        yield
        t = [a + mm(a, b) for a, b in zip(t, p4)]
        vk = [_dot(a.T.astype(BF16), b) for a, b in zip(v_f, k2s)]
        yield
        for m in (8, 16, 32):
            off = same_block(2 * m) & jnp.logical_not(same_block(m))
            x = [mm(jnp.where(off, n, 0.0), a) for n, a in zip(n_m, t)]
            yield
            t = [a + mm(a, b) for a, b in zip(t, x)]
            yield
        for i, key in enumerate(insts):
            res[key] = dict(as_=as_[i], rs=rs[i], b2s=b2s[i], cb=cb[i], mv=mv[i], cv=cv[i], vk=vk[i],
                            tinv=t[i].astype(BF16), wl=jnp.exp(cum_l[i]))

    def dependent_phase(chunks, res):
        for c in chunks:
            rc = [res[(c, p)] for p in pairs]
            s = [s_scr[p] for p in pairs]
            sb = [a.astype(BF16) for a in s]
            x = [_dot_nt(r_['as_'], b) + r_['mv'] for r_, b in zip(rc, sb)]
            rs_s = [_dot_nt(r_['rs'], b) + r_['cv'] for r_, b in zip(rc, sb)]
            yield
            u = [_dot(r_['tinv'], a.astype(BF16)) for r_, a in zip(rc, x)]
            yield
            ys = [a + _dot(r_['cb'], b.astype(BF16)) for a, r_, b in zip(rs_s, rc, u)]
            for p in pairs:
                yc_scr[rows_of(c), lanes_of(p)] = ys[p][0:CHUNK] + ys[p][CHUNK:2 * CHUNK]
                s_scr[p] = s[p] * rc[p]['wl'] + _dot(u[p].T.astype(BF16), rc[p]['b2s']) + rc[p]['vk']
            yield

    def emit(*gens):
        gens = list(gens)
        while gens:
            for gen in list(gens):
                try:
                    next(gen)
                except StopIteration:
                    gens.remove(gen)

    n_chunks = tt // CHUNK
    groups = [list(range(c0, min(c0 + RW_CHUNK_GROUP, n_chunks))) for c0 in range(0, n_chunks, RW_CHUNK_GROUP)]
    res = {}
    emit(independent_phase(groups[0], res))
    for gi in range(1, len(groups)):
        emit(independent_phase(groups[gi], res), dependent_phase(groups[gi - 1], res))
    emit(dependent_phase(groups[-1], res))

    y = yc_scr[...]
    mean = head_sum(y) * (1.0 / RW_N)
    dlt = y - mean
    var = head_sum(dlt * dlt) * (1.0 / RW_N)
    yn = dlt * lax.rsqrt(var + RW_GN_EPS) * lnw_ref[...] + lnb_ref[...]
    y_ref[0] = ((yn + bonus) * g).astype(BF16)

    @pl.when(t_id == pl.num_programs(1) - 1)
    def _():
        st_ref[0] = s_scr[...]


def _rwkv(prw, sh0, s0_bd, wts):
    b, t, _ = prw.shape
    tt = min(RW_TIME_TILE, t)
    assert t % tt == 0 and tt % CHUNK == 0
    c512 = _const_spec((1, RW_C))
    scr = lambda: pltpu.VMEM((tt, RW_C), F32)
    return pl.pallas_call(
        functools.partial(_rwkv_kernel, tt=tt),
        grid=(b, t // tt),
        in_specs=[pl.BlockSpec((1, tt, RW_COLS), lambda bb, i: (bb, i, 0)),
                  pl.BlockSpec((1, 1, RW_COLS), lambda bb, i: (bb, 0, 0)),
                  pl.BlockSpec((1, 4, LANES, LANES), lambda bb, i: (bb, 0, 0, 0)),
                  _const_spec((1, RW_COLS)), c512, _const_spec((LANES, RW_C)), c512,
                  _const_spec((LANES, RW_C)), _const_spec((LANES, RW_C)),
                  c512, c512, c512, c512, c512],
        out_specs=[pl.BlockSpec((1, tt, RW_C), lambda bb, i: (bb, i, 0)),
                   pl.BlockSpec((1, 4, LANES, LANES), lambda bb, i: (bb, 0, 0, 0))],
        out_shape=[jax.ShapeDtypeStruct((b, t, RW_C), BF16),
                   jax.ShapeDtypeStruct((b, 4, LANES, LANES), F32)],
        scratch_shapes=[pltpu.VMEM((8, RW_COLS), F32), pltpu.VMEM((4, LANES, LANES), F32)]
                       + [scr() for _ in range(9)],
        compiler_params=_cparams(("parallel", "arbitrary")),
    )(prw, sh0, s0_bd, *wts)


def _state_to_bd(s):
    b = s.shape[0]
    s = s.reshape(b, 4, 2, RW_N, RW_N)
    z = jnp.zeros_like(s[:, :, 0])
    top = jnp.concatenate([s[:, :, 0], z], axis=-1)
    bot = jnp.concatenate([z, s[:, :, 1]], axis=-1)
    return jnp.concatenate([top, bot], axis=-2)


def _state_from_bd(s):
    b = s.shape[0]
    return jnp.stack([s[:, :, :RW_N, :RW_N], s[:, :, RW_N:, RW_N:]], axis=2).reshape(b, RW_HEADS, RW_N, RW_N)


def _out_proj_kernel(*refs, n_in, nb):
    x_ref, mods_ref = refs[0], refs[1]
    a_refs = refs[2:2 + n_in]
    w_refs = refs[2 + n_in:2 + 2 * n_in]
    o_ref = refs[2 + 2 * n_in]
    y = _dot(a_refs[0][...], w_refs[0][...])
    for a, w in zip(a_refs[1:], w_refs[1:]):
        y = y + _dot(a[...], w[...])
    o_ref[...] = x_ref[...] + _gate_rows(mods_ref[:, 5, :], y, nb)


def _out_proj(x, mods, acts, ws, rows_per_seq):
    n, d = x.shape
    tm, nb = _row_tiling(n, rows_per_seq, 512)
    row = lambda w: pl.BlockSpec((tm, w), lambda i: (i, 0))
    return pl.pallas_call(
        functools.partial(_out_proj_kernel, n_in=len(acts), nb=nb),
        grid=(n // tm,),
        in_specs=[row(d), _mods_spec(tm, nb, rows_per_seq)] + [row(a.shape[1]) for a in acts]
                 + [_const_spec(w.shape) for w in ws],
        out_specs=row(d),
        out_shape=jax.ShapeDtypeStruct((n, d), F32),
        compiler_params=_cparams(("parallel",)),
    )(x, mods, *acts, *ws)


_ODD_SEGS = ([(1024 + g * 64, 1088 + g * 64) for g in range(SW_KV_HEADS) for _ in range(2)]
             + [(1280 + g * 64, 1344 + g * 64) for g in range(SW_KV_HEADS) for _ in range(2)])


def _odd_in_kernel(x_ref, mods_ref, ng_ref, w_ref, b_ref, q_out, k_out, v_out, *, nb):
    h = _norm_mod(x_ref[...], ng_ref[...], mods_ref[:, 3, :], mods_ref[:, 4, :], nb).astype(BF16)
    q = _dot(h, w_ref[:, 0:1024]) + b_ref[:, 0:1024]
    q_out[...] = (q * (SW_HD ** -0.5 * LOG2E)).astype(BF16)
    k_out[...] = _dot(h, w_ref[:, 1024:1536]) + b_ref[:, 1024:1536]
    v_out[...] = _dot(h, w_ref[:, 1536:2048]) + b_ref[:, 1536:2048]


def _odd_in(x, mods, norm_g, w_perm, b_perm, rows_per_seq):
    n, d = x.shape
    tm, nb = _row_tiling(n, rows_per_seq, 512)
    row = lambda w: pl.BlockSpec((tm, w), lambda i: (i, 0))
    return pl.pallas_call(
        functools.partial(_odd_in_kernel, nb=nb),
        grid=(n // tm,),
        in_specs=[row(d), _mods_spec(tm, nb, rows_per_seq), _const_spec((1, d)),
                  _const_spec(w_perm.shape), _const_spec((1, 2048))],
        out_specs=[row(1024), row(512), row(512)],
        out_shape=[jax.ShapeDtypeStruct((n, 1024), BF16), jax.ShapeDtypeStruct((n, 512), F32),
                   jax.ShapeDtypeStruct((n, 512), F32)],
        compiler_params=_cparams(("parallel",)),
    )(x, mods, norm_g.reshape(1, d), w_perm, b_perm.reshape(1, 2048))


def _swa_kernel(sinks_ref, q_ref, kp_ref, kc_ref, vp_ref, vc_ref, o_ref, *, tq, mask_first_prev):
    nk = WINDOW + tq
    qi = lax.broadcasted_iota(jnp.int32, (tq, nk), 0)
    kj = lax.broadcasted_iota(jnp.int32, (tq, nk), 1)
    kc = kj // CHUNK - WINDOW // CHUNK
    qc = qi // CHUNK
    vis = (kc <= qc) & (kc >= qc - WINDOW // CHUNK)
    if mask_first_prev:
        vis = vis & ((kj >= WINDOW) | (pl.program_id(1) > 0))
    ndist = jnp.where(vis, -jnp.abs(qi + WINDOW - kj).astype(F32), NEG)
    keys = jnp.concatenate([kp_ref[0], kc_ref[0]], axis=0).astype(BF16)
    vals = jnp.concatenate([vp_ref[0], vc_ref[0]], axis=0)
    klane = lax.broadcasted_iota(jnp.int32, (nk, LANES), 1)
    v_slots = [jnp.where(klane < SW_HD, vals[:, g * LANES:(g + 1) * LANES], 1.0).astype(BF16)
               for g in range(SW_KV_HEADS)]
    lane = lax.broadcasted_iota(jnp.int32, (tq, LANES), 1)
    low = lane < SW_HD
    sb = min(SWA_SUB_BLOCK, tq)

    def scores(h):
        qp = q_ref[0, :, (h // 2) * LANES:(h // 2 + 1) * LANES]
        qh = jnp.where(low if h % 2 == 0 else jnp.logical_not(low), qp, jnp.zeros_like(qp))
        return _dot_nt(qh, keys[:, (h // SW_GROUP) * LANES:(h // SW_GROUP + 1) * LANES])

    outs = []
    s_next = scores(0)
    for h in range(SW_HEADS):
        s = s_next
        if h + 1 < SW_HEADS:
            s_next = scores(h + 1)
        slope = (2.0 ** (-8.0 * (h + 1) / SW_HEADS)) * LOG2E
        sk = sinks_ref[h] * LOG2E
        es, ms = [], []
        for r1 in range(0, tq, sb):
            z = s[r1:r1 + sb] + slope * ndist[r1:r1 + sb]
            m = jnp.maximum(jnp.broadcast_to(jnp.max(z, axis=-1, keepdims=True), (sb, LANES)), sk)
            m_wide = jnp.concatenate([m] * (nk // LANES), axis=1) if nk % LANES == 0 else m[:, 0:1]
            es.append(jnp.exp2(z - m_wide).astype(BF16))
            ms.append(m)
        e = jnp.concatenate(es, axis=0) if len(es) > 1 else es[0]
        m = jnp.concatenate(ms, axis=0) if len(ms) > 1 else ms[0]
        pv = _dot(e, v_slots[h // SW_GROUP])
        outs.append(pv / (pltpu.roll(pv, SW_HD, axis=1) + jnp.exp2(sk - m)))
    for pr in range(SW_HEADS // 2):
        o_ref[0, :, pr * LANES:(pr + 1) * LANES] = jnp.where(
            low, outs[2 * pr], pltpu.roll(outs[2 * pr + 1], SW_HD, axis=1)).astype(BF16)


SWA_SUB_BLOCK = 32


def _swa(q, k_prev, k_cur, v_prev, v_cur, sinks, tq, same_array):
    b, t, _ = q.shape
    nt = t // tq
    per = tq // WINDOW
    if same_array:
        prev_map = lambda bb, i: (bb, jnp.maximum(i * per - 1, 0), 0)
    else:
        prev_map = lambda bb, i: (bb, 0, 0)
    cur = lambda w: pl.BlockSpec((1, tq, w), lambda bb, i: (bb, i, 0))
    prev = pl.BlockSpec((1, WINDOW, 512), prev_map)
    return pl.pallas_call(
        functools.partial(_swa_kernel, tq=tq, mask_first_prev=same_array),
        grid=(b, nt),
        in_specs=[pl.BlockSpec(memory_space=pltpu.SMEM), cur(1024), prev, cur(512), prev, cur(512)],
        out_specs=cur(1024),
        out_shape=jax.ShapeDtypeStruct((b, t, 1024), BF16),
        compiler_params=_cparams(("parallel", "parallel")),
    )(sinks, q, k_prev, k_cur, v_prev, v_cur)


def _undup(a):
    return a.reshape(a.shape[:-1] + (SW_KV_HEADS, 2, SW_HD))[..., 0, :]


def _dup(a):
    return jnp.concatenate([a, a], axis=-1).reshape(a.shape[:-2] + (512,))


def _prep_weights(p):
    depth = p['w_ada'].shape[0]
    n_even, n_odd = (depth + 1) // 2, depth // 2
    w = {}
    w['ffn_in'] = p['ffn_w_in'].astype(BF16)
    w['ffn_out'] = p['ffn_w_out'].astype(BF16)
    wi = p['even_w_in'].astype(BF16)
    w['even_in'] = jnp.concatenate(
        [wi[:, :, 0:1024]] + [wi[:, :, _RW_OFF + a:_RW_OFF + b] for a, b in _RW_SEGS]
        + [wi[:, :, 1024:1056], jnp.zeros((n_even, 1024, 96), BF16)], axis=2)
    w['wuq'] = jnp.pad(p['mla_w_uq'], ((0, 0), (0, 0), (0, 0), (0, 32))).reshape(n_even, 768, 1024).astype(BF16)
    w['wk'] = jnp.pad(p['mla_w_ukv'][..., :MLA_NOPE], ((0, 0), (0, 0), (0, 0), (0, 64))
                      ).reshape(n_even, 256, 1024).astype(BF16)
    w['wv'] = jnp.pad(p['mla_w_ukv'][..., MLA_NOPE:], ((0, 0), (0, 0), (0, 0), (0, 64))
                      ).reshape(n_even, 256, 1024).astype(BF16)
    sel = np.zeros((MLA_ROPE, 1024), np.float32)
    one = np.zeros((1, 1024), np.float32)
    for h in range(MLA_HEADS):
        sel[np.arange(MLA_ROPE), h * LANES + MLA_NOPE + np.arange(MLA_ROPE)] = 1.0
        one[0, h * LANES + MLA_V] = 1.0
    w['sel'] = jnp.asarray(sel).astype(BF16)
    w['one'] = jnp.asarray(one)
    z64 = jnp.zeros((n_even, 64, RW_C), F32)
    w['wl'] = jnp.concatenate([p['rw_w2'], z64], axis=1).astype(BF16)
    w['al'] = jnp.concatenate([z64, p['rw_a2']], axis=1).astype(BF16)
    w['g2'] = p['rw_g2'].astype(BF16)
    w['mu'] = _rw_permute(p['rw_mu'])
    w['even_out'] = p['even_w_out'].astype(BF16)
    wo = p['odd_w_qkv'].astype(BF16)
    w['odd_in'] = jnp.concatenate([wo[:, :, 0:1024]] + [wo[:, :, a:b] for a, b in _ODD_SEGS], axis=2)
    bo = p['odd_b_qkv']
    w['odd_b'] = jnp.concatenate([bo[:, 0:1024]] + [bo[:, a:b] for a, b in _ODD_SEGS], axis=1)
    w['odd_out'] = p['odd_w_out'].astype(BF16)
    return w


def _trunk(x3, mods_all, start, past, p, w):
    b, t, d = x3.shape
    n = b * t
    depth = mods_all.shape[0]
    x = x3.reshape(n, d)
    rows = t
    tm_even, _ = _row_tiling(n, rows, 512)
    pos = start + jnp.arange(t)
    tabs = _rope_tables(pos, tm_even)
    even_states, odd_states = [], []
    for i in range(depth):
        mods = mods_all[i]
        j = i // 2
        x = _ffn(x, mods, p['norm_g'][i, 0], w['ffn_in'][i, 0], w['ffn_out'][i, 0], 0, rows)
        if i % 2 == 0:
            q, ckv, kr, prw = _even_in(x, mods, p['norm_g'][i, 1], w['even_in'][j], p['mla_q_norm'][j],
                                       p['mla_kv_norm'][j], w['wuq'][j], tabs, rows)
            if past is None:
                kx, vx = _kv_expand(ckv, kr, w['wk'][j], w['sel'], w['wv'][j], w['one'])
                tk = min(MLA_KV_TILE, t)
                tq = min(MLA_Q_TILE, t)
                att = _mla_attn(q.reshape(b, t, 1024), kx.reshape(b, t, 1024), vx.reshape(b, t, 1024),
                                _causal_pairs(t // tq, tq // tk), tq, tk, 0)
                s0 = jnp.zeros((b, RW_HEADS, RW_N, RW_N), F32)
                sh0 = jnp.zeros((b, RW_COLS), F32)
            else:
                ckv_past, kr_past, s0, sh0 = past[0][j], past[1][j], past[2][j], past[3][j]
                n_past = ckv_past.shape[1]
                tk = 512
                t_all = -(-(n_past + t) // tk) * tk
                pad = t_all - n_past - t
                ckv_all = jnp.concatenate([ckv_past, ckv.reshape(b, t, 256),
                                           jnp.zeros((b, pad, 256), F32)], axis=1)
                kr_all = jnp.concatenate([kr_past, kr.reshape(b, t, MLA_ROPE),
                                          jnp.zeros((b, pad, MLA_ROPE), F32)], axis=1)
                kx, vx = _kv_expand(ckv_all.reshape(b * t_all, 256), kr_all.reshape(b * t_all, MLA_ROPE),
                                    w['wk'][j], w['sel'], w['wv'][j], w['one'])
                att = _mla_attn(q.reshape(b, t, 1024), kx.reshape(b, t_all, 1024), vx.reshape(b, t_all, 1024),
                                _full_pairs(t_all // tk), t, tk, start)
            rw_wts = (w['mu'][j].reshape(1, RW_COLS), p['rw_w0'][j].reshape(1, RW_C), w['wl'][j],
                      p['rw_a0'][j].reshape(1, RW_C), w['al'][j], w['g2'][j],
                      p['rw_k_k'][j].reshape(1, RW_C), p['rw_k_a'][j].reshape(1, RW_C),
                      p['rw_r_k'][j].reshape(1, RW_C), p['rw_ln_w'][j].reshape(1, RW_C),
                      p['rw_ln_b'][j].reshape(1, RW_C))
            prw3 = prw.reshape(b, t, RW_COLS)
            y_rw, s_bd = _rwkv(prw3, _rw_permute(sh0).reshape(b, 1, RW_COLS), _state_to_bd(s0), rw_wts)
            x = _out_proj(x, mods, [att.reshape(n, 512), y_rw.reshape(n, RW_C)],
                          [w['even_out'][j][:512], w['even_out'][j][512:]], rows)
            even_states.append((ckv.reshape(b, t, 256), kr.reshape(b, t, MLA_ROPE), _state_from_bd(s_bd),
                                _rw_unpermute(prw3[:, t - 1, :])))
        else:
            q, kd, vd = _odd_in(x, mods, p['norm_g'][i, 1], w['odd_in'][j], w['odd_b'][j], rows)
            q3, kd3, vd3 = q.reshape(b, t, 1024), kd.reshape(b, t, 512), vd.reshape(b, t, 512)
            if past is None:
                tq = min(256, t)
                o = _swa(q3, kd3, kd3, vd3, vd3, p['swa_sinks'][j], tq, True)
                keep = min(WINDOW, t)
                k_new, v_new = _undup(kd3[:, t - keep:]), _undup(vd3[:, t - keep:])
            else:
                k_past, v_past = past[4][j], past[5][j]
                o = _swa(q3, _dup(k_past), kd3, _dup(v_past), vd3, p['swa_sinks'][j], t, False)
                k_new = jnp.concatenate([k_past, _undup(kd3)], axis=1)[:, t:]
                v_new = jnp.concatenate([v_past, _undup(vd3)], axis=1)[:, t:]
            x = _out_proj(x, mods, [o.reshape(n, 1024)], [w['odd_out'][j]], rows)
            odd_states.append((k_new, v_new))
        fg = p['final_norm_g'] if i == depth - 1 else None
        x = _ffn(x, mods, p['norm_g'][i, 2], w['ffn_in'][i, 1], w['ffn_out'][i, 1], 2, rows, final_g=fg)
    es = [jnp.stack([st[k] for st in even_states]) for k in range(4)]
    os_ = [jnp.stack([st[k] for st in odd_states]) for k in range(2)]
    return x.reshape(b, t, d), es + os_


def kernel(x_prompt, x_sample, cache_mla_ckv, cache_mla_krope, state_rwkv, state_rwkv_shift, cache_swa_k, cache_swa_v, c_prompt, c_sample, w_ada, b_ada, norm_g, ffn_w_in, ffn_w_out, even_w_in, even_w_out, mla_q_norm, mla_kv_norm, mla_w_uq, mla_w_ukv, rw_mu, rw_w0, rw_w2, rw_a0, rw_a2, rw_g2, rw_k_k, rw_k_a, rw_r_k, rw_ln_w, rw_ln_b, odd_w_qkv, odd_b_qkv, odd_w_out, swa_sinks, final_norm_g):
    p = dict(w_ada=w_ada, b_ada=b_ada, norm_g=norm_g, ffn_w_in=ffn_w_in, ffn_w_out=ffn_w_out,
             even_w_in=even_w_in, even_w_out=even_w_out, mla_q_norm=mla_q_norm, mla_kv_norm=mla_kv_norm,
             mla_w_uq=mla_w_uq, mla_w_ukv=mla_w_ukv, rw_mu=rw_mu, rw_w0=rw_w0, rw_w2=rw_w2, rw_a0=rw_a0,
             rw_a2=rw_a2, rw_g2=rw_g2, rw_k_k=rw_k_k, rw_k_a=rw_k_a, rw_r_k=rw_r_k, rw_ln_w=rw_ln_w,
             rw_ln_b=rw_ln_b, odd_w_qkv=odd_w_qkv, odd_b_qkv=odd_b_qkv, odd_w_out=odd_w_out,
             swa_sinks=swa_sinks, final_norm_g=final_norm_g)
    w = _prep_weights(p)
    depth = w_ada.shape[0]
    bp, bs = c_prompt.shape[0], c_sample.shape[0]
    d = c_prompt.shape[1]
    b_pad = -(-(bp + bs) // 8) * 8
    c_all = jnp.concatenate([c_prompt, c_sample, jnp.zeros((b_pad - bp - bs, d), F32)], axis=0)
    mods = _ada(c_all, w_ada, b_ada).reshape(depth, b_pad, 3 * N_SUB, d)
    y_prompt, sp = _trunk(x_prompt, mods[:, :bp], 0, None, p, w)
    past = (cache_mla_ckv, cache_mla_krope, state_rwkv, state_rwkv_shift, cache_swa_k, cache_swa_v)
    y_sample, ss = _trunk(x_sample, mods[:, bp:bp + bs], cache_mla_ckv.shape[2], past, p, w)
    return (y_prompt, y_sample, sp[0], sp[1], sp[2], sp[3], sp[4], sp[5],
            ss[0], ss[1], ss[2], ss[3], ss[4], ss[5])
```

```python
import functools
import math

import jax
import jax.numpy as jnp
import numpy as np
from jax import lax
from jax.experimental import pallas as pl
from jax.experimental.pallas import tpu as pltpu

F32 = jnp.float32
BF16 = jnp.bfloat16

CHUNK = 64
EPS = 1e-6
NEG = -1e30
N_SUB = 3
MLA_HEADS = 8
MLA_NOPE = 64
MLA_ROPE = 32
MLA_V = 64
MLA_Q_LORA = 768
MLA_KV_LORA = 256
MLA_SCALE = (MLA_NOPE + MLA_ROPE) ** -0.5
LOG2E = math.log2(math.e)
ROPE_BASE = 10000.0
RW_HEADS = 8
RW_N = 64
RW_C = RW_HEADS * RW_N
RW_GN_EPS = 64e-5
RW_COLS = 3 * RW_C + 64 + 64 + 128
SW_HEADS = 16
SW_KV_HEADS = 4
SW_GROUP = 4
SW_HD = 64
WINDOW = 128

LANES = 128
VMEM_LIMIT = 56 * 1024 * 1024


def _cparams(sem):
    return pltpu.CompilerParams(dimension_semantics=sem, vmem_limit_bytes=VMEM_LIMIT)


def _const_spec(shape):
    nd = len(shape)
    return pl.BlockSpec(shape, lambda *_: (0,) * nd, pipeline_mode=pl.Buffered(1))


def _dot(a, b):
    return jnp.dot(a, b, preferred_element_type=F32)


def _dot_nt(a, b):
    return lax.dot_general(a, b, (((1,), (1,)), ((), ())), preferred_element_type=F32)


def _split2(x):
    hi = x.astype(BF16)
    lo = (x - hi.astype(F32)).astype(BF16)
    return hi, lo


def _norm_mod(x, g, sh, sc, nb):
    y = x * lax.rsqrt(jnp.mean(x * x, axis=-1, keepdims=True) + EPS) * g
    if nb == 1:
        return y * (1.0 + sc) + sh
    tm, d = x.shape
    y3 = y.reshape(nb, tm // nb, d)
    return (y3 * (1.0 + sc[:, None, :]) + sh[:, None, :]).reshape(tm, d)


def _gate_rows(g, y, nb):
    if nb == 1:
        return g * y
    tm, d = y.shape
    return (y.reshape(nb, tm // nb, d) * g[:, None, :]).reshape(tm, d)


def _row_tiling(n_rows, rows_per_seq, pref):
    if rows_per_seq >= pref:
        assert rows_per_seq % pref == 0
        return pref, 1
    tm = min(pref, n_rows)
    assert tm % rows_per_seq == 0 and n_rows % tm == 0
    return tm, tm // rows_per_seq


def _mods_spec(tm, nb, rows_per_seq):
    if nb == 1:
        tiles_per_seq = rows_per_seq // tm
        return pl.BlockSpec((1, 3 * N_SUB, 1024), lambda i: (i // tiles_per_seq, 0, 0))
    return pl.BlockSpec((nb, 3 * N_SUB, 1024), lambda i: (i, 0, 0))


def _ada_kernel(c_ref, w_ref, b_ref, o_ref):
    c = c_ref[...]
    cs = (c * jax.nn.sigmoid(c)).astype(BF16)
    o_ref[0] = _dot(cs, w_ref[0].astype(BF16)) + b_ref[0]


def _ada(c_all, w_ada, b_ada):
    depth, d, n = w_ada.shape
    bp = c_all.shape[0]
    tn = n // 4
    return pl.pallas_call(
        _ada_kernel,
        grid=(depth, n // tn),
        in_specs=[pl.BlockSpec((bp, d), lambda l, j: (0, 0)),
                  pl.BlockSpec((1, d, tn), lambda l, j: (l, 0, j)),
                  pl.BlockSpec((1, 1, tn), lambda l, j: (l, 0, j))],
        out_specs=pl.BlockSpec((1, bp, tn), lambda l, j: (l, 0, j)),
        out_shape=jax.ShapeDtypeStruct((depth, bp, n), F32),
        compiler_params=_cparams(("parallel", "parallel")),
    )(c_all, w_ada, b_ada.reshape(depth, 1, n))


FF_CHUNK = 256


def _ffn_kernel(x_ref, mods_ref, ng_ref, win_ref, wout_ref, *rest, sub, nb, d_ff, final):
    if final:
        fg_ref, o_ref, a_scr = rest
    else:
        o_ref, a_scr = rest
    x = x_ref[...]
    sh = mods_ref[:, 3 * sub, :]
    sc = mods_ref[:, 3 * sub + 1, :]
    gt = mods_ref[:, 3 * sub + 2, :]
    h = _norm_mod(x, ng_ref[...], sh, sc, nb).astype(BF16)
    for c in range(d_ff // FF_CHUNK):
        lo = c * FF_CHUNK
        g = _dot(h, win_ref[:, lo:lo + FF_CHUNK])
        u = _dot(h, win_ref[:, d_ff + lo:d_ff + lo + FF_CHUNK])
        a_scr[:, lo:lo + FF_CHUNK] = (g * jax.nn.sigmoid(g) * u).astype(BF16)
    y = _dot(a_scr[...], wout_ref[...])
    out = x + _gate_rows(0.5 * gt, y, nb)
    if final:
        out = out * lax.rsqrt(jnp.mean(out * out, axis=-1, keepdims=True) + EPS) * fg_ref[...]
    o_ref[...] = out


def _ffn(x, mods, norm_g, w_in, w_out, sub, rows_per_seq, final_g=None):
    n, d = x.shape
    d_ff = w_out.shape[0]
    tm, nb = _row_tiling(n, rows_per_seq, 512)
    final = final_g is not None
    in_specs = [pl.BlockSpec((tm, d), lambda i: (i, 0)),
                _mods_spec(tm, nb, rows_per_seq),
                _const_spec((1, d)), _const_spec(w_in.shape), _const_spec(w_out.shape)]
    args = [x, mods, norm_g.reshape(1, d), w_in, w_out]
    if final:
        in_specs.append(_const_spec((1, d)))
        args.append(final_g.reshape(1, d))
    return pl.pallas_call(
        functools.partial(_ffn_kernel, sub=sub, nb=nb, d_ff=d_ff, final=final),
        grid=(n // tm,),
        in_specs=in_specs,
        out_specs=pl.BlockSpec((tm, d), lambda i: (i, 0)),
        out_shape=jax.ShapeDtypeStruct((n, d), F32),
        scratch_shapes=[pltpu.VMEM((tm, d_ff), BF16)],
        compiler_params=_cparams(("parallel",)),
    )(*args)


EVEN_W = 768 + 256 + 1792 + 128
_RW_OFF = 1056
_RW_SEGS = [(0, 512), (576, 1088), (1088, 1600), (512, 576), (1600, 1664), (1664, 1792)]
_RW_INV_SEGS = [(0, 512), (1536, 1600), (512, 1024), (1024, 1536), (1600, 1664), (1664, 1792)]


def _rw_permute(a):
    return jnp.concatenate([a[..., s:e] for s, e in _RW_SEGS], axis=-1)


def _rw_unpermute(a):
    return jnp.concatenate([a[..., s:e] for s, e in _RW_INV_SEGS], axis=-1)


def _rope_slot(v, c, s1, s2):
    w = v.shape[-1]
    return v * c + pltpu.roll(v, w - 16, axis=1) * s1 + pltpu.roll(v, 16, axis=1) * s2


def _even_in_kernel(x_ref, mods_ref, ng_ref, w_ref, qn_ref, kvn_ref, wuq_ref,
                    cq_ref, s1q_ref, s2q_ref, ck_ref, s1k_ref, s2k_ref,
                    q_out, ckv_out, kr_out, prw_out, *, nb):
    x = x_ref[...]
    h = _norm_mod(x, ng_ref[...], mods_ref[:, 3, :], mods_ref[:, 4, :], nb).astype(BF16)
    cq = _dot(h, w_ref[:, 0:768])
    cqn = (cq * lax.rsqrt(jnp.mean(cq * cq, axis=-1, keepdims=True) + EPS) * qn_ref[...]).astype(BF16)
    q = _dot(cqn, wuq_ref[...])
    rep = lambda t: jnp.concatenate([t] * MLA_HEADS, axis=1)
    q = _rope_slot(q, rep(cq_ref[...]), rep(s1q_ref[...]), rep(s2q_ref[...]))
    q_out[...] = (q * (MLA_SCALE * LOG2E)).astype(BF16)
    ckv = _dot(h, w_ref[:, 768:1024])
    ckv_out[...] = ckv * lax.rsqrt(jnp.mean(ckv * ckv, axis=-1, keepdims=True) + EPS) * kvn_ref[...]
    prw_out[...] = _dot(h, w_ref[:, 1024:2816])
    krs = _dot(h, w_ref[:, 2816:2944])
    krs = _rope_slot(krs, ck_ref[...], s1k_ref[...], s2k_ref[...])
    kr_out[...] = krs[:, 0:MLA_ROPE]


def _even_in(x, mods, norm_g, w_perm, q_norm, kv_norm, wuq_slot, tabs, rows_per_seq):
    n, d = x.shape
    tm, nb = _row_tiling(n, rows_per_seq, 512)
    ttab = tabs[0].shape[0]
    ntab = ttab // tm
    tab_spec = pl.BlockSpec((tm, LANES), lambda i: (i % ntab, 0))
    row = lambda w: pl.BlockSpec((tm, w), lambda i: (i, 0))
    return pl.pallas_call(
        functools.partial(_even_in_kernel, nb=nb),
        grid=(n // tm,),
        in_specs=[row(d), _mods_spec(tm, nb, rows_per_seq), _const_spec((1, d)),
                  _const_spec(w_perm.shape), _const_spec((1, 768)), _const_spec((1, 256)),
                  _const_spec(wuq_slot.shape)] + [tab_spec] * 6,
        out_specs=[row(1024), row(256), row(MLA_ROPE), row(RW_COLS)],
        out_shape=[jax.ShapeDtypeStruct((n, 1024), BF16), jax.ShapeDtypeStruct((n, 256), F32),
                   jax.ShapeDtypeStruct((n, MLA_ROPE), F32), jax.ShapeDtypeStruct((n, RW_COLS), F32)],
        compiler_params=_cparams(("parallel",)),
    )(x, mods, norm_g.reshape(1, d), w_perm, q_norm.reshape(1, 768), kv_norm.reshape(1, 256),
      wuq_slot, *tabs)


def _rope_tables(pos, tile_to):
    half = MLA_ROPE // 2
    freqs = ROPE_BASE ** (-jnp.arange(half, dtype=F32) / half)
    ang = pos.astype(F32)[:, None] * freqs[None, :]
    cos, sin = jnp.cos(ang), jnp.sin(ang)
    t = pos.shape[0]
    z = lambda w: jnp.zeros((t, w), F32)
    o = lambda w: jnp.ones((t, w), F32)
    cq = jnp.concatenate([o(64), cos, cos, z(32)], axis=1)
    s1q = jnp.concatenate([z(64), -sin, z(48)], axis=1)
    s2q = jnp.concatenate([z(80), sin, z(32)], axis=1)
    ck = jnp.concatenate([cos, cos, z(96)], axis=1)
    s1k = jnp.concatenate([-sin, z(112)], axis=1)
    s2k = jnp.concatenate([z(16), sin, z(96)], axis=1)
    tabs = [cq, s1q, s2q, ck, s1k, s2k]
    if tile_to > t:
        tabs = [jnp.tile(a, (tile_to // t, 1)) for a in tabs]
    return tabs


def _kv_expand_kernel(ckv_ref, kr_ref, wk_ref, sel_ref, wv_ref, one_ref, k_out, v_out):
    c = ckv_ref[...].astype(BF16)
    k = _dot(c, wk_ref[...]) + _dot(kr_ref[...].astype(BF16), sel_ref[...])
    k_out[...] = k.astype(BF16)
    v_out[0] = (_dot_nt(wv_ref[...], c) + one_ref[...]).astype(BF16)


def _kv_expand(ckv, kr, wk_slot, sel, wvt_slot, one_col, batch):
    n = ckv.shape[0]
    t_k = n // batch
    tm = 1024 if t_k % 1024 == 0 else 512
    assert t_k % tm == 0
    per = t_k // tm
    row = lambda w: pl.BlockSpec((tm, w), lambda i: (i, 0))
    return pl.pallas_call(
        _kv_expand_kernel,
        grid=(n // tm,),
        in_specs=[row(256), row(MLA_ROPE), _const_spec(wk_slot.shape), _const_spec(sel.shape),
                  _const_spec(wvt_slot.shape), _const_spec(one_col.shape)],
        out_specs=[row(1024), pl.BlockSpec((1, 1024, tm), lambda i: (i // per, 0, i % per))],
        out_shape=[jax.ShapeDtypeStruct((n, 1024), BF16), jax.ShapeDtypeStruct((batch, 1024, t_k), BF16)],
        compiler_params=_cparams(("parallel",)),
    )(ckv, kr, wk_slot, sel, wvt_slot, one_col)


def _mla_attn_kernel(qi_ref, ki_ref, fl_ref, q_ref, k_ref, vt_ref, o_ref, m_scr, acc_scr,
                     *, tq, tk, q_off):
    p_id = pl.program_id(1)
    flags = fl_ref[p_id]
    first = (flags & 1) != 0
    last = (flags & 2) != 0
    masked = (flags & 4) != 0

    @pl.when(first)
    def _():
        m_scr[...] = jnp.full(m_scr.shape, NEG, F32)
        acc_scr[...] = jnp.zeros(acc_scr.shape, F32)

    qb = min(MLA_QUERY_BLOCK, tq)
    insts = [(h, c0) for h in range(MLA_HEADS) for c0 in range(0, tq, qb)]

    def scores_t(h, c0):
        sl = slice(h * LANES, (h + 1) * LANES)
        return _dot_nt(k_ref[0, :, sl], q_ref[0, c0:c0 + qb, sl])

    def body(use_mask):
        if use_mask:
            q_chunk0 = (q_off + qi_ref[p_id] * tq) // CHUNK
            k_chunk0 = (ki_ref[p_id] * tk) // CHUNK
            kc = k_chunk0 + lax.broadcasted_iota(jnp.int32, (tk, qb), 0) // CHUNK
            qc_local = lax.broadcasted_iota(jnp.int32, (tk, qb), 1) // CHUNK
        ahead = [scores_t(*insts[i]) for i in range(min(MLA_SCORES_AHEAD, len(insts)))]
        for idx, (h, c0) in enumerate(insts):
            s = ahead.pop(0)
            if idx + MLA_SCORES_AHEAD < len(insts):
                ahead.append(scores_t(*insts[idx + MLA_SCORES_AHEAD]))
            if use_mask:
                s = jnp.where(kc <= qc_local + (q_chunk0 + c0 // CHUNK), s, NEG)
            m_prev = m_scr[h:h + 1, c0:c0 + qb]
            m_new = jnp.maximum(m_prev, jnp.max(s, axis=0, keepdims=True))
            m_scr[h:h + 1, c0:c0 + qb] = m_new
            alpha = jnp.exp2(m_prev - m_new)
            p_t = jnp.exp2(s - m_new).astype(BF16)
            vt_h = vt_ref[0, h * LANES:(h + 1) * LANES, :]
            acc_scr[h, :, c0:c0 + qb] = alpha * acc_scr[h, :, c0:c0 + qb] + _dot(vt_h, p_t)

    @pl.when(masked)
    def _():
        body(True)

    @pl.when(jnp.logical_not(masked))
    def _():
        body(False)

    @pl.when(last)
    def _():
        outs = []
        for h in range(MLA_HEADS):
            a = acc_scr[h]
            outs.append(a[0:MLA_V] / a[MLA_V:MLA_V + 1])
        o_ref[0] = jnp.concatenate(outs, axis=0).T.astype(BF16)


MLA_Q_TILE = 1024
MLA_KV_TILE = 512
MLA_QUERY_BLOCK = 256
MLA_SCORES_AHEAD = 3


def _mla_attn(q, k, vt, pairs, tq, tk, q_off):
    b, t_q, _ = q.shape
    qi, ki, fl = pairs
    grid_spec = pltpu.PrefetchScalarGridSpec(
        num_scalar_prefetch=3,
        grid=(b, qi.shape[0]),
        in_specs=[pl.BlockSpec((1, tq, 1024), lambda bb, p, qi, ki, fl: (bb, qi[p], 0)),
                  pl.BlockSpec((1, tk, 1024), lambda bb, p, qi, ki, fl: (bb, ki[p], 0)),
                  pl.BlockSpec((1, 1024, tk), lambda bb, p, qi, ki, fl: (bb, 0, ki[p]))],
        out_specs=pl.BlockSpec((1, tq, 512), lambda bb, p, qi, ki, fl: (bb, qi[p], 0)),
        scratch_shapes=[pltpu.VMEM((MLA_HEADS, tq), F32),
                        pltpu.VMEM((MLA_HEADS, LANES, tq), F32)])
    return pl.pallas_call(
        functools.partial(_mla_attn_kernel, tq=tq, tk=tk, q_off=q_off),
        grid_spec=grid_spec,
        out_shape=jax.ShapeDtypeStruct((b, t_q, 512), BF16),
        compiler_params=_cparams(("parallel", "arbitrary")),
    )(qi, ki, fl, q, k, vt)


def _causal_pairs(nq, ratio):
    qi, ki, fl = [], [], []
    for a in range(nq):
        n_kv = (a + 1) * ratio
        for c in range(n_kv):
            qi.append(a)
            ki.append(c)
            fl.append((1 if c == 0 else 0) | (2 if c == n_kv - 1 else 0) | (4 if c >= a * ratio else 0))
    return tuple(jnp.asarray(np.array(z, np.int32)) for z in (qi, ki, fl))


def _full_pairs(nk):
    qi = [0] * nk
    ki = list(range(nk))
    fl = [(1 if c == 0 else 0) | (6 if c == nk - 1 else 0) for c in range(nk)]
    return tuple(jnp.asarray(np.array(z, np.int32)) for z in (qi, ki, fl))


RW_CHUNK_GROUP = 2
RW_TIME_TILE = 256


def _pair_sum(x, ones_bd):
    hi, lo = _split2(x)
    return _dot(hi, ones_bd) + _dot(lo, ones_bd)


def _rwkv_kernel(pr_ref, sh0_ref, s0_ref, mu_ref, w0_ref, wl_ref, a0_ref, al_ref, g2_ref,
                 kk_ref, ka_ref, rk_ref, lnw_ref, lnb_ref, y_ref, st_ref,
                 prev_scr, s_scr, rt_scr, at_scr, bt_scr, kt_scr, bv_scr, k2_scr, v_scr,
                 cum_scr, yc_scr, *, tt):
    t_id = pl.program_id(1)
    n_pairs = RW_HEADS // 2

    @pl.when(t_id == 0)
    def _():
        s_scr[...] = s0_ref[0]
        prev_scr[0:1, :] = sh0_ref[0]

    pr = pr_ref[0]
    row = lax.broadcasted_iota(jnp.int32, (tt, 1), 0)
    prev = jnp.where(row == 0, prev_scr[0:1, :], pltpu.roll(pr, 1, axis=0))
    prev_scr[0:1, :] = pr[tt - 1:tt, :]
    pm = pr + (prev - pr) * mu_ref[...]
    r = pm[:, 0:512]
    k = pm[:, 512:1024]
    v = pm[:, 1024:1536]
    wa = pm[:, 1536:1664]
    g_in = pm[:, 1664:1792]
    z = w0_ref[...] + _dot(jnp.tanh(wa).astype(BF16), wl_ref[...])
    nz = -z
    w = -(jnp.maximum(nz, 0.0) + jnp.log(1.0 + jnp.exp(-jnp.abs(nz)))) - 0.5
    ld = -jnp.exp(w)
    a_sig = jax.nn.sigmoid(a0_ref[...] + _dot(wa.astype(BF16), al_ref[...]))
    g = _dot(jax.nn.sigmoid(g_in).astype(BF16), g2_ref[...])

    li = lax.broadcasted_iota(jnp.int32, (LANES, LANES), 0)
    lj = lax.broadcasted_iota(jnp.int32, (LANES, LANES), 1)
    ones_bd = jnp.where((li // RW_N) == (lj // RW_N), 1.0, 0.0).astype(BF16)

    def head_sum(x):
        return jnp.concatenate([_pair_sum(x[:, p * LANES:(p + 1) * LANES], ones_bd)
                                for p in range(n_pairs)], axis=1)

    kk = k * kk_ref[...]
    kk = kk * lax.rsqrt(jnp.maximum(head_sum(kk * kk), 1e-24))
    k2 = k * (1.0 + (a_sig - 1.0) * ka_ref[...])
    bonus = head_sum(r * k2 * rk_ref[...]) * v

    ti = lax.broadcasted_iota(jnp.int32, (tt, tt), 0)
    tj = lax.broadcasted_iota(jnp.int32, (tt, tt), 1)
    tri = jnp.where(((ti // CHUNK) == (tj // CHUNK)) & (tj <= ti), 1.0, 0.0).astype(BF16)
    l1 = ld.astype(BF16)
    rem = ld - l1.astype(F32)
    l2 = rem.astype(BF16)
    l3 = (rem - l2.astype(F32)).astype(BF16)
    cum = (_dot(tri, l3) + _dot(tri, l2)) + _dot(tri, l1)
    winv = jnp.exp(-cum)
    rt_scr[...] = r * jnp.exp(cum)
    at_scr[...] = -kk * jnp.exp(cum - ld)
    bv = kk * a_sig
    bt_scr[...] = bv * winv
    kt_scr[...] = k2 * winv
    bv_scr[...] = bv
    k2_scr[...] = k2
    v_scr[...] = v
    cum_scr[...] = cum

    lane = lax.broadcasted_iota(jnp.int32, (CHUNK, LANES), 1)
    even = lane < RW_N
    strict = (lj % RW_N) < (li % RW_N)
    incl = (lj % RW_N) <= (li % RW_N)
    eye = jnp.where(li == lj, 1.0, 0.0).astype(F32)
    pairs = range(n_pairs)

    def same_block(m):
        return (li // m) == (lj // m)

    def stack_f32(x):
        return jnp.concatenate([jnp.where(even, x, 0.0), jnp.where(even, 0.0, x)], axis=0)

    def stack(x):
        return stack_f32(x).astype(BF16)

    mm = lambda a, b: _dot(a.astype(BF16), b.astype(BF16))
    rows_of = lambda c: slice(c * CHUNK, (c + 1) * CHUNK)
    lanes_of = lambda p: slice(p * LANES, (p + 1) * LANES)

    def independent_phase(chunks, res):
        insts = [(c, p) for c in chunks for p in pairs]
        load = lambda scr: [scr[rows_of(c), lanes_of(p)] for c, p in insts]
        cum_c = load(cum_scr)
        cum_l = [a[CHUNK - 1:CHUNK, :] for a in cum_c]
        w2 = [jnp.exp(a - b) for a, b in zip(cum_l, cum_c)]
        rs = [stack(a) for a in load(rt_scr)]
        as_ = [stack(a) for a in load(at_scr)]
        bs = [stack(a) for a in load(bt_scr)]
        ks = [stack(a) for a in load(kt_scr)]
        v_f = [stack_f32(a) for a in load(v_scr)]
        vs = [a.astype(BF16) for a in v_f]
        b2s = [stack(a * w) for a, w in zip(load(bv_scr), w2)]
        k2s = [stack(a * w) for a, w in zip(load(k2_scr), w2)]
        yield
        n_m = [jnp.where(strict, _dot_nt(a, b), 0.0) for a, b in zip(as_, bs)]
        mk = [jnp.where(strict, _dot_nt(a, b), 0.0).astype(BF16) for a, b in zip(as_, ks)]
        cb = [jnp.where(incl, _dot_nt(a, b), 0.0).astype(BF16) for a, b in zip(rs, bs)]
        ck = [jnp.where(incl, _dot_nt(a, b), 0.0).astype(BF16) for a, b in zip(rs, ks)]
        yield
        n8 = [jnp.where(same_block(8), n, 0.0) for n in n_m]
        t = [eye + a for a in n8]
        p2 = [mm(a, a) for a in n8]
        mv = [_dot(a, b) for a, b in zip(mk, vs)]
        yield
        t = [a + mm(a, b) for a, b in zip(t, p2)]
        p4 = [mm(a, a) for a in p2]
        cv = [_dot(a, b) for a, b in zip(ck, vs)]
        yield
        t = [a + mm(a, b) for a, b in zip(t, p4)]
        vk = [_dot(a.T.astype(BF16), b) for a, b in zip(v_f, k2s)]
        yield
        for m in (8, 16, 32):
            off = same_block(2 * m) & jnp.logical_not(same_block(m))
            x = [mm(jnp.where(off, n, 0.0), a) for n, a in zip(n_m, t)]
            yield
            t = [a + mm(a, b) for a, b in zip(t, x)]
            yield
        for i, key in enumerate(insts):
            res[key] = dict(as_=as_[i], rs=rs[i], b2s=b2s[i], cb=cb[i], mv=mv[i], cv=cv[i], vk=vk[i],
                            tinv=t[i].astype(BF16), wl=jnp.exp(cum_l[i]))

    def dependent_phase(chunks, res):
        for c in chunks:
            rc = [res[(c, p)] for p in pairs]
            s = [s_scr[p] for p in pairs]
            sb = [a.astype(BF16) for a in s]
            x = [_dot_nt(r_['as_'], b) + r_['mv'] for r_, b in zip(rc, sb)]
            rs_s = [_dot_nt(r_['rs'], b) + r_['cv'] for r_, b in zip(rc, sb)]
            yield
            u = [_dot(r_['tinv'], a.astype(BF16)) for r_, a in zip(rc, x)]
            yield
            ys = [a + _dot(r_['cb'], b.astype(BF16)) for a, r_, b in zip(rs_s, rc, u)]
            for p in pairs:
                yc_scr[rows_of(c), lanes_of(p)] = ys[p][0:CHUNK] + ys[p][CHUNK:2 * CHUNK]
                s_scr[p] = s[p] * rc[p]['wl'] + _dot(u[p].T.astype(BF16), rc[p]['b2s']) + rc[p]['vk']
            yield

    def emit(*gens):
        gens = list(gens)
        while gens:
            for gen in list(gens):
                try:
                    next(gen)
                except StopIteration:
                    gens.remove(gen)

    n_chunks = tt // CHUNK
    groups = [list(range(c0, min(c0 + RW_CHUNK_GROUP, n_chunks))) for c0 in range(0, n_chunks, RW_CHUNK_GROUP)]
    res = {}
    emit(independent_phase(groups[0], res))
    for gi in range(1, len(groups)):
        emit(independent_phase(groups[gi], res), dependent_phase(groups[gi - 1], res))
    emit(dependent_phase(groups[-1], res))

    y = yc_scr[...]
    mean = head_sum(y) * (1.0 / RW_N)
    dlt = y - mean
    var = head_sum(dlt * dlt) * (1.0 / RW_N)
    yn = dlt * lax.rsqrt(var + RW_GN_EPS) * lnw_ref[...] + lnb_ref[...]
    y_ref[0] = ((yn + bonus) * g).astype(BF16)

    @pl.when(t_id == pl.num_programs(1) - 1)
    def _():
        st_ref[0] = s_scr[...]


def _rwkv(prw, sh0, s0_bd, wts):
    b, t, _ = prw.shape
    tt = min(RW_TIME_TILE, t)
    assert t % tt == 0 and tt % CHUNK == 0
    c512 = _const_spec((1, RW_C))
    scr = lambda: pltpu.VMEM((tt, RW_C), F32)
    return pl.pallas_call(
        functools.partial(_rwkv_kernel, tt=tt),
        grid=(b, t // tt),
        in_specs=[pl.BlockSpec((1, tt, RW_COLS), lambda bb, i: (bb, i, 0)),
                  pl.BlockSpec((1, 1, RW_COLS), lambda bb, i: (bb, 0, 0)),
                  pl.BlockSpec((1, 4, LANES, LANES), lambda bb, i: (bb, 0, 0, 0)),
                  _const_spec((1, RW_COLS)), c512, _const_spec((LANES, RW_C)), c512,
                  _const_spec((LANES, RW_C)), _const_spec((LANES, RW_C)),
                  c512, c512, c512, c512, c512],
        out_specs=[pl.BlockSpec((1, tt, RW_C), lambda bb, i: (bb, i, 0)),
                   pl.BlockSpec((1, 4, LANES, LANES), lambda bb, i: (bb, 0, 0, 0))],
        out_shape=[jax.ShapeDtypeStruct((b, t, RW_C), BF16),
                   jax.ShapeDtypeStruct((b, 4, LANES, LANES), F32)],
        scratch_shapes=[pltpu.VMEM((8, RW_COLS), F32), pltpu.VMEM((4, LANES, LANES), F32)]
                       + [scr() for _ in range(9)],
        compiler_params=_cparams(("parallel", "arbitrary")),
    )(prw, sh0, s0_bd, *wts)


def _state_to_bd(s):
    b = s.shape[0]
    s = s.reshape(b, 4, 2, RW_N, RW_N)
    z = jnp.zeros_like(s[:, :, 0])
    top = jnp.concatenate([s[:, :, 0], z], axis=-1)
    bot = jnp.concatenate([z, s[:, :, 1]], axis=-1)
    return jnp.concatenate([top, bot], axis=-2)


def _state_from_bd(s):
    b = s.shape[0]
    return jnp.stack([s[:, :, :RW_N, :RW_N], s[:, :, RW_N:, RW_N:]], axis=2).reshape(b, RW_HEADS, RW_N, RW_N)


def _out_proj_kernel(*refs, n_in, nb):
    x_ref, mods_ref = refs[0], refs[1]
    a_refs = refs[2:2 + n_in]
    w_refs = refs[2 + n_in:2 + 2 * n_in]
    o_ref = refs[2 + 2 * n_in]
    y = _dot(a_refs[0][...], w_refs[0][...])
    for a, w in zip(a_refs[1:], w_refs[1:]):
        y = y + _dot(a[...], w[...])
    o_ref[...] = x_ref[...] + _gate_rows(mods_ref[:, 5, :], y, nb)


def _out_proj(x, mods, acts, ws, rows_per_seq):
    n, d = x.shape
    tm, nb = _row_tiling(n, rows_per_seq, 512)
    row = lambda w: pl.BlockSpec((tm, w), lambda i: (i, 0))
    return pl.pallas_call(
        functools.partial(_out_proj_kernel, n_in=len(acts), nb=nb),
        grid=(n // tm,),
        in_specs=[row(d), _mods_spec(tm, nb, rows_per_seq)] + [row(a.shape[1]) for a in acts]
                 + [_const_spec(w.shape) for w in ws],
        out_specs=row(d),
        out_shape=jax.ShapeDtypeStruct((n, d), F32),
        compiler_params=_cparams(("parallel",)),
    )(x, mods, *acts, *ws)


_ODD_SEGS = ([(1024 + g * 64, 1088 + g * 64) for g in range(SW_KV_HEADS) for _ in range(2)]
             + [(1280 + g * 64, 1344 + g * 64) for g in range(SW_KV_HEADS) for _ in range(2)])


def _odd_in_kernel(x_ref, mods_ref, ng_ref, w_ref, b_ref, q_out, k_out, v_out, *, nb):
    h = _norm_mod(x_ref[...], ng_ref[...], mods_ref[:, 3, :], mods_ref[:, 4, :], nb).astype(BF16)
    q = _dot(h, w_ref[:, 0:1024]) + b_ref[:, 0:1024]
    q_out[...] = (q * (SW_HD ** -0.5 * LOG2E)).astype(BF16)
    k_out[...] = _dot(h, w_ref[:, 1024:1536]) + b_ref[:, 1024:1536]
    v_out[...] = _dot(h, w_ref[:, 1536:2048]) + b_ref[:, 1536:2048]


def _odd_in(x, mods, norm_g, w_perm, b_perm, rows_per_seq):
    n, d = x.shape
    tm, nb = _row_tiling(n, rows_per_seq, 512)
    row = lambda w: pl.BlockSpec((tm, w), lambda i: (i, 0))
    return pl.pallas_call(
        functools.partial(_odd_in_kernel, nb=nb),
        grid=(n // tm,),
        in_specs=[row(d), _mods_spec(tm, nb, rows_per_seq), _const_spec((1, d)),
                  _const_spec(w_perm.shape), _const_spec((1, 2048))],
        out_specs=[row(1024), row(512), row(512)],
        out_shape=[jax.ShapeDtypeStruct((n, 1024), BF16), jax.ShapeDtypeStruct((n, 512), F32),
                   jax.ShapeDtypeStruct((n, 512), F32)],
        compiler_params=_cparams(("parallel",)),
    )(x, mods, norm_g.reshape(1, d), w_perm, b_perm.reshape(1, 2048))


def _swa_kernel(sinks_ref, q_ref, kp_ref, kc_ref, vp_ref, vc_ref, o_ref, *, tq, mask_first_prev):
    nk = WINDOW + tq
    qi = lax.broadcasted_iota(jnp.int32, (tq, nk), 0)
    kj = lax.broadcasted_iota(jnp.int32, (tq, nk), 1)
    kc = kj // CHUNK - WINDOW // CHUNK
    qc = qi // CHUNK
    vis = (kc <= qc) & (kc >= qc - WINDOW // CHUNK)
    if mask_first_prev:
        vis = vis & ((kj >= WINDOW) | (pl.program_id(1) > 0))
    ndist = jnp.where(vis, -jnp.abs(qi + WINDOW - kj).astype(F32), NEG)
    keys = jnp.concatenate([kp_ref[0], kc_ref[0]], axis=0).astype(BF16)
    vals = jnp.concatenate([vp_ref[0], vc_ref[0]], axis=0)
    klane = lax.broadcasted_iota(jnp.int32, (nk, LANES), 1)
    v_slots = [jnp.where(klane < SW_HD, vals[:, g * LANES:(g + 1) * LANES], 1.0).astype(BF16)
               for g in range(SW_KV_HEADS)]
    lane = lax.broadcasted_iota(jnp.int32, (tq, LANES), 1)
    low = lane < SW_HD
    sb = min(SWA_SUB_BLOCK, tq)

    def scores(h):
        qp = q_ref[0, :, (h // 2) * LANES:(h // 2 + 1) * LANES]
        qh = jnp.where(low if h % 2 == 0 else jnp.logical_not(low), qp, jnp.zeros_like(qp))
        return _dot_nt(qh, keys[:, (h // SW_GROUP) * LANES:(h // SW_GROUP + 1) * LANES])

    outs = []
    s_next = scores(0)
    for h in range(SW_HEADS):
        s = s_next
        if h + 1 < SW_HEADS:
            s_next = scores(h + 1)
        slope = (2.0 ** (-8.0 * (h + 1) / SW_HEADS)) * LOG2E
        sk = sinks_ref[h] * LOG2E
        es, ms = [], []
        for r1 in range(0, tq, sb):
            z = s[r1:r1 + sb] + slope * ndist[r1:r1 + sb]
            m = jnp.maximum(jnp.broadcast_to(jnp.max(z, axis=-1, keepdims=True), (sb, LANES)), sk)
            m_wide = jnp.concatenate([m] * (nk // LANES), axis=1) if nk % LANES == 0 else m[:, 0:1]
            es.append(jnp.exp2(z - m_wide).astype(BF16))
            ms.append(m)
        e = jnp.concatenate(es, axis=0) if len(es) > 1 else es[0]
        m = jnp.concatenate(ms, axis=0) if len(ms) > 1 else ms[0]
        pv = _dot(e, v_slots[h // SW_GROUP])
        outs.append(pv / (pltpu.roll(pv, SW_HD, axis=1) + jnp.exp2(sk - m)))
    for pr in range(SW_HEADS // 2):
        o_ref[0, :, pr * LANES:(pr + 1) * LANES] = jnp.where(
            low, outs[2 * pr], pltpu.roll(outs[2 * pr + 1], SW_HD, axis=1)).astype(BF16)


SWA_SUB_BLOCK = 32


def _swa(q, k_prev, k_cur, v_prev, v_cur, sinks, tq, same_array):
    b, t, _ = q.shape
    nt = t // tq
    per = tq // WINDOW
    if same_array:
        prev_map = lambda bb, i: (bb, jnp.maximum(i * per - 1, 0), 0)
    else:
        prev_map = lambda bb, i: (bb, 0, 0)
    cur = lambda w: pl.BlockSpec((1, tq, w), lambda bb, i: (bb, i, 0))
    prev = pl.BlockSpec((1, WINDOW, 512), prev_map)
    return pl.pallas_call(
        functools.partial(_swa_kernel, tq=tq, mask_first_prev=same_array),
        grid=(b, nt),
        in_specs=[pl.BlockSpec(memory_space=pltpu.SMEM), cur(1024), prev, cur(512), prev, cur(512)],
        out_specs=cur(1024),
        out_shape=jax.ShapeDtypeStruct((b, t, 1024), BF16),
        compiler_params=_cparams(("parallel", "parallel")),
    )(sinks, q, k_prev, k_cur, v_prev, v_cur)


def _undup(a):
    return a.reshape(a.shape[:-1] + (SW_KV_HEADS, 2, SW_HD))[..., 0, :]


def _dup(a):
    return jnp.concatenate([a, a], axis=-1).reshape(a.shape[:-2] + (512,))


def _prep_weights(p):
    depth = p['w_ada'].shape[0]
    n_even, n_odd = (depth + 1) // 2, depth // 2
    w = {}
    w['ffn_in'] = p['ffn_w_in'].astype(BF16)
    w['ffn_out'] = p['ffn_w_out'].astype(BF16)
    wi = p['even_w_in'].astype(BF16)
    w['even_in'] = jnp.concatenate(
        [wi[:, :, 0:1024]] + [wi[:, :, _RW_OFF + a:_RW_OFF + b] for a, b in _RW_SEGS]
        + [wi[:, :, 1024:1056], jnp.zeros((n_even, 1024, 96), BF16)], axis=2)
    w['wuq'] = jnp.pad(p['mla_w_uq'], ((0, 0), (0, 0), (0, 0), (0, 32))).reshape(n_even, 768, 1024).astype(BF16)
    w['wk'] = jnp.pad(p['mla_w_ukv'][..., :MLA_NOPE], ((0, 0), (0, 0), (0, 0), (0, 64))
                      ).reshape(n_even, 256, 1024).astype(BF16)
    w['wv'] = jnp.swapaxes(jnp.pad(p['mla_w_ukv'][..., MLA_NOPE:], ((0, 0), (0, 0), (0, 0), (0, 64))
                                   ).reshape(n_even, 256, 1024), 1, 2).astype(BF16)
    sel = np.zeros((MLA_ROPE, 1024), np.float32)
    one = np.zeros((1024, 1), np.float32)
    for h in range(MLA_HEADS):
        sel[np.arange(MLA_ROPE), h * LANES + MLA_NOPE + np.arange(MLA_ROPE)] = 1.0
        one[h * LANES + MLA_V, 0] = 1.0
    w['sel'] = jnp.asarray(sel).astype(BF16)
    w['one'] = jnp.asarray(one)
    z64 = jnp.zeros((n_even, 64, RW_C), F32)
    w['wl'] = jnp.concatenate([p['rw_w2'], z64], axis=1).astype(BF16)
    w['al'] = jnp.concatenate([z64, p['rw_a2']], axis=1).astype(BF16)
    w['g2'] = p['rw_g2'].astype(BF16)
    w['mu'] = _rw_permute(p['rw_mu'])
    w['even_out'] = p['even_w_out'].astype(BF16)
    wo = p['odd_w_qkv'].astype(BF16)
    w['odd_in'] = jnp.concatenate([wo[:, :, 0:1024]] + [wo[:, :, a:b] for a, b in _ODD_SEGS], axis=2)
    bo = p['odd_b_qkv']
    w['odd_b'] = jnp.concatenate([bo[:, 0:1024]] + [bo[:, a:b] for a, b in _ODD_SEGS], axis=1)
    w['odd_out'] = p['odd_w_out'].astype(BF16)
    return w


def _trunk(x3, mods_all, start, past, p, w):
    b, t, d = x3.shape
    n = b * t
    depth = mods_all.shape[0]
    x = x3.reshape(n, d)
    rows = t
    tm_even, _ = _row_tiling(n, rows, 512)
    pos = start + jnp.arange(t)
    tabs = _rope_tables(pos, tm_even)
    even_states, odd_states = [], []
    for i in range(depth):
        mods = mods_all[i]
        j = i // 2
        x = _ffn(x, mods, p['norm_g'][i, 0], w['ffn_in'][i, 0], w['ffn_out'][i, 0], 0, rows)
        if i % 2 == 0:
            q, ckv, kr, prw = _even_in(x, mods, p['norm_g'][i, 1], w['even_in'][j], p['mla_q_norm'][j],
                                       p['mla_kv_norm'][j], w['wuq'][j], tabs, rows)
            if past is None:
                kx, vtx = _kv_expand(ckv, kr, w['wk'][j], w['sel'], w['wv'][j], w['one'], b)
                tk = min(MLA_KV_TILE, t)
                tq = min(MLA_Q_TILE, t)
                att = _mla_attn(q.reshape(b, t, 1024), kx.reshape(b, t, 1024), vtx,
                                _causal_pairs(t // tq, tq // tk), tq, tk, 0)
                s0 = jnp.zeros((b, RW_HEADS, RW_N, RW_N), F32)
                sh0 = jnp.zeros((b, RW_COLS), F32)
            else:
                ckv_past, kr_past, s0, sh0 = past[0][j], past[1][j], past[2][j], past[3][j]
                n_past = ckv_past.shape[1]
                tk = 512
                t_all = -(-(n_past + t) // tk) * tk
                pad = t_all - n_past - t
                ckv_all = jnp.concatenate([ckv_past, ckv.reshape(b, t, 256),
                                           jnp.zeros((b, pad, 256), F32)], axis=1)
                kr_all = jnp.concatenate([kr_past, kr.reshape(b, t, MLA_ROPE),
                                          jnp.zeros((b, pad, MLA_ROPE), F32)], axis=1)
                kx, vtx = _kv_expand(ckv_all.reshape(b * t_all, 256), kr_all.reshape(b * t_all, MLA_ROPE),
                                     w['wk'][j], w['sel'], w['wv'][j], w['one'], b)
                tq_s = -(-t // LANES) * LANES
                q_pad = jnp.pad(q.reshape(b, t, 1024), ((0, 0), (0, tq_s - t), (0, 0)))
                att = _mla_attn(q_pad, kx.reshape(b, t_all, 1024), vtx,
                                _full_pairs(t_all // tk), tq_s, tk, start)[:, :t]
            rw_wts = (w['mu'][j].reshape(1, RW_COLS), p['rw_w0'][j].reshape(1, RW_C), w['wl'][j],
                      p['rw_a0'][j].reshape(1, RW_C), w['al'][j], w['g2'][j],
                      p['rw_k_k'][j].reshape(1, RW_C), p['rw_k_a'][j].reshape(1, RW_C),
                      p['rw_r_k'][j].reshape(1, RW_C), p['rw_ln_w'][j].reshape(1, RW_C),
                      p['rw_ln_b'][j].reshape(1, RW_C))
            prw3 = prw.reshape(b, t, RW_COLS)
            y_rw, s_bd = _rwkv(prw3, _rw_permute(sh0).reshape(b, 1, RW_COLS), _state_to_bd(s0), rw_wts)
            x = _out_proj(x, mods, [att.reshape(n, 512), y_rw.reshape(n, RW_C)],
                          [w['even_out'][j][:512], w['even_out'][j][512:]], rows)
            even_states.append((ckv.reshape(b, t, 256), kr.reshape(b, t, MLA_ROPE), _state_from_bd(s_bd),
                                _rw_unpermute(prw3[:, t - 1, :])))
        else:
            q, kd, vd = _odd_in(x, mods, p['norm_g'][i, 1], w['odd_in'][j], w['odd_b'][j], rows)
            q3, kd3, vd3 = q.reshape(b, t, 1024), kd.reshape(b, t, 512), vd.reshape(b, t, 512)
            if past is None:
                tq = min(256, t)
                o = _swa(q3, kd3, kd3, vd3, vd3, p['swa_sinks'][j], tq, True)
                keep = min(WINDOW, t)
                k_new, v_new = _undup(kd3[:, t - keep:]), _undup(vd3[:, t - keep:])
            else:
                k_past, v_past = past[4][j], past[5][j]
                o = _swa(q3, _dup(k_past), kd3, _dup(v_past), vd3, p['swa_sinks'][j], t, False)
                k_new = jnp.concatenate([k_past, _undup(kd3)], axis=1)[:, t:]
                v_new = jnp.concatenate([v_past, _undup(vd3)], axis=1)[:, t:]
            x = _out_proj(x, mods, [o.reshape(n, 1024)], [w['odd_out'][j]], rows)
            odd_states.append((k_new, v_new))
        fg = p['final_norm_g'] if i == depth - 1 else None
        x = _ffn(x, mods, p['norm_g'][i, 2], w['ffn_in'][i, 1], w['ffn_out'][i, 1], 2, rows, final_g=fg)
    es = [jnp.stack([st[k] for st in even_states]) for k in range(4)]
    os_ = [jnp.stack([st[k] for st in odd_states]) for k in range(2)]
    return x.reshape(b, t, d), es + os_


def kernel(x_prompt, x_sample, cache_mla_ckv, cache_mla_krope, state_rwkv, state_rwkv_shift, cache_swa_k, cache_swa_v, c_prompt, c_sample, w_ada, b_ada, norm_g, ffn_w_in, ffn_w_out, even_w_in, even_w_out, mla_q_norm, mla_kv_norm, mla_w_uq, mla_w_ukv, rw_mu, rw_w0, rw_w2, rw_a0, rw_a2, rw_g2, rw_k_k, rw_k_a, rw_r_k, rw_ln_w, rw_ln_b, odd_w_qkv, odd_b_qkv, odd_w_out, swa_sinks, final_norm_g):
    p = dict(w_ada=w_ada, b_ada=b_ada, norm_g=norm_g, ffn_w_in=ffn_w_in, ffn_w_out=ffn_w_out,
             even_w_in=even_w_in, even_w_out=even_w_out, mla_q_norm=mla_q_norm, mla_kv_norm=mla_kv_norm,
             mla_w_uq=mla_w_uq, mla_w_ukv=mla_w_ukv, rw_mu=rw_mu, rw_w0=rw_w0, rw_w2=rw_w2, rw_a0=rw_a0,
             rw_a2=rw_a2, rw_g2=rw_g2, rw_k_k=rw_k_k, rw_k_a=rw_k_a, rw_r_k=rw_r_k, rw_ln_w=rw_ln_w,
             rw_ln_b=rw_ln_b, odd_w_qkv=odd_w_qkv, odd_b_qkv=odd_b_qkv, odd_w_out=odd_w_out,
             swa_sinks=swa_sinks, final_norm_g=final_norm_g)
    w = _prep_weights(p)
    depth = w_ada.shape[0]
    bp, bs = c_prompt.shape[0], c_sample.shape[0]
    d = c_prompt.shape[1]
    b_pad = -(-(bp + bs) // 8) * 8
    c_all = jnp.concatenate([c_prompt, c_sample, jnp.zeros((b_pad - bp - bs, d), F32)], axis=0)
    mods = _ada(c_all, w_ada, b_ada).reshape(depth, b_pad, 3 * N_SUB, d)
    y_prompt, sp = _trunk(x_prompt, mods[:, :bp], 0, None, p, w)
    past = (cache_mla_ckv, cache_mla_krope, state_rwkv, state_rwkv_shift, cache_swa_k, cache_swa_v)
    y_sample, ss = _trunk(x_sample, mods[:, bp:bp + bs], cache_mla_ckv.shape[2], past, p, w)
    return (y_prompt, y_sample, sp[0], sp[1], sp[2], sp[3], sp[4], sp[5],
            ss[0], ss[1], ss[2], ss[3], ss[4], ss[5])
```

```python
import functools
import math

import jax
import jax.numpy as jnp
import numpy as np
from jax import lax
from jax.experimental import pallas as pl
from jax.experimental.pallas import tpu as pltpu

F32 = jnp.float32
BF16 = jnp.bfloat16

CHUNK = 64
EPS = 1e-6
NEG = -1e30
N_SUB = 3
MLA_HEADS = 8
MLA_NOPE = 64
MLA_ROPE = 32
MLA_V = 64
MLA_Q_LORA = 768
MLA_KV_LORA = 256
MLA_SCALE = (MLA_NOPE + MLA_ROPE) ** -0.5
LOG2E = math.log2(math.e)
ROPE_BASE = 10000.0
RW_HEADS = 8
RW_N = 64
RW_C = RW_HEADS * RW_N
RW_GN_EPS = 64e-5
RW_COLS = 3 * RW_C + 64 + 64 + 128
SW_HEADS = 16
SW_KV_HEADS = 4
SW_GROUP = 4
SW_HD = 64
WINDOW = 128

LANES = 128
VMEM_LIMIT = 56 * 1024 * 1024


def _cparams(sem):
    return pltpu.CompilerParams(dimension_semantics=sem, vmem_limit_bytes=VMEM_LIMIT)


def _const_spec(shape):
    nd = len(shape)
    return pl.BlockSpec(shape, lambda *_: (0,) * nd, pipeline_mode=pl.Buffered(1))


def _dot(a, b):
    return jnp.dot(a, b, preferred_element_type=F32)


def _dot_nt(a, b):
    return lax.dot_general(a, b, (((1,), (1,)), ((), ())), preferred_element_type=F32)


def _split2(x):
    hi = x.astype(BF16)
    lo = (x - hi.astype(F32)).astype(BF16)
    return hi, lo


def _norm_mod(x, g, sh, sc, nb):
    y = x * lax.rsqrt(jnp.mean(x * x, axis=-1, keepdims=True) + EPS) * g
    if nb == 1:
        return y * (1.0 + sc) + sh
    tm, d = x.shape
    y3 = y.reshape(nb, tm // nb, d)
    return (y3 * (1.0 + sc[:, None, :]) + sh[:, None, :]).reshape(tm, d)


def _gate_rows(g, y, nb):
    if nb == 1:
        return g * y
    tm, d = y.shape
    return (y.reshape(nb, tm // nb, d) * g[:, None, :]).reshape(tm, d)


def _row_tiling(n_rows, rows_per_seq, pref):
    if rows_per_seq >= pref:
        assert rows_per_seq % pref == 0
        return pref, 1
    tm = min(pref, n_rows)
    assert tm % rows_per_seq == 0 and n_rows % tm == 0
    return tm, tm // rows_per_seq


def _mods_spec(tm, nb, rows_per_seq):
    if nb == 1:
        tiles_per_seq = rows_per_seq // tm
        return pl.BlockSpec((1, 3 * N_SUB, 1024), lambda i: (i // tiles_per_seq, 0, 0))
    return pl.BlockSpec((nb, 3 * N_SUB, 1024), lambda i: (i, 0, 0))


def _ada_kernel(c_ref, w_ref, b_ref, o_ref):
    c = c_ref[...]
    cs = (c * jax.nn.sigmoid(c)).astype(BF16)
    o_ref[0] = _dot(cs, w_ref[0].astype(BF16)) + b_ref[0]


def _ada(c_all, w_ada, b_ada):
    depth, d, n = w_ada.shape
    bp = c_all.shape[0]
    tn = n // 4
    return pl.pallas_call(
        _ada_kernel,
        grid=(depth, n // tn),
        in_specs=[pl.BlockSpec((bp, d), lambda l, j: (0, 0)),
                  pl.BlockSpec((1, d, tn), lambda l, j: (l, 0, j)),
                  pl.BlockSpec((1, 1, tn), lambda l, j: (l, 0, j))],
        out_specs=pl.BlockSpec((1, bp, tn), lambda l, j: (l, 0, j)),
        out_shape=jax.ShapeDtypeStruct((depth, bp, n), F32),
        compiler_params=_cparams(("parallel", "parallel")),
    )(c_all, w_ada, b_ada.reshape(depth, 1, n))


FF_CHUNK = 256


def _ffn_kernel(x_ref, mods_ref, ng_ref, win_ref, wout_ref, *rest, sub, nb, d_ff, final, n_mix):
    mix_a, mix_w, rest = rest[:n_mix], rest[n_mix:2 * n_mix], rest[2 * n_mix:]
    if final:
        fg_ref, o_ref, a_scr = rest
    else:
        o_ref, a_scr = rest
    x = x_ref[...]
    if n_mix:
        ym = _dot(mix_a[0][...], mix_w[0][...])
        for a, w in zip(mix_a[1:], mix_w[1:]):
            ym = ym + _dot(a[...], w[...])
        x = x + _gate_rows(mods_ref[:, 5, :], ym, nb)
    sh = mods_ref[:, 3 * sub, :]
    sc = mods_ref[:, 3 * sub + 1, :]
    gt = mods_ref[:, 3 * sub + 2, :]
    h = _norm_mod(x, ng_ref[...], sh, sc, nb).astype(BF16)
    for c in range(d_ff // FF_CHUNK):
        lo = c * FF_CHUNK
        g = _dot(h, win_ref[:, lo:lo + FF_CHUNK])
        u = _dot(h, win_ref[:, d_ff + lo:d_ff + lo + FF_CHUNK])
        a_scr[:, lo:lo + FF_CHUNK] = (g * jax.nn.sigmoid(g) * u).astype(BF16)
    y = _dot(a_scr[...], wout_ref[...])
    out = x + _gate_rows(0.5 * gt, y, nb)
    if final:
        out = out * lax.rsqrt(jnp.mean(out * out, axis=-1, keepdims=True) + EPS) * fg_ref[...]
    o_ref[...] = out


def _ffn(x, mods, norm_g, w_in, w_out, sub, rows_per_seq, final_g=None, mix=()):
    n, d = x.shape
    d_ff = w_out.shape[0]
    tm, nb = _row_tiling(n, rows_per_seq, 512)
    final = final_g is not None
    in_specs = [pl.BlockSpec((tm, d), lambda i: (i, 0)),
                _mods_spec(tm, nb, rows_per_seq),
                _const_spec((1, d)), _const_spec(w_in.shape), _const_spec(w_out.shape)]
    args = [x, mods, norm_g.reshape(1, d), w_in, w_out]
    in_specs += [pl.BlockSpec((tm, a.shape[1]), lambda i: (i, 0)) for a, _ in mix]
    in_specs += [_const_spec(wm.shape) for _, wm in mix]
    args += [a for a, _ in mix] + [wm for _, wm in mix]
    if final:
        in_specs.append(_const_spec((1, d)))
        args.append(final_g.reshape(1, d))
    return pl.pallas_call(
        functools.partial(_ffn_kernel, sub=sub, nb=nb, d_ff=d_ff, final=final, n_mix=len(mix)),
        grid=(n // tm,),
        in_specs=in_specs,
        out_specs=pl.BlockSpec((tm, d), lambda i: (i, 0)),
        out_shape=jax.ShapeDtypeStruct((n, d), F32),
        scratch_shapes=[pltpu.VMEM((tm, d_ff), BF16)],
        compiler_params=_cparams(("parallel",)),
    )(*args)


EVEN_W = 768 + 256 + 1792 + 128
_RW_OFF = 1056
_RW_SEGS = [(0, 512), (576, 1088), (1088, 1600), (512, 576), (1600, 1664), (1664, 1792)]
_RW_INV_SEGS = [(0, 512), (1536, 1600), (512, 1024), (1024, 1536), (1600, 1664), (1664, 1792)]


def _rw_permute(a):
    return jnp.concatenate([a[..., s:e] for s, e in _RW_SEGS], axis=-1)


def _rw_unpermute(a):
    return jnp.concatenate([a[..., s:e] for s, e in _RW_INV_SEGS], axis=-1)


def _rope_slot(v, c, s1, s2):
    w = v.shape[-1]
    return v * c + pltpu.roll(v, w - 16, axis=1) * s1 + pltpu.roll(v, 16, axis=1) * s2


def _even_in_kernel(x_ref, mods_ref, ng_ref, w_ref, qn_ref, kvn_ref, wuq_ref,
                    cq_ref, s1q_ref, s2q_ref, ck_ref, s1k_ref, s2k_ref,
                    q_out, ckv_out, kr_out, prw_out, *, nb):
    x = x_ref[...]
    h = _norm_mod(x, ng_ref[...], mods_ref[:, 3, :], mods_ref[:, 4, :], nb).astype(BF16)
    cq = _dot(h, w_ref[:, 0:768])
    cqn = (cq * lax.rsqrt(jnp.mean(cq * cq, axis=-1, keepdims=True) + EPS) * qn_ref[...]).astype(BF16)
    q = _dot(cqn, wuq_ref[...])
    rep = lambda t: jnp.concatenate([t] * MLA_HEADS, axis=1)
    q = _rope_slot(q, rep(cq_ref[...]), rep(s1q_ref[...]), rep(s2q_ref[...]))
    q_out[...] = (q * (MLA_SCALE * LOG2E)).astype(BF16)
    ckv = _dot(h, w_ref[:, 768:1024])
    ckv_out[...] = ckv * lax.rsqrt(jnp.mean(ckv * ckv, axis=-1, keepdims=True) + EPS) * kvn_ref[...]
    prw_out[...] = _dot(h, w_ref[:, 1024:2816])
    krs = _dot(h, w_ref[:, 2816:2944])
    krs = _rope_slot(krs, ck_ref[...], s1k_ref[...], s2k_ref[...])
    kr_out[...] = krs[:, 0:MLA_ROPE]


def _even_in(x, mods, norm_g, w_perm, q_norm, kv_norm, wuq_slot, tabs, rows_per_seq):
    n, d = x.shape
    tm, nb = _row_tiling(n, rows_per_seq, 512)
    ttab = tabs[0].shape[0]
    ntab = ttab // tm
    tab_spec = pl.BlockSpec((tm, LANES), lambda i: (i % ntab, 0))
    row = lambda w: pl.BlockSpec((tm, w), lambda i: (i, 0))
    return pl.pallas_call(
        functools.partial(_even_in_kernel, nb=nb),
        grid=(n // tm,),
        in_specs=[row(d), _mods_spec(tm, nb, rows_per_seq), _const_spec((1, d)),
                  _const_spec(w_perm.shape), _const_spec((1, 768)), _const_spec((1, 256)),
                  _const_spec(wuq_slot.shape)] + [tab_spec] * 6,
        out_specs=[row(1024), row(256), row(MLA_ROPE), row(RW_COLS)],
        out_shape=[jax.ShapeDtypeStruct((n, 1024), BF16), jax.ShapeDtypeStruct((n, 256), F32),
                   jax.ShapeDtypeStruct((n, MLA_ROPE), F32), jax.ShapeDtypeStruct((n, RW_COLS), F32)],
        compiler_params=_cparams(("parallel",)),
    )(x, mods, norm_g.reshape(1, d), w_perm, q_norm.reshape(1, 768), kv_norm.reshape(1, 256),
      wuq_slot, *tabs)


def _rope_tables(pos, tile_to):
    half = MLA_ROPE // 2
    freqs = ROPE_BASE ** (-jnp.arange(half, dtype=F32) / half)
    ang = pos.astype(F32)[:, None] * freqs[None, :]
    cos, sin = jnp.cos(ang), jnp.sin(ang)
    t = pos.shape[0]
    z = lambda w: jnp.zeros((t, w), F32)
    o = lambda w: jnp.ones((t, w), F32)
    cq = jnp.concatenate([o(64), cos, cos, z(32)], axis=1)
    s1q = jnp.concatenate([z(64), -sin, z(48)], axis=1)
    s2q = jnp.concatenate([z(80), sin, z(32)], axis=1)
    ck = jnp.concatenate([cos, cos, z(96)], axis=1)
    s1k = jnp.concatenate([-sin, z(112)], axis=1)
    s2k = jnp.concatenate([z(16), sin, z(96)], axis=1)
    tabs = [cq, s1q, s2q, ck, s1k, s2k]
    if tile_to > t:
        tabs = [jnp.tile(a, (tile_to // t, 1)) for a in tabs]
    return tabs


def _kv_expand_kernel(ckv_ref, kr_ref, wk_ref, sel_ref, wv_ref, one_ref, k_out, v_out):
    c = ckv_ref[...].astype(BF16)
    k = _dot(c, wk_ref[...]) + _dot(kr_ref[...].astype(BF16), sel_ref[...])
    k_out[...] = k.astype(BF16)
    v_out[0] = (_dot_nt(wv_ref[...], c) + one_ref[...]).astype(BF16)


def _kv_expand(ckv, kr, wk_slot, sel, wvt_slot, one_col, batch):
    n = ckv.shape[0]
    t_k = n // batch
    tm = 1024 if t_k % 1024 == 0 else 512
    assert t_k % tm == 0
    per = t_k // tm
    row = lambda w: pl.BlockSpec((tm, w), lambda i: (i, 0))
    return pl.pallas_call(
        _kv_expand_kernel,
        grid=(n // tm,),
        in_specs=[row(256), row(MLA_ROPE), _const_spec(wk_slot.shape), _const_spec(sel.shape),
                  _const_spec(wvt_slot.shape), _const_spec(one_col.shape)],
        out_specs=[row(1024), pl.BlockSpec((1, 1024, tm), lambda i: (i // per, 0, i % per))],
        out_shape=[jax.ShapeDtypeStruct((n, 1024), BF16), jax.ShapeDtypeStruct((batch, 1024, t_k), BF16)],
        compiler_params=_cparams(("parallel",)),
    )(ckv, kr, wk_slot, sel, wvt_slot, one_col)


def _mla_attn_kernel(qi_ref, ki_ref, fl_ref, q_ref, k_ref, vt_ref, o_ref, m_scr, acc_scr,
                     *, tq, tk, q_off):
    p_id = pl.program_id(1)
    flags = fl_ref[p_id]
    first = (flags & 1) != 0
    last = (flags & 2) != 0
    masked = (flags & 4) != 0

    @pl.when(first)
    def _():
        m_scr[...] = jnp.full(m_scr.shape, NEG, F32)
        acc_scr[...] = jnp.zeros(acc_scr.shape, F32)

    qb = min(MLA_QUERY_BLOCK, tq)
    insts = [(h, c0) for h in range(MLA_HEADS) for c0 in range(0, tq, qb)]

    def scores_t(h, c0):
        sl = slice(h * LANES, (h + 1) * LANES)
        return _dot_nt(k_ref[0, :, sl], q_ref[0, c0:c0 + qb, sl])

    def body(use_mask):
        if use_mask:
            q_chunk0 = (q_off + qi_ref[p_id] * tq) // CHUNK
            k_chunk0 = (ki_ref[p_id] * tk) // CHUNK
            kc = k_chunk0 + lax.broadcasted_iota(jnp.int32, (tk, qb), 0) // CHUNK
            qc_local = lax.broadcasted_iota(jnp.int32, (tk, qb), 1) // CHUNK
        ahead = [scores_t(*insts[i]) for i in range(min(MLA_SCORES_AHEAD, len(insts)))]
        for idx, (h, c0) in enumerate(insts):
            s = ahead.pop(0)
            if idx + MLA_SCORES_AHEAD < len(insts):
                ahead.append(scores_t(*insts[idx + MLA_SCORES_AHEAD]))
            if use_mask:
                s = jnp.where(kc <= qc_local + (q_chunk0 + c0 // CHUNK), s, NEG)
            m_prev = m_scr[h:h + 1, c0:c0 + qb]
            m_new = jnp.maximum(m_prev, jnp.max(s, axis=0, keepdims=True))
            m_scr[h:h + 1, c0:c0 + qb] = m_new
            alpha = jnp.exp2(m_prev - m_new)
            p_t = jnp.exp2(s - m_new).astype(BF16)
            vt_h = vt_ref[0, h * LANES:(h + 1) * LANES, :]
            acc_scr[h, :, c0:c0 + qb] = alpha * acc_scr[h, :, c0:c0 + qb] + _dot(vt_h, p_t)

    @pl.when(masked)
    def _():
        body(True)

    @pl.when(jnp.logical_not(masked))
    def _():
        body(False)

    @pl.when(last)
    def _():
        outs = []
        for h in range(MLA_HEADS):
            a = acc_scr[h]
            outs.append(a[0:MLA_V] / a[MLA_V:MLA_V + 1])
        o_ref[0] = jnp.concatenate(outs, axis=0).T.astype(BF16)


MLA_Q_TILE = 1024
MLA_KV_TILE = 512
MLA_QUERY_BLOCK = 256
MLA_SCORES_AHEAD = 3


def _mla_attn(q, k, vt, pairs, tq, tk, q_off):
    b, t_q, _ = q.shape
    qi, ki, fl = pairs
    grid_spec = pltpu.PrefetchScalarGridSpec(
        num_scalar_prefetch=3,
        grid=(b, qi.shape[0]),
        in_specs=[pl.BlockSpec((1, tq, 1024), lambda bb, p, qi, ki, fl: (bb, qi[p], 0)),
                  pl.BlockSpec((1, tk, 1024), lambda bb, p, qi, ki, fl: (bb, ki[p], 0)),
                  pl.BlockSpec((1, 1024, tk), lambda bb, p, qi, ki, fl: (bb, 0, ki[p]))],
        out_specs=pl.BlockSpec((1, tq, 512), lambda bb, p, qi, ki, fl: (bb, qi[p], 0)),
        scratch_shapes=[pltpu.VMEM((MLA_HEADS, tq), F32),
                        pltpu.VMEM((MLA_HEADS, LANES, tq), F32)])
    return pl.pallas_call(
        functools.partial(_mla_attn_kernel, tq=tq, tk=tk, q_off=q_off),
        grid_spec=grid_spec,
        out_shape=jax.ShapeDtypeStruct((b, t_q, 512), BF16),
        compiler_params=_cparams(("parallel", "arbitrary")),
    )(qi, ki, fl, q, k, vt)


def _causal_pairs(nq, ratio):
    qi, ki, fl = [], [], []
    for a in range(nq):
        n_kv = (a + 1) * ratio
        for c in range(n_kv):
            qi.append(a)
            ki.append(c)
            fl.append((1 if c == 0 else 0) | (2 if c == n_kv - 1 else 0) | (4 if c >= a * ratio else 0))
    return tuple(jnp.asarray(np.array(z, np.int32)) for z in (qi, ki, fl))


def _mla_decode_kernel(q_ref, cc_ref, kc_ref, cn_ref, kn_ref, wka_ref, prope_ref, wvp_ref, o_ref,
                       qa_scr, qr_scr, m_scr, l_scr, acc_scr, *, n_cache_blocks, t):
    k_id = pl.program_id(1)

    @pl.when(k_id == 0)
    def _():
        for h in range(MLA_HEADS):
            qs = q_ref[0, :, h * LANES:(h + 1) * LANES]
            qa_scr[h * t:(h + 1) * t, :] = _dot(qs, wka_ref[h]).astype(BF16)
            qr_scr[h * t:(h + 1) * t, :] = _dot(qs, prope_ref[...]).astype(BF16)
        m_scr[...] = jnp.full(m_scr.shape, NEG, F32)
        l_scr[...] = jnp.zeros(l_scr.shape, F32)
        acc_scr[...] = jnp.zeros(acc_scr.shape, F32)

    def update(ckv, kr):
        cb = ckv.astype(BF16)
        s = _dot_nt(qa_scr[...], cb) + _dot_nt(qr_scr[:, 0:MLA_ROPE], kr.astype(BF16))
        m_prev = m_scr[...]
        m_new = jnp.maximum(m_prev, jnp.max(s, axis=-1, keepdims=True))
        alpha = jnp.exp2(m_prev - m_new)
        p = jnp.exp2(s - m_new[:, 0:1])
        l_scr[...] = alpha * l_scr[...] + jnp.sum(p, axis=-1, keepdims=True)
        m_scr[...] = m_new
        acc_scr[...] = jnp.concatenate([alpha, alpha], axis=1) * acc_scr[...] + _dot(p.astype(BF16), cb)

    @pl.when(k_id < n_cache_blocks)
    def _():
        update(cc_ref[0], kc_ref[0])

    @pl.when(k_id == n_cache_blocks)
    def _():
        update(cn_ref[0], kn_ref[0])
        l = l_scr[...]
        o_lat = (acc_scr[...] / jnp.concatenate([l, l], axis=1)).astype(BF16)
        for pr in range(MLA_HEADS // 2):
            oe = o_lat[(2 * pr) * t:(2 * pr + 1) * t]
            oo = o_lat[(2 * pr + 1) * t:(2 * pr + 2) * t]
            o_ref[0, :, pr * LANES:(pr + 1) * LANES] = (
                _dot(oe, wvp_ref[2 * pr]) + _dot(oo, wvp_ref[2 * pr + 1])).astype(BF16)


def _mla_decode(q, ckv_cache, kr_cache, ckv_new, kr_new, wka, prope, wvp):
    b, t, _ = q.shape
    n_past = ckv_cache.shape[1]
    kb = math.gcd(n_past, MLA_KV_TILE)
    ncb = n_past // kb
    rows = MLA_HEADS * t
    per_b = lambda shape: pl.BlockSpec((1,) + shape, lambda bb, k: (bb, 0, 0))
    cache = lambda w: pl.BlockSpec((1, kb, w), lambda bb, k: (bb, jnp.minimum(k, ncb - 1), 0))
    return pl.pallas_call(
        functools.partial(_mla_decode_kernel, n_cache_blocks=ncb, t=t),
        grid=(b, ncb + 1),
        in_specs=[per_b((t, 1024)), cache(256), cache(MLA_ROPE), per_b((t, 256)), per_b((t, MLA_ROPE)),
                  _const_spec(wka.shape), _const_spec(prope.shape), _const_spec(wvp.shape)],
        out_specs=per_b((t, 512)),
        out_shape=jax.ShapeDtypeStruct((b, t, 512), BF16),
        scratch_shapes=[pltpu.VMEM((rows, 256), BF16), pltpu.VMEM((rows, LANES), BF16),
                        pltpu.VMEM((rows, LANES), F32), pltpu.VMEM((rows, LANES), F32),
                        pltpu.VMEM((rows, 256), F32)],
        compiler_params=_cparams(("parallel", "arbitrary")),
    )(q, ckv_cache, kr_cache, ckv_new, kr_new, wka, prope, wvp)


RW_CHUNK_GROUP = 2
RW_TIME_TILE = 256


def _pair_sum(x, ones_bd):
    hi, lo = _split2(x)
    return _dot(hi, ones_bd) + _dot(lo, ones_bd)


def _rwkv_kernel(pr_ref, sh0_ref, s0_ref, mu_ref, w0_ref, wl_ref, a0_ref, al_ref, g2_ref,
                 kk_ref, ka_ref, rk_ref, lnw_ref, lnb_ref, y_ref, st_ref,
                 prev_scr, s_scr, rt_scr, at_scr, bt_scr, kt_scr, bv_scr, k2_scr, v_scr,
                 cum_scr, yc_scr, *, tt):
    t_id = pl.program_id(1)
    n_pairs = RW_HEADS // 2

    @pl.when(t_id == 0)
    def _():
        s_scr[...] = s0_ref[0]
        prev_scr[0:1, :] = sh0_ref[0]

    pr = pr_ref[0]
    row = lax.broadcasted_iota(jnp.int32, (tt, 1), 0)
    prev = jnp.where(row == 0, prev_scr[0:1, :], pltpu.roll(pr, 1, axis=0))
    prev_scr[0:1, :] = pr[tt - 1:tt, :]
    pm = pr + (prev - pr) * mu_ref[...]
    r = pm[:, 0:512]
    k = pm[:, 512:1024]
    v = pm[:, 1024:1536]
    wa = pm[:, 1536:1664]
    g_in = pm[:, 1664:1792]
    z = w0_ref[...] + _dot(jnp.tanh(wa).astype(BF16), wl_ref[...])
    nz = -z
    w = -(jnp.maximum(nz, 0.0) + jnp.log(1.0 + jnp.exp(-jnp.abs(nz)))) - 0.5
    ld = -jnp.exp(w)
    a_sig = jax.nn.sigmoid(a0_ref[...] + _dot(wa.astype(BF16), al_ref[...]))
    g = _dot(jax.nn.sigmoid(g_in).astype(BF16), g2_ref[...])

    li = lax.broadcasted_iota(jnp.int32, (LANES, LANES), 0)
    lj = lax.broadcasted_iota(jnp.int32, (LANES, LANES), 1)
    ones_bd = jnp.where((li // RW_N) == (lj // RW_N), 1.0, 0.0).astype(BF16)

    def head_sum(x):
        return jnp.concatenate([_pair_sum(x[:, p * LANES:(p + 1) * LANES], ones_bd)
                                for p in range(n_pairs)], axis=1)

    kk = k * kk_ref[...]
    kk = kk * lax.rsqrt(jnp.maximum(head_sum(kk * kk), 1e-24))
    k2 = k * (1.0 + (a_sig - 1.0) * ka_ref[...])
    bonus = head_sum(r * k2 * rk_ref[...]) * v

    ti = lax.broadcasted_iota(jnp.int32, (tt, tt), 0)
    tj = lax.broadcasted_iota(jnp.int32, (tt, tt), 1)
    tri = jnp.where(((ti // CHUNK) == (tj // CHUNK)) & (tj <= ti), 1.0, 0.0).astype(BF16)
    l1 = ld.astype(BF16)
    rem = ld - l1.astype(F32)
    l2 = rem.astype(BF16)
    l3 = (rem - l2.astype(F32)).astype(BF16)
    cum = (_dot(tri, l3) + _dot(tri, l2)) + _dot(tri, l1)
    winv = jnp.exp(-cum)
    rt_scr[...] = r * jnp.exp(cum)
    at_scr[...] = -kk * jnp.exp(cum - ld)
    bv = kk * a_sig
    bt_scr[...] = bv * winv
    kt_scr[...] = k2 * winv
    bv_scr[...] = bv
    k2_scr[...] = k2
    v_scr[...] = v
    cum_scr[...] = cum

    lane = lax.broadcasted_iota(jnp.int32, (CHUNK, LANES), 1)
    even = lane < RW_N
    strict = (lj % RW_N) < (li % RW_N)
    incl = (lj % RW_N) <= (li % RW_N)
    eye = jnp.where(li == lj, 1.0, 0.0).astype(F32)
    pairs = range(n_pairs)

    def same_block(m):
        return (li // m) == (lj // m)

    def stack_f32(x):
        return jnp.concatenate([jnp.where(even, x, 0.0), jnp.where(even, 0.0, x)], axis=0)

    def stack(x):
        return stack_f32(x).astype(BF16)

    mm = lambda a, b: _dot(a.astype(BF16), b.astype(BF16))
    rows_of = lambda c: slice(c * CHUNK, (c + 1) * CHUNK)
    lanes_of = lambda p: slice(p * LANES, (p + 1) * LANES)

    def independent_phase(chunks, res):
        insts = [(c, p) for c in chunks for p in pairs]
        load = lambda scr: [scr[rows_of(c), lanes_of(p)] for c, p in insts]
        cum_c = load(cum_scr)
        cum_l = [a[CHUNK - 1:CHUNK, :] for a in cum_c]
        w2 = [jnp.exp(a - b) for a, b in zip(cum_l, cum_c)]
        rs = [stack(a) for a in load(rt_scr)]
        as_ = [stack(a) for a in load(at_scr)]
        bs = [stack(a) for a in load(bt_scr)]
        ks = [stack(a) for a in load(kt_scr)]
        v_f = [stack_f32(a) for a in load(v_scr)]
        vs = [a.astype(BF16) for a in v_f]
        b2s = [stack(a * w) for a, w in zip(load(bv_scr), w2)]
        k2s = [stack(a * w) for a, w in zip(load(k2_scr), w2)]
        yield
        n_m = [jnp.where(strict, _dot_nt(a, b), 0.0) for a, b in zip(as_, bs)]
        mk = [jnp.where(strict, _dot_nt(a, b), 0.0).astype(BF16) for a, b in zip(as_, ks)]
        cb = [jnp.where(incl, _dot_nt(a, b), 0.0).astype(BF16) for a, b in zip(rs, bs)]
        ck = [jnp.where(incl, _dot_nt(a, b), 0.0).astype(BF16) for a, b in zip(rs, ks)]
        yield
        n8 = [jnp.where(same_block(8), n, 0.0) for n in n_m]
        t = [eye + a for a in n8]
        p2 = [mm(a, a) for a in n8]
        mv = [_dot(a, b) for a, b in zip(mk, vs)]
        yield
        t = [a + mm(a, b) for a, b in zip(t, p2)]
        p4 = [mm(a, a) for a in p2]
        cv = [_dot(a, b) for a, b in zip(ck, vs)]
        yield
        t = [a + mm(a, b) for a, b in zip(t, p4)]
        vk = [_dot(a.T.astype(BF16), b) for a, b in zip(v_f, k2s)]
        yield
        for m in (8, 16, 32):
            off = same_block(2 * m) & jnp.logical_not(same_block(m))
            x = [mm(jnp.where(off, n, 0.0), a) for n, a in zip(n_m, t)]
            yield
            t = [a + mm(a, b) for a, b in zip(t, x)]
            yield
        for i, key in enumerate(insts):
            res[key] = dict(as_=as_[i], rs=rs[i], b2s=b2s[i], cb=cb[i], mv=mv[i], cv=cv[i], vk=vk[i],
                            tinv=t[i].astype(BF16), wl=jnp.exp(cum_l[i]))

    def dependent_phase(chunks, res):
        for c in chunks:
            rc = [res[(c, p)] for p in pairs]
            s = [s_scr[p] for p in pairs]
            sb = [a.astype(BF16) for a in s]
            x = [_dot_nt(r_['as_'], b) + r_['mv'] for r_, b in zip(rc, sb)]
            rs_s = [_dot_nt(r_['rs'], b) + r_['cv'] for r_, b in zip(rc, sb)]
            yield
            u = [_dot(r_['tinv'], a.astype(BF16)) for r_, a in zip(rc, x)]
            yield
            ys = [a + _dot(r_['cb'], b.astype(BF16)) for a, r_, b in zip(rs_s, rc, u)]
            for p in pairs:
                yc_scr[rows_of(c), lanes_of(p)] = ys[p][0:CHUNK] + ys[p][CHUNK:2 * CHUNK]
                s_scr[p] = s[p] * rc[p]['wl'] + _dot(u[p].T.astype(BF16), rc[p]['b2s']) + rc[p]['vk']
            yield

    def emit(*gens):
        gens = list(gens)
        while gens:
            for gen in list(gens):
                try:
                    next(gen)
                except StopIteration:
                    gens.remove(gen)

    n_chunks = tt // CHUNK
    groups = [list(range(c0, min(c0 + RW_CHUNK_GROUP, n_chunks))) for c0 in range(0, n_chunks, RW_CHUNK_GROUP)]
    res = {}
    emit(independent_phase(groups[0], res))
    for gi in range(1, len(groups)):
        emit(independent_phase(groups[gi], res), dependent_phase(groups[gi - 1], res))
    emit(dependent_phase(groups[-1], res))

    y = yc_scr[...]
    mean = head_sum(y) * (1.0 / RW_N)
    dlt = y - mean
    var = head_sum(dlt * dlt) * (1.0 / RW_N)
    yn = dlt * lax.rsqrt(var + RW_GN_EPS) * lnw_ref[...] + lnb_ref[...]
    y_ref[0] = ((yn + bonus) * g).astype(BF16)

    @pl.when(t_id == pl.num_programs(1) - 1)
    def _():
        st_ref[0] = s_scr[...]


def _rwkv(prw, sh0, s0_bd, wts):
    b, t, _ = prw.shape
    tt = min(RW_TIME_TILE, t)
    assert t % tt == 0 and tt % CHUNK == 0
    c512 = _const_spec((1, RW_C))
    scr = lambda: pltpu.VMEM((tt, RW_C), F32)
    return pl.pallas_call(
        functools.partial(_rwkv_kernel, tt=tt),
        grid=(b, t // tt),
        in_specs=[pl.BlockSpec((1, tt, RW_COLS), lambda bb, i: (bb, i, 0)),
                  pl.BlockSpec((1, 1, RW_COLS), lambda bb, i: (bb, 0, 0)),
                  pl.BlockSpec((1, 4, LANES, LANES), lambda bb, i: (bb, 0, 0, 0)),
                  _const_spec((1, RW_COLS)), c512, _const_spec((LANES, RW_C)), c512,
                  _const_spec((LANES, RW_C)), _const_spec((LANES, RW_C)),
                  c512, c512, c512, c512, c512],
        out_specs=[pl.BlockSpec((1, tt, RW_C), lambda bb, i: (bb, i, 0)),
                   pl.BlockSpec((1, 4, LANES, LANES), lambda bb, i: (bb, 0, 0, 0))],
        out_shape=[jax.ShapeDtypeStruct((b, t, RW_C), BF16),
                   jax.ShapeDtypeStruct((b, 4, LANES, LANES), F32)],
        scratch_shapes=[pltpu.VMEM((8, RW_COLS), F32), pltpu.VMEM((4, LANES, LANES), F32)]
                       + [scr() for _ in range(9)],
        compiler_params=_cparams(("parallel", "arbitrary")),
    )(prw, sh0, s0_bd, *wts)


def _state_to_bd(s):
    b = s.shape[0]
    s = s.reshape(b, 4, 2, RW_N, RW_N)
    z = jnp.zeros_like(s[:, :, 0])
    top = jnp.concatenate([s[:, :, 0], z], axis=-1)
    bot = jnp.concatenate([z, s[:, :, 1]], axis=-1)
    return jnp.concatenate([top, bot], axis=-2)


def _state_from_bd(s):
    b = s.shape[0]
    return jnp.stack([s[:, :, :RW_N, :RW_N], s[:, :, RW_N:, RW_N:]], axis=2).reshape(b, RW_HEADS, RW_N, RW_N)


_ODD_SEGS = ([(1024 + g * 64, 1088 + g * 64) for g in range(SW_KV_HEADS) for _ in range(2)]
             + [(1280 + g * 64, 1344 + g * 64) for g in range(SW_KV_HEADS) for _ in range(2)])


def _odd_in_kernel(x_ref, mods_ref, ng_ref, w_ref, b_ref, q_out, k_out, v_out, *, nb):
    h = _norm_mod(x_ref[...], ng_ref[...], mods_ref[:, 3, :], mods_ref[:, 4, :], nb).astype(BF16)
    q = _dot(h, w_ref[:, 0:1024]) + b_ref[:, 0:1024]
    q_out[...] = (q * (SW_HD ** -0.5 * LOG2E)).astype(BF16)
    k_out[...] = _dot(h, w_ref[:, 1024:1536]) + b_ref[:, 1024:1536]
    v_out[...] = _dot(h, w_ref[:, 1536:2048]) + b_ref[:, 1536:2048]


def _odd_in(x, mods, norm_g, w_perm, b_perm, rows_per_seq):
    n, d = x.shape
    tm, nb = _row_tiling(n, rows_per_seq, 512)
    row = lambda w: pl.BlockSpec((tm, w), lambda i: (i, 0))
    return pl.pallas_call(
        functools.partial(_odd_in_kernel, nb=nb),
        grid=(n // tm,),
        in_specs=[row(d), _mods_spec(tm, nb, rows_per_seq), _const_spec((1, d)),
                  _const_spec(w_perm.shape), _const_spec((1, 2048))],
        out_specs=[row(1024), row(512), row(512)],
        out_shape=[jax.ShapeDtypeStruct((n, 1024), BF16), jax.ShapeDtypeStruct((n, 512), F32),
                   jax.ShapeDtypeStruct((n, 512), F32)],
        compiler_params=_cparams(("parallel",)),
    )(x, mods, norm_g.reshape(1, d), w_perm, b_perm.reshape(1, 2048))


def _swa_kernel(sinks_ref, q_ref, kp_ref, kc_ref, vp_ref, vc_ref, o_ref, *, tq, mask_first_prev):
    nk = WINDOW + tq
    qi = lax.broadcasted_iota(jnp.int32, (tq, nk), 0)
    kj = lax.broadcasted_iota(jnp.int32, (tq, nk), 1)
    kc = kj // CHUNK - WINDOW // CHUNK
    qc = qi // CHUNK
    vis = (kc <= qc) & (kc >= qc - WINDOW // CHUNK)
    if mask_first_prev:
        vis = vis & ((kj >= WINDOW) | (pl.program_id(1) > 0))
    ndist = jnp.where(vis, -jnp.abs(qi + WINDOW - kj).astype(F32), NEG)
    keys = jnp.concatenate([kp_ref[0], kc_ref[0]], axis=0).astype(BF16)
    vals = jnp.concatenate([vp_ref[0], vc_ref[0]], axis=0)
    klane = lax.broadcasted_iota(jnp.int32, (nk, LANES), 1)
    v_slots = [jnp.where(klane < SW_HD, vals[:, g * LANES:(g + 1) * LANES], 1.0).astype(BF16)
               for g in range(SW_KV_HEADS)]
    lane = lax.broadcasted_iota(jnp.int32, (tq, LANES), 1)
    low = lane < SW_HD
    sb = min(SWA_SUB_BLOCK, tq)

    def scores(h):
        qp = q_ref[0, :, (h // 2) * LANES:(h // 2 + 1) * LANES]
        qh = jnp.where(low if h % 2 == 0 else jnp.logical_not(low), qp, jnp.zeros_like(qp))
        return _dot_nt(qh, keys[:, (h // SW_GROUP) * LANES:(h // SW_GROUP + 1) * LANES])

    outs = []
    s_next = scores(0)
    for h in range(SW_HEADS):
        s = s_next
        if h + 1 < SW_HEADS:
            s_next = scores(h + 1)
        slope = (2.0 ** (-8.0 * (h + 1) / SW_HEADS)) * LOG2E
        sk = sinks_ref[h] * LOG2E
        es, ms = [], []
        for r1 in range(0, tq, sb):
            z = s[r1:r1 + sb] + slope * ndist[r1:r1 + sb]
            m = jnp.maximum(jnp.broadcast_to(jnp.max(z, axis=-1, keepdims=True), (sb, LANES)), sk)
            m_wide = jnp.concatenate([m] * (nk // LANES), axis=1) if nk % LANES == 0 else m[:, 0:1]
            es.append(jnp.exp2(z - m_wide).astype(BF16))
            ms.append(m)
        e = jnp.concatenate(es, axis=0) if len(es) > 1 else es[0]
        m = jnp.concatenate(ms, axis=0) if len(ms) > 1 else ms[0]
        pv = _dot(e, v_slots[h // SW_GROUP])
        outs.append(pv / (pltpu.roll(pv, SW_HD, axis=1) + jnp.exp2(sk - m)))
    for pr in range(SW_HEADS // 2):
        o_ref[0, :, pr * LANES:(pr + 1) * LANES] = jnp.where(
            low, outs[2 * pr], pltpu.roll(outs[2 * pr + 1], SW_HD, axis=1)).astype(BF16)


SWA_SUB_BLOCK = 32


def _swa(q, k_prev, k_cur, v_prev, v_cur, sinks, tq, same_array):
    b, t, _ = q.shape
    nt = t // tq
    per = tq // WINDOW
    if same_array:
        prev_map = lambda bb, i: (bb, jnp.maximum(i * per - 1, 0), 0)
    else:
        prev_map = lambda bb, i: (bb, 0, 0)
    cur = lambda w: pl.BlockSpec((1, tq, w), lambda bb, i: (bb, i, 0))
    prev = pl.BlockSpec((1, WINDOW, 512), prev_map)
    return pl.pallas_call(
        functools.partial(_swa_kernel, tq=tq, mask_first_prev=same_array),
        grid=(b, nt),
        in_specs=[pl.BlockSpec(memory_space=pltpu.SMEM), cur(1024), prev, cur(512), prev, cur(512)],
        out_specs=cur(1024),
        out_shape=jax.ShapeDtypeStruct((b, t, 1024), BF16),
        compiler_params=_cparams(("parallel", "parallel")),
    )(sinks, q, k_prev, k_cur, v_prev, v_cur)


def _undup(a):
    return a.reshape(a.shape[:-1] + (SW_KV_HEADS, 2, SW_HD))[..., 0, :]


def _dup(a):
    return jnp.concatenate([a, a], axis=-1).reshape(a.shape[:-2] + (512,))


def _prep_weights(p):
    depth = p['w_ada'].shape[0]
    n_even, n_odd = (depth + 1) // 2, depth // 2
    w = {}
    w['ffn_in'] = p['ffn_w_in'].astype(BF16)
    w['ffn_out'] = p['ffn_w_out'].astype(BF16)
    wi = p['even_w_in'].astype(BF16)
    w['even_in'] = jnp.concatenate(
        [wi[:, :, 0:1024]] + [wi[:, :, _RW_OFF + a:_RW_OFF + b] for a, b in _RW_SEGS]
        + [wi[:, :, 1024:1056], jnp.zeros((n_even, 1024, 96), BF16)], axis=2)
    w['wuq'] = jnp.pad(p['mla_w_uq'], ((0, 0), (0, 0), (0, 0), (0, 32))).reshape(n_even, 768, 1024).astype(BF16)
    w['wk'] = jnp.pad(p['mla_w_ukv'][..., :MLA_NOPE], ((0, 0), (0, 0), (0, 0), (0, 64))
                      ).reshape(n_even, 256, 1024).astype(BF16)
    w['wv'] = jnp.swapaxes(jnp.pad(p['mla_w_ukv'][..., MLA_NOPE:], ((0, 0), (0, 0), (0, 0), (0, 64))
                                   ).reshape(n_even, 256, 1024), 1, 2).astype(BF16)
    sel = np.zeros((MLA_ROPE, 1024), np.float32)
    one = np.zeros((1024, 1), np.float32)
    for h in range(MLA_HEADS):
        sel[np.arange(MLA_ROPE), h * LANES + MLA_NOPE + np.arange(MLA_ROPE)] = 1.0
        one[h * LANES + MLA_V, 0] = 1.0
    w['sel'] = jnp.asarray(sel).astype(BF16)
    w['one'] = jnp.asarray(one)
    wk_t = jnp.transpose(p['mla_w_ukv'][..., :MLA_NOPE], (0, 2, 3, 1))
    w['wka'] = jnp.pad(wk_t, ((0, 0), (0, 0), (0, 64), (0, 0))).astype(BF16)
    prope = np.zeros((LANES, LANES), np.float32)
    prope[MLA_NOPE + np.arange(MLA_ROPE), np.arange(MLA_ROPE)] = 1.0
    w['prope'] = jnp.asarray(prope).astype(BF16)
    wv_h = jnp.transpose(p['mla_w_ukv'][..., MLA_NOPE:], (0, 2, 1, 3))
    wv_even = jnp.pad(wv_h[:, 0::2], ((0, 0), (0, 0), (0, 0), (0, 64)))
    wv_odd = jnp.pad(wv_h[:, 1::2], ((0, 0), (0, 0), (0, 0), (64, 0)))
    w['wvp'] = jnp.stack([wv_even, wv_odd], axis=2).reshape(n_even, MLA_HEADS, 256, LANES).astype(BF16)
    z64 = jnp.zeros((n_even, 64, RW_C), F32)
    w['wl'] = jnp.concatenate([p['rw_w2'], z64], axis=1).astype(BF16)
    w['al'] = jnp.concatenate([z64, p['rw_a2']], axis=1).astype(BF16)
    w['g2'] = p['rw_g2'].astype(BF16)
    w['mu'] = _rw_permute(p['rw_mu'])
    w['even_out'] = p['even_w_out'].astype(BF16)
    wo = p['odd_w_qkv'].astype(BF16)
    w['odd_in'] = jnp.concatenate([wo[:, :, 0:1024]] + [wo[:, :, a:b] for a, b in _ODD_SEGS], axis=2)
    bo = p['odd_b_qkv']
    w['odd_b'] = jnp.concatenate([bo[:, 0:1024]] + [bo[:, a:b] for a, b in _ODD_SEGS], axis=1)
    w['odd_out'] = p['odd_w_out'].astype(BF16)
    return w


def _trunk(x3, mods_all, start, past, p, w):
    b, t, d = x3.shape
    n = b * t
    depth = mods_all.shape[0]
    x = x3.reshape(n, d)
    rows = t
    tm_even, _ = _row_tiling(n, rows, 512)
    pos = start + jnp.arange(t)
    tabs = _rope_tables(pos, tm_even)
    even_states, odd_states = [], []
    for i in range(depth):
        mods = mods_all[i]
        j = i // 2
        x = _ffn(x, mods, p['norm_g'][i, 0], w['ffn_in'][i, 0], w['ffn_out'][i, 0], 0, rows)
        if i % 2 == 0:
            q, ckv, kr, prw = _even_in(x, mods, p['norm_g'][i, 1], w['even_in'][j], p['mla_q_norm'][j],
                                       p['mla_kv_norm'][j], w['wuq'][j], tabs, rows)
            if past is None:
                kx, vtx = _kv_expand(ckv, kr, w['wk'][j], w['sel'], w['wv'][j], w['one'], b)
                tk = min(MLA_KV_TILE, t)
                tq = min(MLA_Q_TILE, t)
                att = _mla_attn(q.reshape(b, t, 1024), kx.reshape(b, t, 1024), vtx,
                                _causal_pairs(t // tq, tq // tk), tq, tk, 0)
                s0 = jnp.zeros((b, RW_HEADS, RW_N, RW_N), F32)
                sh0 = jnp.zeros((b, RW_COLS), F32)
            else:
                ckv_past, kr_past, s0, sh0 = past[0][j], past[1][j], past[2][j], past[3][j]
                att = _mla_decode(q.reshape(b, t, 1024), ckv_past, kr_past, ckv.reshape(b, t, 256),
                                  kr.reshape(b, t, MLA_ROPE), w['wka'][j], w['prope'], w['wvp'][j])
            rw_wts = (w['mu'][j].reshape(1, RW_COLS), p['rw_w0'][j].reshape(1, RW_C), w['wl'][j],
                      p['rw_a0'][j].reshape(1, RW_C), w['al'][j], w['g2'][j],
                      p['rw_k_k'][j].reshape(1, RW_C), p['rw_k_a'][j].reshape(1, RW_C),
                      p['rw_r_k'][j].reshape(1, RW_C), p['rw_ln_w'][j].reshape(1, RW_C),
                      p['rw_ln_b'][j].reshape(1, RW_C))
            prw3 = prw.reshape(b, t, RW_COLS)
            y_rw, s_bd = _rwkv(prw3, _rw_permute(sh0).reshape(b, 1, RW_COLS), _state_to_bd(s0), rw_wts)
            mix = ((att.reshape(n, 512), w['even_out'][j][:512]), (y_rw.reshape(n, RW_C), w['even_out'][j][512:]))
            even_states.append((ckv.reshape(b, t, 256), kr.reshape(b, t, MLA_ROPE), _state_from_bd(s_bd),
                                _rw_unpermute(prw3[:, t - 1, :])))
        else:
            q, kd, vd = _odd_in(x, mods, p['norm_g'][i, 1], w['odd_in'][j], w['odd_b'][j], rows)
            q3, kd3, vd3 = q.reshape(b, t, 1024), kd.reshape(b, t, 512), vd.reshape(b, t, 512)
            if past is None:
                tq = min(256, t)
                o = _swa(q3, kd3, kd3, vd3, vd3, p['swa_sinks'][j], tq, True)
                keep = min(WINDOW, t)
                k_new, v_new = _undup(kd3[:, t - keep:]), _undup(vd3[:, t - keep:])
            else:
                k_past, v_past = past[4][j], past[5][j]
                o = _swa(q3, _dup(k_past), kd3, _dup(v_past), vd3, p['swa_sinks'][j], t, False)
                k_new = jnp.concatenate([k_past, _undup(kd3)], axis=1)[:, t:]
                v_new = jnp.concatenate([v_past, _undup(vd3)], axis=1)[:, t:]
            mix = ((o.reshape(n, 1024), w['odd_out'][j]),)
            odd_states.append((k_new, v_new))
        fg = p['final_norm_g'] if i == depth - 1 else None
        x = _ffn(x, mods, p['norm_g'][i, 2], w['ffn_in'][i, 1], w['ffn_out'][i, 1], 2, rows, final_g=fg, mix=mix)
    es = [jnp.stack([st[k] for st in even_states]) for k in range(4)]
    os_ = [jnp.stack([st[k] for st in odd_states]) for k in range(2)]
    return x.reshape(b, t, d), es + os_


def kernel(x_prompt, x_sample, cache_mla_ckv, cache_mla_krope, state_rwkv, state_rwkv_shift, cache_swa_k, cache_swa_v, c_prompt, c_sample, w_ada, b_ada, norm_g, ffn_w_in, ffn_w_out, even_w_in, even_w_out, mla_q_norm, mla_kv_norm, mla_w_uq, mla_w_ukv, rw_mu, rw_w0, rw_w2, rw_a0, rw_a2, rw_g2, rw_k_k, rw_k_a, rw_r_k, rw_ln_w, rw_ln_b, odd_w_qkv, odd_b_qkv, odd_w_out, swa_sinks, final_norm_g):
    p = dict(w_ada=w_ada, b_ada=b_ada, norm_g=norm_g, ffn_w_in=ffn_w_in, ffn_w_out=ffn_w_out,
             even_w_in=even_w_in, even_w_out=even_w_out, mla_q_norm=mla_q_norm, mla_kv_norm=mla_kv_norm,
             mla_w_uq=mla_w_uq, mla_w_ukv=mla_w_ukv, rw_mu=rw_mu, rw_w0=rw_w0, rw_w2=rw_w2, rw_a0=rw_a0,
             rw_a2=rw_a2, rw_g2=rw_g2, rw_k_k=rw_k_k, rw_k_a=rw_k_a, rw_r_k=rw_r_k, rw_ln_w=rw_ln_w,
             rw_ln_b=rw_ln_b, odd_w_qkv=odd_w_qkv, odd_b_qkv=odd_b_qkv, odd_w_out=odd_w_out,
             swa_sinks=swa_sinks, final_norm_g=final_norm_g)
    w = _prep_weights(p)
    depth = w_ada.shape[0]
    bp, bs = c_prompt.shape[0], c_sample.shape[0]
    d = c_prompt.shape[1]
    b_pad = -(-(bp + bs) // 8) * 8
    c_all = jnp.concatenate([c_prompt, c_sample, jnp.zeros((b_pad - bp - bs, d), F32)], axis=0)
    mods = _ada(c_all, w_ada, b_ada).reshape(depth, b_pad, 3 * N_SUB, d)
    y_prompt, sp = _trunk(x_prompt, mods[:, :bp], 0, None, p, w)
    past = (cache_mla_ckv, cache_mla_krope, state_rwkv, state_rwkv_shift, cache_swa_k, cache_swa_v)
    y_sample, ss = _trunk(x_sample, mods[:, bp:bp + bs], cache_mla_ckv.shape[2], past, p, w)
    return (y_prompt, y_sample, sp[0], sp[1], sp[2], sp[3], sp[4], sp[5],
            ss[0], ss[1], ss[2], ss[3], ss[4], ss[5])
```

```python
import functools
import math

import jax
import jax.numpy as jnp
import numpy as np
from jax import lax
from jax.experimental import pallas as pl
from jax.experimental.pallas import tpu as pltpu

F32 = jnp.float32
BF16 = jnp.bfloat16

CHUNK = 64
EPS = 1e-6
NEG = -1e30
N_SUB = 3
MLA_HEADS = 8
MLA_NOPE = 64
MLA_ROPE = 32
MLA_V = 64
MLA_Q_LORA = 768
MLA_KV_LORA = 256
MLA_SCALE = (MLA_NOPE + MLA_ROPE) ** -0.5
LOG2E = math.log2(math.e)
ROPE_BASE = 10000.0
RW_HEADS = 8
RW_N = 64
RW_C = RW_HEADS * RW_N
RW_GN_EPS = 64e-5
RW_COLS = 3 * RW_C + 64 + 64 + 128
SW_HEADS = 16
SW_KV_HEADS = 4
SW_GROUP = 4
SW_HD = 64
WINDOW = 128

LANES = 128
VMEM_LIMIT = 56 * 1024 * 1024


def _cparams(sem):
    return pltpu.CompilerParams(dimension_semantics=sem, vmem_limit_bytes=VMEM_LIMIT)


def _const_spec(shape):
    nd = len(shape)
    return pl.BlockSpec(shape, lambda *_: (0,) * nd, pipeline_mode=pl.Buffered(1))


def _dot(a, b):
    return jnp.dot(a, b, preferred_element_type=F32)


def _dot_nt(a, b):
    return lax.dot_general(a, b, (((1,), (1,)), ((), ())), preferred_element_type=F32)


def _split2(x):
    hi = x.astype(BF16)
    lo = (x - hi.astype(F32)).astype(BF16)
    return hi, lo


def _norm_mod(x, g, sh, sc, nb):
    y = x * lax.rsqrt(jnp.mean(x * x, axis=-1, keepdims=True) + EPS) * g
    if nb == 1:
        return y * (1.0 + sc) + sh
    tm, d = x.shape
    y3 = y.reshape(nb, tm // nb, d)
    return (y3 * (1.0 + sc[:, None, :]) + sh[:, None, :]).reshape(tm, d)


def _gate_rows(g, y, nb):
    if nb == 1:
        return g * y
    tm, d = y.shape
    return (y.reshape(nb, tm // nb, d) * g[:, None, :]).reshape(tm, d)


def _row_tiling(n_rows, rows_per_seq, pref):
    if rows_per_seq >= pref:
        assert rows_per_seq % pref == 0
        return pref, 1
    tm = min(pref, n_rows)
    assert tm % rows_per_seq == 0 and n_rows % tm == 0
    return tm, tm // rows_per_seq


def _mods_spec(tm, nb, rows_per_seq):
    if nb == 1:
        tiles_per_seq = rows_per_seq // tm
        return pl.BlockSpec((1, 3 * N_SUB, 1024), lambda i: (i // tiles_per_seq, 0, 0))
    return pl.BlockSpec((nb, 3 * N_SUB, 1024), lambda i: (i, 0, 0))


def _ada_kernel(c_ref, w_ref, b_ref, o_ref):
    c = c_ref[...]
    cs = (c * jax.nn.sigmoid(c)).astype(BF16)
    o_ref[0] = _dot(cs, w_ref[0].astype(BF16)) + b_ref[0]


def _ada(c_all, w_ada, b_ada):
    depth, d, n = w_ada.shape
    bp = c_all.shape[0]
    tn = n // 4
    return pl.pallas_call(
        _ada_kernel,
        grid=(depth, n // tn),
        in_specs=[pl.BlockSpec((bp, d), lambda l, j: (0, 0)),
                  pl.BlockSpec((1, d, tn), lambda l, j: (l, 0, j)),
                  pl.BlockSpec((1, 1, tn), lambda l, j: (l, 0, j))],
        out_specs=pl.BlockSpec((1, bp, tn), lambda l, j: (l, 0, j)),
        out_shape=jax.ShapeDtypeStruct((depth, bp, n), F32),
        compiler_params=_cparams(("parallel", "parallel")),
    )(c_all, w_ada, b_ada.reshape(depth, 1, n))


FF_CHUNK = 256


def _ffn_kernel(x_ref, mods_ref, ng_ref, win_ref, wout_ref, *rest, sub, nb, d_ff, final, n_mix):
    mix_a, mix_w, rest = rest[:n_mix], rest[n_mix:2 * n_mix], rest[2 * n_mix:]
    if final:
        fg_ref, o_ref, a_scr = rest
    else:
        o_ref, a_scr = rest
    x = x_ref[...]
    if n_mix:
        ym = _dot(mix_a[0][...], mix_w[0][...])
        for a, w in zip(mix_a[1:], mix_w[1:]):
            ym = ym + _dot(a[...], w[...])
        x = x + _gate_rows(mods_ref[:, 5, :], ym, nb)
    sh = mods_ref[:, 3 * sub, :]
    sc = mods_ref[:, 3 * sub + 1, :]
    gt = mods_ref[:, 3 * sub + 2, :]
    h = _norm_mod(x, ng_ref[...], sh, sc, nb).astype(BF16)
    for c in range(d_ff // FF_CHUNK):
        lo = c * FF_CHUNK
        g = _dot(h, win_ref[:, lo:lo + FF_CHUNK])
        u = _dot(h, win_ref[:, d_ff + lo:d_ff + lo + FF_CHUNK])
        a_scr[:, lo:lo + FF_CHUNK] = (g * jax.nn.sigmoid(g) * u).astype(BF16)
    y = _dot(a_scr[...], wout_ref[...])
    out = x + _gate_rows(0.5 * gt, y, nb)
    if final:
        out = out * lax.rsqrt(jnp.mean(out * out, axis=-1, keepdims=True) + EPS) * fg_ref[...]
    o_ref[...] = out


def _ffn(x, mods, norm_g, w_in, w_out, sub, rows_per_seq, final_g=None, mix=()):
    n, d = x.shape
    d_ff = w_out.shape[0]
    tm, nb = _row_tiling(n, rows_per_seq, 512)
    final = final_g is not None
    in_specs = [pl.BlockSpec((tm, d), lambda i: (i, 0)),
                _mods_spec(tm, nb, rows_per_seq),
                _const_spec((1, d)), _const_spec(w_in.shape), _const_spec(w_out.shape)]
    args = [x, mods, norm_g.reshape(1, d), w_in, w_out]
    in_specs += [pl.BlockSpec((tm, a.shape[1]), lambda i: (i, 0)) for a, _ in mix]
    in_specs += [_const_spec(wm.shape) for _, wm in mix]
    args += [a for a, _ in mix] + [wm for _, wm in mix]
    if final:
        in_specs.append(_const_spec((1, d)))
        args.append(final_g.reshape(1, d))
    return pl.pallas_call(
        functools.partial(_ffn_kernel, sub=sub, nb=nb, d_ff=d_ff, final=final, n_mix=len(mix)),
        grid=(n // tm,),
        in_specs=in_specs,
        out_specs=pl.BlockSpec((tm, d), lambda i: (i, 0)),
        out_shape=jax.ShapeDtypeStruct((n, d), F32),
        scratch_shapes=[pltpu.VMEM((tm, d_ff), BF16)],
        compiler_params=_cparams(("parallel",)),
    )(*args)


EVEN_W = 768 + 256 + 1792 + 128
_RW_OFF = 1056
_RW_SEGS = [(0, 512), (576, 1088), (1088, 1600), (512, 576), (1600, 1664), (1664, 1792)]
_RW_INV_SEGS = [(0, 512), (1536, 1600), (512, 1024), (1024, 1536), (1600, 1664), (1664, 1792)]


def _rw_permute(a):
    return jnp.concatenate([a[..., s:e] for s, e in _RW_SEGS], axis=-1)


def _rw_unpermute(a):
    return jnp.concatenate([a[..., s:e] for s, e in _RW_INV_SEGS], axis=-1)


def _rope_slot(v, c, s1, s2):
    w = v.shape[-1]
    return v * c + pltpu.roll(v, w - 16, axis=1) * s1 + pltpu.roll(v, 16, axis=1) * s2


def _even_in_kernel(x_ref, mods_ref, ng_ref, w_ref, qn_ref, kvn_ref, wuq_ref,
                    cq_ref, s1q_ref, s2q_ref, ck_ref, s1k_ref, s2k_ref,
                    q_out, ckv_out, kr_out, prw_out, *, nb):
    x = x_ref[...]
    h = _norm_mod(x, ng_ref[...], mods_ref[:, 3, :], mods_ref[:, 4, :], nb).astype(BF16)
    cq = _dot(h, w_ref[:, 0:768])
    cqn = (cq * lax.rsqrt(jnp.mean(cq * cq, axis=-1, keepdims=True) + EPS) * qn_ref[...]).astype(BF16)
    q = _dot(cqn, wuq_ref[...])
    rep = lambda t: jnp.concatenate([t] * MLA_HEADS, axis=1)
    q = _rope_slot(q, rep(cq_ref[...]), rep(s1q_ref[...]), rep(s2q_ref[...]))
    q_out[...] = (q * (MLA_SCALE * LOG2E)).astype(BF16)
    ckv = _dot(h, w_ref[:, 768:1024])
    ckv_out[...] = ckv * lax.rsqrt(jnp.mean(ckv * ckv, axis=-1, keepdims=True) + EPS) * kvn_ref[...]
    prw_out[...] = _dot(h, w_ref[:, 1024:2816])
    krs = _dot(h, w_ref[:, 2816:2944])
    krs = _rope_slot(krs, ck_ref[...], s1k_ref[...], s2k_ref[...])
    kr_out[...] = krs[:, 0:MLA_ROPE]


def _even_in(x, mods, norm_g, w_perm, q_norm, kv_norm, wuq_slot, tabs, rows_per_seq):
    n, d = x.shape
    tm, nb = _row_tiling(n, rows_per_seq, 512)
    ttab = tabs[0].shape[0]
    ntab = ttab // tm
    tab_spec = pl.BlockSpec((tm, LANES), lambda i: (i % ntab, 0))
    row = lambda w: pl.BlockSpec((tm, w), lambda i: (i, 0))
    return pl.pallas_call(
        functools.partial(_even_in_kernel, nb=nb),
        grid=(n // tm,),
        in_specs=[row(d), _mods_spec(tm, nb, rows_per_seq), _const_spec((1, d)),
                  _const_spec(w_perm.shape), _const_spec((1, 768)), _const_spec((1, 256)),
                  _const_spec(wuq_slot.shape)] + [tab_spec] * 6,
        out_specs=[row(1024), row(256), row(MLA_ROPE), row(RW_COLS)],
        out_shape=[jax.ShapeDtypeStruct((n, 1024), BF16), jax.ShapeDtypeStruct((n, 256), F32),
                   jax.ShapeDtypeStruct((n, MLA_ROPE), F32), jax.ShapeDtypeStruct((n, RW_COLS), F32)],
        compiler_params=_cparams(("parallel",)),
    )(x, mods, norm_g.reshape(1, d), w_perm, q_norm.reshape(1, 768), kv_norm.reshape(1, 256),
      wuq_slot, *tabs)


def _rope_tables(pos, tile_to):
    half = MLA_ROPE // 2
    freqs = ROPE_BASE ** (-jnp.arange(half, dtype=F32) / half)
    ang = pos.astype(F32)[:, None] * freqs[None, :]
    cos, sin = jnp.cos(ang), jnp.sin(ang)
    t = pos.shape[0]
    z = lambda w: jnp.zeros((t, w), F32)
    o = lambda w: jnp.ones((t, w), F32)
    cq = jnp.concatenate([o(64), cos, cos, z(32)], axis=1)
    s1q = jnp.concatenate([z(64), -sin, z(48)], axis=1)
    s2q = jnp.concatenate([z(80), sin, z(32)], axis=1)
    ck = jnp.concatenate([cos, cos, z(96)], axis=1)
    s1k = jnp.concatenate([-sin, z(112)], axis=1)
    s2k = jnp.concatenate([z(16), sin, z(96)], axis=1)
    tabs = [cq, s1q, s2q, ck, s1k, s2k]
    if tile_to > t:
        tabs = [jnp.tile(a, (tile_to // t, 1)) for a in tabs]
    return tabs


def _kv_expand_kernel(ckv_ref, kr_ref, wk_ref, sel_ref, wv_ref, one_ref, k_out, v_out):
    c = ckv_ref[...].astype(BF16)
    k = _dot(c, wk_ref[...]) + _dot(kr_ref[...].astype(BF16), sel_ref[...])
    k_out[...] = k.astype(BF16)
    v_out[0] = (_dot_nt(wv_ref[...], c) + one_ref[...]).astype(BF16)


def _kv_expand(ckv, kr, wk_slot, sel, wvt_slot, one_col, batch):
    n = ckv.shape[0]
    t_k = n // batch
    tm = 1024 if t_k % 1024 == 0 else 512
    assert t_k % tm == 0
    per = t_k // tm
    row = lambda w: pl.BlockSpec((tm, w), lambda i: (i, 0))
    return pl.pallas_call(
        _kv_expand_kernel,
        grid=(n // tm,),
        in_specs=[row(256), row(MLA_ROPE), _const_spec(wk_slot.shape), _const_spec(sel.shape),
                  _const_spec(wvt_slot.shape), _const_spec(one_col.shape)],
        out_specs=[row(1024), pl.BlockSpec((1, 1024, tm), lambda i: (i // per, 0, i % per))],
        out_shape=[jax.ShapeDtypeStruct((n, 1024), BF16), jax.ShapeDtypeStruct((batch, 1024, t_k), BF16)],
        compiler_params=_cparams(("parallel",)),
    )(ckv, kr, wk_slot, sel, wvt_slot, one_col)


def _mla_attn_kernel(qi_ref, ki_ref, fl_ref, q_ref, k_ref, vt_ref, o_ref, m_scr, acc_scr,
                     *, tq, tk, q_off):
    p_id = pl.program_id(1)
    flags = fl_ref[p_id]
    first = (flags & 1) != 0
    last = (flags & 2) != 0
    masked = (flags & 4) != 0

    @pl.when(first)
    def _():
        m_scr[...] = jnp.full(m_scr.shape, NEG, F32)
        acc_scr[...] = jnp.zeros(acc_scr.shape, F32)

    qb = min(MLA_QUERY_BLOCK, tq)
    insts = [(h, c0) for h in range(MLA_HEADS) for c0 in range(0, tq, qb)]

    def scores_t(h, c0):
        sl = slice(h * LANES, (h + 1) * LANES)
        return _dot_nt(k_ref[0, :, sl], q_ref[0, c0:c0 + qb, sl])

    def body(use_mask):
        if use_mask:
            q_chunk0 = (q_off + qi_ref[p_id] * tq) // CHUNK
            k_chunk0 = (ki_ref[p_id] * tk) // CHUNK
            kc = k_chunk0 + lax.broadcasted_iota(jnp.int32, (tk, qb), 0) // CHUNK
            qc_local = lax.broadcasted_iota(jnp.int32, (tk, qb), 1) // CHUNK
        ahead = [scores_t(*insts[i]) for i in range(min(MLA_SCORES_AHEAD, len(insts)))]
        for idx, (h, c0) in enumerate(insts):
            s = ahead.pop(0)
            if idx + MLA_SCORES_AHEAD < len(insts):
                ahead.append(scores_t(*insts[idx + MLA_SCORES_AHEAD]))
            if use_mask:
                s = jnp.where(kc <= qc_local + (q_chunk0 + c0 // CHUNK), s, NEG)
            m_prev = m_scr[h:h + 1, c0:c0 + qb]
            m_new = jnp.maximum(m_prev, jnp.max(s, axis=0, keepdims=True))
            m_scr[h:h + 1, c0:c0 + qb] = m_new
            alpha = jnp.exp2(m_prev - m_new)
            p_t = jnp.exp2(s - m_new).astype(BF16)
            vt_h = vt_ref[0, h * LANES:(h + 1) * LANES, :]
            acc_scr[h, :, c0:c0 + qb] = alpha * acc_scr[h, :, c0:c0 + qb] + _dot(vt_h, p_t)

    @pl.when(masked)
    def _():
        body(True)

    @pl.when(jnp.logical_not(masked))
    def _():
        body(False)

    @pl.when(last)
    def _():
        outs = []
        for h in range(MLA_HEADS):
            a = acc_scr[h]
            outs.append(a[0:MLA_V] / a[MLA_V:MLA_V + 1])
        o_ref[0] = jnp.concatenate(outs, axis=0).T.astype(BF16)


MLA_Q_TILE = 1024
MLA_KV_TILE = 512
MLA_QUERY_BLOCK = 256
MLA_SCORES_AHEAD = 3


def _mla_attn(q, k, vt, pairs, tq, tk, q_off):
    b, t_q, _ = q.shape
    qi, ki, fl = pairs
    grid_spec = pltpu.PrefetchScalarGridSpec(
        num_scalar_prefetch=3,
        grid=(b, qi.shape[0]),
        in_specs=[pl.BlockSpec((1, tq, 1024), lambda bb, p, qi, ki, fl: (bb, qi[p], 0)),
                  pl.BlockSpec((1, tk, 1024), lambda bb, p, qi, ki, fl: (bb, ki[p], 0)),
                  pl.BlockSpec((1, 1024, tk), lambda bb, p, qi, ki, fl: (bb, 0, ki[p]))],
        out_specs=pl.BlockSpec((1, tq, 512), lambda bb, p, qi, ki, fl: (bb, qi[p], 0)),
        scratch_shapes=[pltpu.VMEM((MLA_HEADS, tq), F32),
                        pltpu.VMEM((MLA_HEADS, LANES, tq), F32)])
    return pl.pallas_call(
        functools.partial(_mla_attn_kernel, tq=tq, tk=tk, q_off=q_off),
        grid_spec=grid_spec,
        out_shape=jax.ShapeDtypeStruct((b, t_q, 512), BF16),
        compiler_params=_cparams(("parallel", "arbitrary")),
    )(qi, ki, fl, q, k, vt)


def _causal_pairs(nq, ratio):
    qi, ki, fl = [], [], []
    for a in range(nq):
        n_kv = (a + 1) * ratio
        for c in range(n_kv):
            qi.append(a)
            ki.append(c)
            fl.append((1 if c == 0 else 0) | (2 if c == n_kv - 1 else 0) | (4 if c >= a * ratio else 0))
    return tuple(jnp.asarray(np.array(z, np.int32)) for z in (qi, ki, fl))


def _mla_decode_kernel(q_ref, cc_ref, kc_ref, cn_ref, kn_ref, wka_ref, prope_ref, wvp_ref, o_ref,
                       qa_scr, qr_scr, m_scr, l_scr, acc_scr, *, n_cache_blocks, t):
    k_id = pl.program_id(1)

    @pl.when(k_id == 0)
    def _():
        for h in range(MLA_HEADS):
            qs = q_ref[0, :, h * LANES:(h + 1) * LANES]
            qa_scr[h * t:(h + 1) * t, :] = _dot(qs, wka_ref[h]).astype(BF16)
            qr_scr[h * t:(h + 1) * t, :] = _dot(qs, prope_ref[...]).astype(BF16)
        m_scr[...] = jnp.full(m_scr.shape, NEG, F32)
        l_scr[...] = jnp.zeros(l_scr.shape, F32)
        acc_scr[...] = jnp.zeros(acc_scr.shape, F32)

    def update(ckv, kr):
        cb = ckv.astype(BF16)
        s = _dot_nt(qa_scr[...], cb) + _dot_nt(qr_scr[:, 0:MLA_ROPE], kr.astype(BF16))
        m_prev = m_scr[...]
        m_new = jnp.maximum(m_prev, jnp.max(s, axis=-1, keepdims=True))
        alpha = jnp.exp2(m_prev - m_new)
        p = jnp.exp2(s - m_new[:, 0:1])
        l_scr[...] = alpha * l_scr[...] + jnp.sum(p, axis=-1, keepdims=True)
        m_scr[...] = m_new
        acc_scr[...] = jnp.concatenate([alpha, alpha], axis=1) * acc_scr[...] + _dot(p.astype(BF16), cb)

    @pl.when(k_id < n_cache_blocks)
    def _():
        update(cc_ref[0], kc_ref[0])

    @pl.when(k_id == n_cache_blocks)
    def _():
        update(cn_ref[0], kn_ref[0])
        l = l_scr[...]
        o_lat = (acc_scr[...] / jnp.concatenate([l, l], axis=1)).astype(BF16)
        for pr in range(MLA_HEADS // 2):
            oe = o_lat[(2 * pr) * t:(2 * pr + 1) * t]
            oo = o_lat[(2 * pr + 1) * t:(2 * pr + 2) * t]
            o_ref[0, :, pr * LANES:(pr + 1) * LANES] = (
                _dot(oe, wvp_ref[2 * pr]) + _dot(oo, wvp_ref[2 * pr + 1])).astype(BF16)


MLA_DECODE_KV_TILE = 2048


def _mla_decode(q, ckv_cache, kr_cache, layer, ckv_new, kr_new, wka, prope, wvp):
    b, t, _ = q.shape
    n_past = ckv_cache.shape[2]
    kb = math.gcd(n_past, MLA_DECODE_KV_TILE)
    ncb = n_past // kb
    rows = MLA_HEADS * t
    per_b = lambda shape: pl.BlockSpec((1,) + shape, lambda bb, k: (bb, 0, 0))
    cache = lambda w: pl.BlockSpec((None, 1, kb, w), lambda bb, k: (layer, bb, jnp.minimum(k, ncb - 1), 0))
    return pl.pallas_call(
        functools.partial(_mla_decode_kernel, n_cache_blocks=ncb, t=t),
        grid=(b, ncb + 1),
        in_specs=[per_b((t, 1024)), cache(256), cache(MLA_ROPE), per_b((t, 256)), per_b((t, MLA_ROPE)),
                  _const_spec(wka.shape), _const_spec(prope.shape), _const_spec(wvp.shape)],
        out_specs=per_b((t, 512)),
        out_shape=jax.ShapeDtypeStruct((b, t, 512), BF16),
        scratch_shapes=[pltpu.VMEM((rows, 256), BF16), pltpu.VMEM((rows, LANES), BF16),
                        pltpu.VMEM((rows, LANES), F32), pltpu.VMEM((rows, LANES), F32),
                        pltpu.VMEM((rows, 256), F32)],
        compiler_params=_cparams(("parallel", "arbitrary")),
    )(q, ckv_cache, kr_cache, ckv_new, kr_new, wka, prope, wvp)


RW_CHUNK_GROUP = 2
RW_TIME_TILE = 256


def _pair_sum(x, ones_bd):
    hi, lo = _split2(x)
    return _dot(hi, ones_bd) + _dot(lo, ones_bd)


def _rwkv_kernel(pr_ref, sh0_ref, s0_ref, mu_ref, w0_ref, wl_ref, a0_ref, al_ref, g2_ref,
                 kk_ref, ka_ref, rk_ref, lnw_ref, lnb_ref, y_ref, st_ref,
                 prev_scr, s_scr, rt_scr, at_scr, bt_scr, kt_scr, bv_scr, k2_scr, v_scr,
                 cum_scr, yc_scr, *, tt):
    t_id = pl.program_id(1)
    n_pairs = RW_HEADS // 2

    @pl.when(t_id == 0)
    def _():
        s_scr[...] = s0_ref[0]
        prev_scr[0:1, :] = sh0_ref[0]

    pr = pr_ref[0]
    row = lax.broadcasted_iota(jnp.int32, (tt, 1), 0)
    prev = jnp.where(row == 0, prev_scr[0:1, :], pltpu.roll(pr, 1, axis=0))
    prev_scr[0:1, :] = pr[tt - 1:tt, :]
    pm = pr + (prev - pr) * mu_ref[...]
    r = pm[:, 0:512]
    k = pm[:, 512:1024]
    v = pm[:, 1024:1536]
    wa = pm[:, 1536:1664]
    g_in = pm[:, 1664:1792]
    z = w0_ref[...] + _dot(jnp.tanh(wa).astype(BF16), wl_ref[...])
    nz = -z
    w = -(jnp.maximum(nz, 0.0) + jnp.log(1.0 + jnp.exp(-jnp.abs(nz)))) - 0.5
    ld = -jnp.exp(w)
    a_sig = jax.nn.sigmoid(a0_ref[...] + _dot(wa.astype(BF16), al_ref[...]))
    g = _dot(jax.nn.sigmoid(g_in).astype(BF16), g2_ref[...])

    li = lax.broadcasted_iota(jnp.int32, (LANES, LANES), 0)
    lj = lax.broadcasted_iota(jnp.int32, (LANES, LANES), 1)
    ones_bd = jnp.where((li // RW_N) == (lj // RW_N), 1.0, 0.0).astype(BF16)

    def head_sum(x):
        return jnp.concatenate([_pair_sum(x[:, p * LANES:(p + 1) * LANES], ones_bd)
                                for p in range(n_pairs)], axis=1)

    kk = k * kk_ref[...]
    kk = kk * lax.rsqrt(jnp.maximum(head_sum(kk * kk), 1e-24))
    k2 = k * (1.0 + (a_sig - 1.0) * ka_ref[...])
    bonus = head_sum(r * k2 * rk_ref[...]) * v

    ti = lax.broadcasted_iota(jnp.int32, (tt, tt), 0)
    tj = lax.broadcasted_iota(jnp.int32, (tt, tt), 1)
    tri = jnp.where(((ti // CHUNK) == (tj // CHUNK)) & (tj <= ti), 1.0, 0.0).astype(BF16)
    l1 = ld.astype(BF16)
    rem = ld - l1.astype(F32)
    l2 = rem.astype(BF16)
    l3 = (rem - l2.astype(F32)).astype(BF16)
    cum = (_dot(tri, l3) + _dot(tri, l2)) + _dot(tri, l1)
    winv = jnp.exp(-cum)
    rt_scr[...] = r * jnp.exp(cum)
    at_scr[...] = -kk * jnp.exp(cum - ld)
    bv = kk * a_sig
    bt_scr[...] = bv * winv
    kt_scr[...] = k2 * winv
    bv_scr[...] = bv
    k2_scr[...] = k2
    v_scr[...] = v
    cum_scr[...] = cum

    lane = lax.broadcasted_iota(jnp.int32, (CHUNK, LANES), 1)
    even = lane < RW_N
    strict = (lj % RW_N) < (li % RW_N)
    incl = (lj % RW_N) <= (li % RW_N)
    eye = jnp.where(li == lj, 1.0, 0.0).astype(F32)
    pairs = range(n_pairs)

    def same_block(m):
        return (li // m) == (lj // m)

    def stack_f32(x):
        return jnp.concatenate([jnp.where(even, x, 0.0), jnp.where(even, 0.0, x)], axis=0)

    def stack(x):
        return stack_f32(x).astype(BF16)

    mm = lambda a, b: _dot(a.astype(BF16), b.astype(BF16))
    rows_of = lambda c: slice(c * CHUNK, (c + 1) * CHUNK)
    lanes_of = lambda p: slice(p * LANES, (p + 1) * LANES)

    def independent_phase(chunks, res):
        insts = [(c, p) for c in chunks for p in pairs]
        load = lambda scr: [scr[rows_of(c), lanes_of(p)] for c, p in insts]
        cum_c = load(cum_scr)
        cum_l = [a[CHUNK - 1:CHUNK, :] for a in cum_c]
        w2 = [jnp.exp(a - b) for a, b in zip(cum_l, cum_c)]
        rs = [stack(a) for a in load(rt_scr)]
        as_ = [stack(a) for a in load(at_scr)]
        bs = [stack(a) for a in load(bt_scr)]
        ks = [stack(a) for a in load(kt_scr)]
        v_f = [stack_f32(a) for a in load(v_scr)]
        vs = [a.astype(BF16) for a in v_f]
        b2s = [stack(a * w) for a, w in zip(load(bv_scr), w2)]
        k2s = [stack(a * w) for a, w in zip(load(k2_scr), w2)]
        yield
        n_m = [jnp.where(strict, _dot_nt(a, b), 0.0) for a, b in zip(as_, bs)]
        mk = [jnp.where(strict, _dot_nt(a, b), 0.0).astype(BF16) for a, b in zip(as_, ks)]
        cb = [jnp.where(incl, _dot_nt(a, b), 0.0).astype(BF16) for a, b in zip(rs, bs)]
        ck = [jnp.where(incl, _dot_nt(a, b), 0.0).astype(BF16) for a, b in zip(rs, ks)]
        yield
        n8 = [jnp.where(same_block(8), n, 0.0) for n in n_m]
        t = [eye + a for a in n8]
        p2 = [mm(a, a) for a in n8]
        mv = [_dot(a, b) for a, b in zip(mk, vs)]
        yield
        t = [a + mm(a, b) for a, b in zip(t, p2)]
        p4 = [mm(a, a) for a in p2]
        cv = [_dot(a, b) for a, b in zip(ck, vs)]
        yield
        t = [a + mm(a, b) for a, b in zip(t, p4)]
        vk = [_dot(a.T.astype(BF16), b) for a, b in zip(v_f, k2s)]
        yield
        for m in (8, 16, 32):
            off = same_block(2 * m) & jnp.logical_not(same_block(m))
            x = [mm(jnp.where(off, n, 0.0), a) for n, a in zip(n_m, t)]
            yield
            t = [a + mm(a, b) for a, b in zip(t, x)]
            yield
        for i, key in enumerate(insts):
            res[key] = dict(as_=as_[i], rs=rs[i], b2s=b2s[i], cb=cb[i], mv=mv[i], cv=cv[i], vk=vk[i],
                            tinv=t[i].astype(BF16), wl=jnp.exp(cum_l[i]))

    def dependent_phase(chunks, res):
        for c in chunks:
            rc = [res[(c, p)] for p in pairs]
            s = [s_scr[p] for p in pairs]
            sb = [a.astype(BF16) for a in s]
            x = [_dot_nt(r_['as_'], b) + r_['mv'] for r_, b in zip(rc, sb)]
            rs_s = [_dot_nt(r_['rs'], b) + r_['cv'] for r_, b in zip(rc, sb)]
            yield
            u = [_dot(r_['tinv'], a.astype(BF16)) for r_, a in zip(rc, x)]
            yield
            ys = [a + _dot(r_['cb'], b.astype(BF16)) for a, r_, b in zip(rs_s, rc, u)]
            for p in pairs:
                yc_scr[rows_of(c), lanes_of(p)] = ys[p][0:CHUNK] + ys[p][CHUNK:2 * CHUNK]
                s_scr[p] = s[p] * rc[p]['wl'] + _dot(u[p].T.astype(BF16), rc[p]['b2s']) + rc[p]['vk']
            yield

    def emit(*gens):
        gens = list(gens)
        while gens:
            for gen in list(gens):
                try:
                    next(gen)
                except StopIteration:
                    gens.remove(gen)

    n_chunks = tt // CHUNK
    groups = [list(range(c0, min(c0 + RW_CHUNK_GROUP, n_chunks))) for c0 in range(0, n_chunks, RW_CHUNK_GROUP)]
    res = {}
    emit(independent_phase(groups[0], res))
    for gi in range(1, len(groups)):
        emit(independent_phase(groups[gi], res), dependent_phase(groups[gi - 1], res))
    emit(dependent_phase(groups[-1], res))

    y = yc_scr[...]
    mean = head_sum(y) * (1.0 / RW_N)
    dlt = y - mean
    var = head_sum(dlt * dlt) * (1.0 / RW_N)
    yn = dlt * lax.rsqrt(var + RW_GN_EPS) * lnw_ref[...] + lnb_ref[...]
    y_ref[0] = ((yn + bonus) * g).astype(BF16)

    @pl.when(t_id == pl.num_programs(1) - 1)
    def _():
        st_ref[0] = s_scr[...]


def _rwkv(prw, sh0, s0_bd, wts):
    b, t, _ = prw.shape
    tt = min(RW_TIME_TILE, t)
    assert t % tt == 0 and tt % CHUNK == 0
    c512 = _const_spec((1, RW_C))
    scr = lambda: pltpu.VMEM((tt, RW_C), F32)
    return pl.pallas_call(
        functools.partial(_rwkv_kernel, tt=tt),
        grid=(b, t // tt),
        in_specs=[pl.BlockSpec((1, tt, RW_COLS), lambda bb, i: (bb, i, 0)),
                  pl.BlockSpec((1, 1, RW_COLS), lambda bb, i: (bb, 0, 0)),
                  pl.BlockSpec((1, 4, LANES, LANES), lambda bb, i: (bb, 0, 0, 0)),
                  _const_spec((1, RW_COLS)), c512, _const_spec((LANES, RW_C)), c512,
                  _const_spec((LANES, RW_C)), _const_spec((LANES, RW_C)),
                  c512, c512, c512, c512, c512],
        out_specs=[pl.BlockSpec((1, tt, RW_C), lambda bb, i: (bb, i, 0)),
                   pl.BlockSpec((1, 4, LANES, LANES), lambda bb, i: (bb, 0, 0, 0))],
        out_shape=[jax.ShapeDtypeStruct((b, t, RW_C), BF16),
                   jax.ShapeDtypeStruct((b, 4, LANES, LANES), F32)],
        scratch_shapes=[pltpu.VMEM((8, RW_COLS), F32), pltpu.VMEM((4, LANES, LANES), F32)]
                       + [scr() for _ in range(9)],
        compiler_params=_cparams(("parallel", "arbitrary")),
    )(prw, sh0, s0_bd, *wts)


def _state_to_bd(s):
    b = s.shape[0]
    s = s.reshape(b, 4, 2, RW_N, RW_N)
    z = jnp.zeros_like(s[:, :, 0])
    top = jnp.concatenate([s[:, :, 0], z], axis=-1)
    bot = jnp.concatenate([z, s[:, :, 1]], axis=-1)
    return jnp.concatenate([top, bot], axis=-2)


def _state_from_bd(s):
    b = s.shape[0]
    return jnp.stack([s[:, :, :RW_N, :RW_N], s[:, :, RW_N:, RW_N:]], axis=2).reshape(b, RW_HEADS, RW_N, RW_N)


_ODD_SEGS = ([(1024 + g * 64, 1088 + g * 64) for g in range(SW_KV_HEADS) for _ in range(2)]
             + [(1280 + g * 64, 1344 + g * 64) for g in range(SW_KV_HEADS) for _ in range(2)])


def _odd_in_kernel(x_ref, mods_ref, ng_ref, w_ref, b_ref, q_out, k_out, v_out, *, nb):
    h = _norm_mod(x_ref[...], ng_ref[...], mods_ref[:, 3, :], mods_ref[:, 4, :], nb).astype(BF16)
    q = _dot(h, w_ref[:, 0:1024]) + b_ref[:, 0:1024]
    q_out[...] = (q * (SW_HD ** -0.5 * LOG2E)).astype(BF16)
    k_out[...] = _dot(h, w_ref[:, 1024:1536]) + b_ref[:, 1024:1536]
    v_out[...] = _dot(h, w_ref[:, 1536:2048]) + b_ref[:, 1536:2048]


def _odd_in(x, mods, norm_g, w_perm, b_perm, rows_per_seq):
    n, d = x.shape
    tm, nb = _row_tiling(n, rows_per_seq, 512)
    row = lambda w: pl.BlockSpec((tm, w), lambda i: (i, 0))
    return pl.pallas_call(
        functools.partial(_odd_in_kernel, nb=nb),
        grid=(n // tm,),
        in_specs=[row(d), _mods_spec(tm, nb, rows_per_seq), _const_spec((1, d)),
                  _const_spec(w_perm.shape), _const_spec((1, 2048))],
        out_specs=[row(1024), row(512), row(512)],
        out_shape=[jax.ShapeDtypeStruct((n, 1024), BF16), jax.ShapeDtypeStruct((n, 512), F32),
                   jax.ShapeDtypeStruct((n, 512), F32)],
        compiler_params=_cparams(("parallel",)),
    )(x, mods, norm_g.reshape(1, d), w_perm, b_perm.reshape(1, 2048))


def _swa_kernel(sinks_ref, q_ref, kp_ref, kc_ref, vp_ref, vc_ref, o_ref, *, tq, mask_first_prev):
    nk = WINDOW + tq
    qi = lax.broadcasted_iota(jnp.int32, (tq, nk), 0)
    kj = lax.broadcasted_iota(jnp.int32, (tq, nk), 1)
    kc = kj // CHUNK - WINDOW // CHUNK
    qc = qi // CHUNK
    vis = (kc <= qc) & (kc >= qc - WINDOW // CHUNK)
    if mask_first_prev:
        vis = vis & ((kj >= WINDOW) | (pl.program_id(1) > 0))
    ndist = jnp.where(vis, -jnp.abs(qi + WINDOW - kj).astype(F32), NEG)
    keys = jnp.concatenate([kp_ref[0], kc_ref[0]], axis=0).astype(BF16)
    vals = jnp.concatenate([vp_ref[0], vc_ref[0]], axis=0)
    klane = lax.broadcasted_iota(jnp.int32, (nk, LANES), 1)
    v_slots = [jnp.where(klane < SW_HD, vals[:, g * LANES:(g + 1) * LANES], 1.0).astype(BF16)
               for g in range(SW_KV_HEADS)]
    lane = lax.broadcasted_iota(jnp.int32, (tq, LANES), 1)
    low = lane < SW_HD
    sb = min(SWA_SUB_BLOCK, tq)

    def scores(h):
        qp = q_ref[0, :, (h // 2) * LANES:(h // 2 + 1) * LANES]
        qh = jnp.where(low if h % 2 == 0 else jnp.logical_not(low), qp, jnp.zeros_like(qp))
        return _dot_nt(qh, keys[:, (h // SW_GROUP) * LANES:(h // SW_GROUP + 1) * LANES])

    outs = []
    ahead = [scores(h) for h in range(SWA_SCORES_AHEAD)]
    for h in range(SW_HEADS):
        s = ahead.pop(0)
        if h + SWA_SCORES_AHEAD < SW_HEADS:
            ahead.append(scores(h + SWA_SCORES_AHEAD))
        slope = (2.0 ** (-8.0 * (h + 1) / SW_HEADS)) * LOG2E
        sk = sinks_ref[h] * LOG2E
        es, ms = [], []
        for r1 in range(0, tq, sb):
            z = s[r1:r1 + sb] + slope * ndist[r1:r1 + sb]
            m = jnp.maximum(jnp.broadcast_to(jnp.max(z, axis=-1, keepdims=True), (sb, LANES)), sk)
            m_wide = jnp.concatenate([m] * (nk // LANES), axis=1) if nk % LANES == 0 else m[:, 0:1]
            es.append(jnp.exp2(z - m_wide).astype(BF16))
            ms.append(m)
        e = jnp.concatenate(es, axis=0) if len(es) > 1 else es[0]
        m = jnp.concatenate(ms, axis=0) if len(ms) > 1 else ms[0]
        pv = _dot(e, v_slots[h // SW_GROUP])
        outs.append(pv / (pltpu.roll(pv, SW_HD, axis=1) + jnp.exp2(sk - m)))
    for pr in range(SW_HEADS // 2):
        o_ref[0, :, pr * LANES:(pr + 1) * LANES] = jnp.where(
            low, outs[2 * pr], pltpu.roll(outs[2 * pr + 1], SW_HD, axis=1)).astype(BF16)


SWA_SUB_BLOCK = 32
SWA_SCORES_AHEAD = 2


def _swa(q, k_prev, k_cur, v_prev, v_cur, sinks, tq, same_array):
    b, t, _ = q.shape
    nt = t // tq
    per = tq // WINDOW
    if same_array:
        prev_map = lambda bb, i: (bb, jnp.maximum(i * per - 1, 0), 0)
    else:
        prev_map = lambda bb, i: (bb, 0, 0)
    cur = lambda w: pl.BlockSpec((1, tq, w), lambda bb, i: (bb, i, 0))
    prev = pl.BlockSpec((1, WINDOW, 512), prev_map)
    return pl.pallas_call(
        functools.partial(_swa_kernel, tq=tq, mask_first_prev=same_array),
        grid=(b, nt),
        in_specs=[pl.BlockSpec(memory_space=pltpu.SMEM), cur(1024), prev, cur(512), prev, cur(512)],
        out_specs=cur(1024),
        out_shape=jax.ShapeDtypeStruct((b, t, 1024), BF16),
        compiler_params=_cparams(("parallel", "parallel")),
    )(sinks, q, k_prev, k_cur, v_prev, v_cur)


def _undup(a):
    return a.reshape(a.shape[:-1] + (SW_KV_HEADS, 2, SW_HD))[..., 0, :]


def _dup(a):
    return jnp.concatenate([a, a], axis=-1).reshape(a.shape[:-2] + (512,))


def _prep_weights(p):
    depth = p['w_ada'].shape[0]
    n_even, n_odd = (depth + 1) // 2, depth // 2
    w = {}
    w['ffn_in'] = [[p['ffn_w_in'][i, s].astype(BF16) for s in range(2)] for i in range(depth)]
    w['ffn_out'] = [[p['ffn_w_out'][i, s].astype(BF16) for s in range(2)] for i in range(depth)]
    wi = p['even_w_in'].astype(BF16)
    w['even_in'] = jnp.concatenate(
        [wi[:, :, 0:1024]] + [wi[:, :, _RW_OFF + a:_RW_OFF + b] for a, b in _RW_SEGS]
        + [wi[:, :, 1024:1056], jnp.zeros((n_even, 1024, 96), BF16)], axis=2)
    w['wuq'] = jnp.pad(p['mla_w_uq'], ((0, 0), (0, 0), (0, 0), (0, 32))).reshape(n_even, 768, 1024).astype(BF16)
    w['wk'] = jnp.pad(p['mla_w_ukv'][..., :MLA_NOPE], ((0, 0), (0, 0), (0, 0), (0, 64))
                      ).reshape(n_even, 256, 1024).astype(BF16)
    w['wv'] = jnp.swapaxes(jnp.pad(p['mla_w_ukv'][..., MLA_NOPE:], ((0, 0), (0, 0), (0, 0), (0, 64))
                                   ).reshape(n_even, 256, 1024), 1, 2).astype(BF16)
    sel = np.zeros((MLA_ROPE, 1024), np.float32)
    one = np.zeros((1024, 1), np.float32)
    for h in range(MLA_HEADS):
        sel[np.arange(MLA_ROPE), h * LANES + MLA_NOPE + np.arange(MLA_ROPE)] = 1.0
        one[h * LANES + MLA_V, 0] = 1.0
    w['sel'] = jnp.asarray(sel).astype(BF16)
    w['one'] = jnp.asarray(one)
    wk_t = jnp.transpose(p['mla_w_ukv'][..., :MLA_NOPE], (0, 2, 3, 1))
    w['wka'] = jnp.pad(wk_t, ((0, 0), (0, 0), (0, 64), (0, 0))).astype(BF16)
    prope = np.zeros((LANES, LANES), np.float32)
    prope[MLA_NOPE + np.arange(MLA_ROPE), np.arange(MLA_ROPE)] = 1.0
    w['prope'] = jnp.asarray(prope).astype(BF16)
    wv_h = jnp.transpose(p['mla_w_ukv'][..., MLA_NOPE:], (0, 2, 1, 3))
    wv_even = jnp.pad(wv_h[:, 0::2], ((0, 0), (0, 0), (0, 0), (0, 64)))
    wv_odd = jnp.pad(wv_h[:, 1::2], ((0, 0), (0, 0), (0, 0), (64, 0)))
    w['wvp'] = jnp.stack([wv_even, wv_odd], axis=2).reshape(n_even, MLA_HEADS, 256, LANES).astype(BF16)
    z64 = jnp.zeros((n_even, 64, RW_C), F32)
    w['wl'] = jnp.concatenate([p['rw_w2'], z64], axis=1).astype(BF16)
    w['al'] = jnp.concatenate([z64, p['rw_a2']], axis=1).astype(BF16)
    w['g2'] = p['rw_g2'].astype(BF16)
    w['mu'] = _rw_permute(p['rw_mu'])
    w['even_out'] = p['even_w_out'].astype(BF16)
    wo = p['odd_w_qkv'].astype(BF16)
    w['odd_in'] = jnp.concatenate([wo[:, :, 0:1024]] + [wo[:, :, a:b] for a, b in _ODD_SEGS], axis=2)
    bo = p['odd_b_qkv']
    w['odd_b'] = jnp.concatenate([bo[:, 0:1024]] + [bo[:, a:b] for a, b in _ODD_SEGS], axis=1)
    w['odd_out'] = p['odd_w_out'].astype(BF16)
    return w


def _trunk(x3, mods_all, start, past, p, w):
    b, t, d = x3.shape
    n = b * t
    depth = mods_all.shape[0]
    x = x3.reshape(n, d)
    rows = t
    tm_even, _ = _row_tiling(n, rows, 512)
    pos = start + jnp.arange(t)
    tabs = _rope_tables(pos, tm_even)
    even_states, odd_states = [], []
    for i in range(depth):
        mods = mods_all[i]
        j = i // 2
        x = _ffn(x, mods, p['norm_g'][i, 0], w['ffn_in'][i][0], w['ffn_out'][i][0], 0, rows)
        if i % 2 == 0:
            q, ckv, kr, prw = _even_in(x, mods, p['norm_g'][i, 1], w['even_in'][j], p['mla_q_norm'][j],
                                       p['mla_kv_norm'][j], w['wuq'][j], tabs, rows)
            if past is None:
                kx, vtx = _kv_expand(ckv, kr, w['wk'][j], w['sel'], w['wv'][j], w['one'], b)
                tk = min(MLA_KV_TILE, t)
                tq = min(MLA_Q_TILE, t)
                att = _mla_attn(q.reshape(b, t, 1024), kx.reshape(b, t, 1024), vtx,
                                _causal_pairs(t // tq, tq // tk), tq, tk, 0)
                s0 = jnp.zeros((b, RW_HEADS, RW_N, RW_N), F32)
                sh0 = jnp.zeros((b, RW_COLS), F32)
            else:
                s0, sh0 = past[2][j], past[3][j]
                att = _mla_decode(q.reshape(b, t, 1024), past[0], past[1], j, ckv.reshape(b, t, 256),
                                  kr.reshape(b, t, MLA_ROPE), w['wka'][j], w['prope'], w['wvp'][j])
            rw_wts = (w['mu'][j].reshape(1, RW_COLS), p['rw_w0'][j].reshape(1, RW_C), w['wl'][j],
                      p['rw_a0'][j].reshape(1, RW_C), w['al'][j], w['g2'][j],
                      p['rw_k_k'][j].reshape(1, RW_C), p['rw_k_a'][j].reshape(1, RW_C),
                      p['rw_r_k'][j].reshape(1, RW_C), p['rw_ln_w'][j].reshape(1, RW_C),
                      p['rw_ln_b'][j].reshape(1, RW_C))
            prw3 = prw.reshape(b, t, RW_COLS)
            y_rw, s_bd = _rwkv(prw3, _rw_permute(sh0).reshape(b, 1, RW_COLS), _state_to_bd(s0), rw_wts)
            mix = ((att.reshape(n, 512), w['even_out'][j][:512]), (y_rw.reshape(n, RW_C), w['even_out'][j][512:]))
            even_states.append((ckv.reshape(b, t, 256), kr.reshape(b, t, MLA_ROPE), _state_from_bd(s_bd),
                                _rw_unpermute(prw3[:, t - 1, :])))
        else:
            q, kd, vd = _odd_in(x, mods, p['norm_g'][i, 1], w['odd_in'][j], w['odd_b'][j], rows)
            q3, kd3, vd3 = q.reshape(b, t, 1024), kd.reshape(b, t, 512), vd.reshape(b, t, 512)
            if past is None:
                tq = min(256, t)
                o = _swa(q3, kd3, kd3, vd3, vd3, p['swa_sinks'][j], tq, True)
                keep = min(WINDOW, t)
                k_new, v_new = _undup(kd3[:, t - keep:]), _undup(vd3[:, t - keep:])
            else:
                k_past, v_past = past[4][j], past[5][j]
                o = _swa(q3, _dup(k_past), kd3, _dup(v_past), vd3, p['swa_sinks'][j], t, False)
                k_new = jnp.concatenate([k_past, _undup(kd3)], axis=1)[:, t:]
                v_new = jnp.concatenate([v_past, _undup(vd3)], axis=1)[:, t:]
            mix = ((o.reshape(n, 1024), w['odd_out'][j]),)
            odd_states.append((k_new, v_new))
        fg = p['final_norm_g'] if i == depth - 1 else None
        x = _ffn(x, mods, p['norm_g'][i, 2], w['ffn_in'][i][1], w['ffn_out'][i][1], 2, rows, final_g=fg, mix=mix)
    es = [jnp.stack([st[k] for st in even_states]) for k in range(4)]
    os_ = [jnp.stack([st[k] for st in odd_states]) for k in range(2)]
    return x.reshape(b, t, d), es + os_


def kernel(x_prompt, x_sample, cache_mla_ckv, cache_mla_krope, state_rwkv, state_rwkv_shift, cache_swa_k, cache_swa_v, c_prompt, c_sample, w_ada, b_ada, norm_g, ffn_w_in, ffn_w_out, even_w_in, even_w_out, mla_q_norm, mla_kv_norm, mla_w_uq, mla_w_ukv, rw_mu, rw_w0, rw_w2, rw_a0, rw_a2, rw_g2, rw_k_k, rw_k_a, rw_r_k, rw_ln_w, rw_ln_b, odd_w_qkv, odd_b_qkv, odd_w_out, swa_sinks, final_norm_g):
    p = dict(w_ada=w_ada, b_ada=b_ada, norm_g=norm_g, ffn_w_in=ffn_w_in, ffn_w_out=ffn_w_out,
             even_w_in=even_w_in, even_w_out=even_w_out, mla_q_norm=mla_q_norm, mla_kv_norm=mla_kv_norm,
             mla_w_uq=mla_w_uq, mla_w_ukv=mla_w_ukv, rw_mu=rw_mu, rw_w0=rw_w0, rw_w2=rw_w2, rw_a0=rw_a0,
             rw_a2=rw_a2, rw_g2=rw_g2, rw_k_k=rw_k_k, rw_k_a=rw_k_a, rw_r_k=rw_r_k, rw_ln_w=rw_ln_w,
             rw_ln_b=rw_ln_b, odd_w_qkv=odd_w_qkv, odd_b_qkv=odd_b_qkv, odd_w_out=odd_w_out,
             swa_sinks=swa_sinks, final_norm_g=final_norm_g)
    w = _prep_weights(p)
    depth = w_ada.shape[0]
    bp, bs = c_prompt.shape[0], c_sample.shape[0]
    d = c_prompt.shape[1]
    b_pad = -(-(bp + bs) // 8) * 8
    c_all = jnp.concatenate([c_prompt, c_sample, jnp.zeros((b_pad - bp - bs, d), F32)], axis=0)
    mods = _ada(c_all, w_ada, b_ada).reshape(depth, b_pad, 3 * N_SUB, d)
    y_prompt, sp = _trunk(x_prompt, mods[:, :bp], 0, None, p, w)
    past = (cache_mla_ckv, cache_mla_krope, state_rwkv, state_rwkv_shift, cache_swa_k, cache_swa_v)
    y_sample, ss = _trunk(x_sample, mods[:, bp:bp + bs], cache_mla_ckv.shape[2], past, p, w)
    return (y_prompt, y_sample, sp[0], sp[1], sp[2], sp[3], sp[4], sp[5],
            ss[0], ss[1], ss[2], ss[3], ss[4], ss[5])
```

```python
import functools
import math

import jax
import jax.numpy as jnp
import numpy as np
from jax import lax
from jax.experimental import pallas as pl
from jax.experimental.pallas import tpu as pltpu

F32 = jnp.float32
BF16 = jnp.bfloat16

CHUNK = 64
EPS = 1e-6
NEG = -1e30
N_SUB = 3
MLA_HEADS = 8
MLA_NOPE = 64
MLA_ROPE = 32
MLA_V = 64
MLA_Q_LORA = 768
MLA_KV_LORA = 256
MLA_SCALE = (MLA_NOPE + MLA_ROPE) ** -0.5
LOG2E = math.log2(math.e)
ROPE_BASE = 10000.0
RW_HEADS = 8
RW_N = 64
RW_C = RW_HEADS * RW_N
RW_GN_EPS = 64e-5
RW_COLS = 3 * RW_C + 64 + 64 + 128
SW_HEADS = 16
SW_KV_HEADS = 4
SW_GROUP = 4
SW_HD = 64
WINDOW = 128

LANES = 128
VMEM_LIMIT = 56 * 1024 * 1024


def _cparams(sem):
    return pltpu.CompilerParams(dimension_semantics=sem, vmem_limit_bytes=VMEM_LIMIT)


def _const_spec(shape):
    nd = len(shape)
    return pl.BlockSpec(shape, lambda *_: (0,) * nd, pipeline_mode=pl.Buffered(1))


def _dot(a, b):
    return jnp.dot(a, b, preferred_element_type=F32)


def _dot_nt(a, b):
    return lax.dot_general(a, b, (((1,), (1,)), ((), ())), preferred_element_type=F32)


def _split2(x):
    hi = x.astype(BF16)
    lo = (x - hi.astype(F32)).astype(BF16)
    return hi, lo


def _norm_mod(x, g, sh, sc, nb):
    y = x * lax.rsqrt(jnp.mean(x * x, axis=-1, keepdims=True) + EPS) * g
    if nb == 1:
        return y * (1.0 + sc) + sh
    tm, d = x.shape
    y3 = y.reshape(nb, tm // nb, d)
    return (y3 * (1.0 + sc[:, None, :]) + sh[:, None, :]).reshape(tm, d)


def _gate_rows(g, y, nb):
    if nb == 1:
        return g * y
    tm, d = y.shape
    return (y.reshape(nb, tm // nb, d) * g[:, None, :]).reshape(tm, d)


def _row_tiling(n_rows, rows_per_seq, pref):
    if rows_per_seq >= pref:
        assert rows_per_seq % pref == 0
        return pref, 1
    tm = min(pref, n_rows)
    assert tm % rows_per_seq == 0 and n_rows % tm == 0
    return tm, tm // rows_per_seq


def _mods_spec(tm, nb, rows_per_seq):
    if nb == 1:
        tiles_per_seq = rows_per_seq // tm
        return pl.BlockSpec((1, 3 * N_SUB, 1024), lambda i: (i // tiles_per_seq, 0, 0))
    return pl.BlockSpec((nb, 3 * N_SUB, 1024), lambda i: (i, 0, 0))


def _ada_kernel(c_ref, w_ref, b_ref, o_ref):
    c = c_ref[...]
    cs = (c * jax.nn.sigmoid(c)).astype(BF16)
    o_ref[0] = _dot(cs, w_ref[0].astype(BF16)) + b_ref[0]


def _ada(c_all, w_ada, b_ada):
    depth, d, n = w_ada.shape
    bp = c_all.shape[0]
    tn = n // 4
    return pl.pallas_call(
        _ada_kernel,
        grid=(depth, n // tn),
        in_specs=[pl.BlockSpec((bp, d), lambda l, j: (0, 0)),
                  pl.BlockSpec((1, d, tn), lambda l, j: (l, 0, j)),
                  pl.BlockSpec((1, 1, tn), lambda l, j: (l, 0, j))],
        out_specs=pl.BlockSpec((1, bp, tn), lambda l, j: (l, 0, j)),
        out_shape=jax.ShapeDtypeStruct((depth, bp, n), F32),
        compiler_params=_cparams(("parallel", "parallel")),
    )(c_all, w_ada, b_ada.reshape(depth, 1, n))


FF_CHUNK = 256


def _ffn_kernel(x_ref, mods_ref, ng_ref, win_ref, wout_ref, *rest, sub, nb, d_ff, final, n_mix):
    mix_a, mix_w, rest = rest[:n_mix], rest[n_mix:2 * n_mix], rest[2 * n_mix:]
    if final:
        fg_ref, o_ref, a_scr = rest
    else:
        o_ref, a_scr = rest
    x = x_ref[...]
    if n_mix:
        ym = _dot(mix_a[0][...], mix_w[0][...])
        for a, w in zip(mix_a[1:], mix_w[1:]):
            ym = ym + _dot(a[...], w[...])
        x = x + _gate_rows(mods_ref[:, 5, :], ym, nb)
    sh = mods_ref[:, 3 * sub, :]
    sc = mods_ref[:, 3 * sub + 1, :]
    gt = mods_ref[:, 3 * sub + 2, :]
    h = _norm_mod(x, ng_ref[...], sh, sc, nb).astype(BF16)
    for c in range(d_ff // FF_CHUNK):
        lo = c * FF_CHUNK
        g = _dot(h, win_ref[:, lo:lo + FF_CHUNK])
        u = _dot(h, win_ref[:, d_ff + lo:d_ff + lo + FF_CHUNK])
        a_scr[:, lo:lo + FF_CHUNK] = (g * jax.nn.sigmoid(g) * u).astype(BF16)
    y = _dot(a_scr[...], wout_ref[...])
    out = x + _gate_rows(0.5 * gt, y, nb)
    if final:
        out = out * lax.rsqrt(jnp.mean(out * out, axis=-1, keepdims=True) + EPS) * fg_ref[...]
    o_ref[...] = out


def _ffn(x, mods, norm_g, w_in, w_out, layer, half, sub, rows_per_seq, final_g=None, mix=()):
    n, d = x.shape
    d_ff = w_out.shape[2]
    tm, nb = _row_tiling(n, rows_per_seq, 512)
    final = final_g is not None
    stacked = lambda a: pl.BlockSpec((None, None) + a.shape[2:], lambda i: (layer, half, 0, 0),
                                     pipeline_mode=pl.Buffered(1))
    in_specs = [pl.BlockSpec((tm, d), lambda i: (i, 0)),
                _mods_spec(tm, nb, rows_per_seq),
                _const_spec((1, d)), stacked(w_in), stacked(w_out)]
    args = [x, mods, norm_g.reshape(1, d), w_in, w_out]
    in_specs += [pl.BlockSpec((tm, a.shape[1]), lambda i: (i, 0)) for a, _ in mix]
    in_specs += [_const_spec(wm.shape) for _, wm in mix]
    args += [a for a, _ in mix] + [wm for _, wm in mix]
    if final:
        in_specs.append(_const_spec((1, d)))
        args.append(final_g.reshape(1, d))
    return pl.pallas_call(
        functools.partial(_ffn_kernel, sub=sub, nb=nb, d_ff=d_ff, final=final, n_mix=len(mix)),
        grid=(n // tm,),
        in_specs=in_specs,
        out_specs=pl.BlockSpec((tm, d), lambda i: (i, 0)),
        out_shape=jax.ShapeDtypeStruct((n, d), F32),
        scratch_shapes=[pltpu.VMEM((tm, d_ff), BF16)],
        compiler_params=_cparams(("parallel",)),
    )(*args)


EVEN_W = 768 + 256 + 1792 + 128
_RW_OFF = 1056
_RW_SEGS = [(0, 512), (576, 1088), (1088, 1600), (512, 576), (1600, 1664), (1664, 1792)]
_RW_INV_SEGS = [(0, 512), (1536, 1600), (512, 1024), (1024, 1536), (1600, 1664), (1664, 1792)]


def _rw_permute(a):
    return jnp.concatenate([a[..., s:e] for s, e in _RW_SEGS], axis=-1)


def _rw_unpermute(a):
    return jnp.concatenate([a[..., s:e] for s, e in _RW_INV_SEGS], axis=-1)


def _rope_slot(v, c, s1, s2):
    w = v.shape[-1]
    return v * c + pltpu.roll(v, w - 16, axis=1) * s1 + pltpu.roll(v, 16, axis=1) * s2


def _even_in_kernel(x_ref, mods_ref, ng_ref, w_ref, qn_ref, kvn_ref, wuq_ref,
                    cq_ref, s1q_ref, s2q_ref, ck_ref, s1k_ref, s2k_ref,
                    q_out, ckv_out, kr_out, prw_out, *, nb):
    x = x_ref[...]
    h = _norm_mod(x, ng_ref[...], mods_ref[:, 3, :], mods_ref[:, 4, :], nb).astype(BF16)
    cq = _dot(h, w_ref[:, 0:768])
    cqn = (cq * lax.rsqrt(jnp.mean(cq * cq, axis=-1, keepdims=True) + EPS) * qn_ref[...]).astype(BF16)
    q = _dot(cqn, wuq_ref[...])
    rep = lambda t: jnp.concatenate([t] * MLA_HEADS, axis=1)
    q = _rope_slot(q, rep(cq_ref[...]), rep(s1q_ref[...]), rep(s2q_ref[...]))
    q_out[...] = (q * (MLA_SCALE * LOG2E)).astype(BF16)
    ckv = _dot(h, w_ref[:, 768:1024])
    ckv_out[...] = ckv * lax.rsqrt(jnp.mean(ckv * ckv, axis=-1, keepdims=True) + EPS) * kvn_ref[...]
    prw_out[...] = _dot(h, w_ref[:, 1024:2816])
    krs = _dot(h, w_ref[:, 2816:2944])
    krs = _rope_slot(krs, ck_ref[...], s1k_ref[...], s2k_ref[...])
    kr_out[...] = krs[:, 0:MLA_ROPE]


def _even_in(x, mods, norm_g, w_perm, q_norm, kv_norm, wuq_slot, tabs, rows_per_seq):
    n, d = x.shape
    tm, nb = _row_tiling(n, rows_per_seq, 512)
    ttab = tabs[0].shape[0]
    ntab = ttab // tm
    tab_spec = pl.BlockSpec((tm, LANES), lambda i: (i % ntab, 0))
    row = lambda w: pl.BlockSpec((tm, w), lambda i: (i, 0))
    return pl.pallas_call(
        functools.partial(_even_in_kernel, nb=nb),
        grid=(n // tm,),
        in_specs=[row(d), _mods_spec(tm, nb, rows_per_seq), _const_spec((1, d)),
                  _const_spec(w_perm.shape), _const_spec((1, 768)), _const_spec((1, 256)),
                  _const_spec(wuq_slot.shape)] + [tab_spec] * 6,
        out_specs=[row(1024), row(256), row(MLA_ROPE), row(RW_COLS)],
        out_shape=[jax.ShapeDtypeStruct((n, 1024), BF16), jax.ShapeDtypeStruct((n, 256), F32),
                   jax.ShapeDtypeStruct((n, MLA_ROPE), F32), jax.ShapeDtypeStruct((n, RW_COLS), F32)],
        compiler_params=_cparams(("parallel",)),
    )(x, mods, norm_g.reshape(1, d), w_perm, q_norm.reshape(1, 768), kv_norm.reshape(1, 256),
      wuq_slot, *tabs)


def _rope_tables(pos, tile_to):
    half = MLA_ROPE // 2
    freqs = ROPE_BASE ** (-jnp.arange(half, dtype=F32) / half)
    ang = pos.astype(F32)[:, None] * freqs[None, :]
    cos, sin = jnp.cos(ang), jnp.sin(ang)
    t = pos.shape[0]
    z = lambda w: jnp.zeros((t, w), F32)
    o = lambda w: jnp.ones((t, w), F32)
    cq = jnp.concatenate([o(64), cos, cos, z(32)], axis=1)
    s1q = jnp.concatenate([z(64), -sin, z(48)], axis=1)
    s2q = jnp.concatenate([z(80), sin, z(32)], axis=1)
    ck = jnp.concatenate([cos, cos, z(96)], axis=1)
    s1k = jnp.concatenate([-sin, z(112)], axis=1)
    s2k = jnp.concatenate([z(16), sin, z(96)], axis=1)
    tabs = [cq, s1q, s2q, ck, s1k, s2k]
    if tile_to > t:
        tabs = [jnp.tile(a, (tile_to // t, 1)) for a in tabs]
    return tabs


def _kv_expand_kernel(ckv_ref, kr_ref, wk_ref, sel_ref, wv_ref, one_ref, k_out, v_out):
    c = ckv_ref[...].astype(BF16)
    k = _dot(c, wk_ref[...]) + _dot(kr_ref[...].astype(BF16), sel_ref[...])
    k_out[...] = k.astype(BF16)
    v_out[0] = (_dot_nt(wv_ref[...], c) + one_ref[...]).astype(BF16)


def _kv_expand(ckv, kr, wk_slot, sel, wvt_slot, one_col, batch):
    n = ckv.shape[0]
    t_k = n // batch
    tm = 1024 if t_k % 1024 == 0 else 512
    assert t_k % tm == 0
    per = t_k // tm
    row = lambda w: pl.BlockSpec((tm, w), lambda i: (i, 0))
    return pl.pallas_call(
        _kv_expand_kernel,
        grid=(n // tm,),
        in_specs=[row(256), row(MLA_ROPE), _const_spec(wk_slot.shape), _const_spec(sel.shape),
                  _const_spec(wvt_slot.shape), _const_spec(one_col.shape)],
        out_specs=[row(1024), pl.BlockSpec((1, 1024, tm), lambda i: (i // per, 0, i % per))],
        out_shape=[jax.ShapeDtypeStruct((n, 1024), BF16), jax.ShapeDtypeStruct((batch, 1024, t_k), BF16)],
        compiler_params=_cparams(("parallel",)),
    )(ckv, kr, wk_slot, sel, wvt_slot, one_col)


def _mla_attn_kernel(qi_ref, ki_ref, fl_ref, q_ref, k_ref, vt_ref, o_ref, m_scr, acc_scr,
                     *, tq, tk, q_off):
    p_id = pl.program_id(1)
    flags = fl_ref[p_id]
    first = (flags & 1) != 0
    last = (flags & 2) != 0
    masked = (flags & 4) != 0

    @pl.when(first)
    def _():
        m_scr[...] = jnp.full(m_scr.shape, NEG, F32)
        acc_scr[...] = jnp.zeros(acc_scr.shape, F32)

    qb = min(MLA_QUERY_BLOCK, tq)

    def scores_t(h, c0):
        sl = slice(h * LANES, (h + 1) * LANES)
        return _dot_nt(k_ref[0, :, sl], q_ref[0, c0:c0 + qb, sl])

    def body(use_mask, first_query=0):
        insts = [(h, c0) for h in range(MLA_HEADS) for c0 in range(first_query, tq, qb)]
        if use_mask:
            q_chunk0 = (q_off + qi_ref[p_id] * tq) // CHUNK
            k_chunk0 = (ki_ref[p_id] * tk) // CHUNK
            kc = k_chunk0 + lax.broadcasted_iota(jnp.int32, (tk, qb), 0) // CHUNK
            qc_local = lax.broadcasted_iota(jnp.int32, (tk, qb), 1) // CHUNK
        ahead = [scores_t(*insts[i]) for i in range(min(MLA_SCORES_AHEAD, len(insts)))]
        for idx, (h, c0) in enumerate(insts):
            s = ahead.pop(0)
            if idx + MLA_SCORES_AHEAD < len(insts):
                ahead.append(scores_t(*insts[idx + MLA_SCORES_AHEAD]))
            if use_mask:
                s = jnp.where(kc <= qc_local + (q_chunk0 + c0 // CHUNK), s, NEG)
            m_prev = m_scr[h:h + 1, c0:c0 + qb]
            m_new = jnp.maximum(m_prev, jnp.max(s, axis=0, keepdims=True))
            m_scr[h:h + 1, c0:c0 + qb] = m_new
            alpha = jnp.exp2(m_prev - m_new)
            p_t = jnp.exp2(s - m_new).astype(BF16)
            vt_h = vt_ref[0, h * LANES:(h + 1) * LANES, :]
            acc_scr[h, :, c0:c0 + qb] = alpha * acc_scr[h, :, c0:c0 + qb] + _dot(vt_h, p_t)

    diag = flags >> 3
    for j in range(max(tq // tk, 1)):
        @pl.when(masked & (diag == j))
        def _(j=j):
            body(True, first_query=j * tk)

    @pl.when(jnp.logical_not(masked))
    def _():
        body(False)

    @pl.when(last)
    def _():
        outs = []
        for h in range(MLA_HEADS):
            a = acc_scr[h]
            outs.append(a[0:MLA_V] / a[MLA_V:MLA_V + 1])
        o_ref[0] = jnp.concatenate(outs, axis=0).T.astype(BF16)


MLA_Q_TILE = 1024
MLA_KV_TILE = 512
MLA_QUERY_BLOCK = 256
MLA_SCORES_AHEAD = 3


def _mla_attn(q, k, vt, pairs, tq, tk, q_off):
    b, t_q, _ = q.shape
    qi, ki, fl = pairs
    grid_spec = pltpu.PrefetchScalarGridSpec(
        num_scalar_prefetch=3,
        grid=(b, qi.shape[0]),
        in_specs=[pl.BlockSpec((1, tq, 1024), lambda bb, p, qi, ki, fl: (bb, qi[p], 0)),
                  pl.BlockSpec((1, tk, 1024), lambda bb, p, qi, ki, fl: (bb, ki[p], 0)),
                  pl.BlockSpec((1, 1024, tk), lambda bb, p, qi, ki, fl: (bb, 0, ki[p]))],
        out_specs=pl.BlockSpec((1, tq, 512), lambda bb, p, qi, ki, fl: (bb, qi[p], 0)),
        scratch_shapes=[pltpu.VMEM((MLA_HEADS, tq), F32),
                        pltpu.VMEM((MLA_HEADS, LANES, tq), F32)])
    return pl.pallas_call(
        functools.partial(_mla_attn_kernel, tq=tq, tk=tk, q_off=q_off),
        grid_spec=grid_spec,
        out_shape=jax.ShapeDtypeStruct((b, t_q, 512), BF16),
        compiler_params=_cparams(("parallel", "arbitrary")),
    )(qi, ki, fl, q, k, vt)


def _causal_pairs(nq, ratio):
    qi, ki, fl = [], [], []
    for a in range(nq):
        n_kv = (a + 1) * ratio
        for c in range(n_kv):
            qi.append(a)
            ki.append(c)
            diag = max(c - a * ratio, 0)
            fl.append((1 if c == 0 else 0) | (2 if c == n_kv - 1 else 0) | (4 if c >= a * ratio else 0)
                      | (diag << 3))
    return tuple(jnp.asarray(np.array(z, np.int32)) for z in (qi, ki, fl))


def _mla_decode_kernel(q_ref, cc_ref, kc_ref, cn_ref, kn_ref, wka_ref, prope_ref, wvp_ref, o_ref,
                       qa_scr, qr_scr, m_scr, l_scr, acc_scr, *, n_cache_blocks, t):
    k_id = pl.program_id(1)

    @pl.when(k_id == 0)
    def _():
        for h in range(MLA_HEADS):
            qs = q_ref[0, :, h * LANES:(h + 1) * LANES]
            qa_scr[h * t:(h + 1) * t, :] = _dot(qs, wka_ref[h]).astype(BF16)
            qr_scr[h * t:(h + 1) * t, :] = _dot(qs, prope_ref[...]).astype(BF16)
        m_scr[...] = jnp.full(m_scr.shape, NEG, F32)
        l_scr[...] = jnp.zeros(l_scr.shape, F32)
        acc_scr[...] = jnp.zeros(acc_scr.shape, F32)

    def update(ckv, kr):
        cb = ckv.astype(BF16)
        s = _dot_nt(qa_scr[...], cb) + _dot_nt(qr_scr[:, 0:MLA_ROPE], kr.astype(BF16))
        m_prev = m_scr[...]
        m_new = jnp.maximum(m_prev, jnp.max(s, axis=-1, keepdims=True))
        alpha = jnp.exp2(m_prev - m_new)
        p = jnp.exp2(s - m_new[:, 0:1])
        l_scr[...] = alpha * l_scr[...] + jnp.sum(p, axis=-1, keepdims=True)
        m_scr[...] = m_new
        acc_scr[...] = jnp.concatenate([alpha, alpha], axis=1) * acc_scr[...] + _dot(p.astype(BF16), cb)

    @pl.when(k_id < n_cache_blocks)
    def _():
        update(cc_ref[0], kc_ref[0])

    @pl.when(k_id == n_cache_blocks)
    def _():
        update(cn_ref[0], kn_ref[0])
        l = l_scr[...]
        o_lat = (acc_scr[...] / jnp.concatenate([l, l], axis=1)).astype(BF16)
        for pr in range(MLA_HEADS // 2):
            oe = o_lat[(2 * pr) * t:(2 * pr + 1) * t]
            oo = o_lat[(2 * pr + 1) * t:(2 * pr + 2) * t]
            o_ref[0, :, pr * LANES:(pr + 1) * LANES] = (
                _dot(oe, wvp_ref[2 * pr]) + _dot(oo, wvp_ref[2 * pr + 1])).astype(BF16)


MLA_DECODE_KV_TILE = 2048


def _mla_decode(q, ckv_cache, kr_cache, layer, ckv_new, kr_new, wka, prope, wvp):
    b, t, _ = q.shape
    n_past = ckv_cache.shape[2]
    kb = math.gcd(n_past, MLA_DECODE_KV_TILE)
    ncb = n_past // kb
    rows = MLA_HEADS * t
    per_b = lambda shape: pl.BlockSpec((1,) + shape, lambda bb, k: (bb, 0, 0))
    cache = lambda w: pl.BlockSpec((None, 1, kb, w), lambda bb, k: (layer, bb, jnp.minimum(k, ncb - 1), 0))
    return pl.pallas_call(
        functools.partial(_mla_decode_kernel, n_cache_blocks=ncb, t=t),
        grid=(b, ncb + 1),
        in_specs=[per_b((t, 1024)), cache(256), cache(MLA_ROPE), per_b((t, 256)), per_b((t, MLA_ROPE)),
                  _const_spec(wka.shape), _const_spec(prope.shape), _const_spec(wvp.shape)],
        out_specs=per_b((t, 512)),
        out_shape=jax.ShapeDtypeStruct((b, t, 512), BF16),
        scratch_shapes=[pltpu.VMEM((rows, 256), BF16), pltpu.VMEM((rows, LANES), BF16),
                        pltpu.VMEM((rows, LANES), F32), pltpu.VMEM((rows, LANES), F32),
                        pltpu.VMEM((rows, 256), F32)],
        compiler_params=_cparams(("parallel", "arbitrary")),
    )(q, ckv_cache, kr_cache, ckv_new, kr_new, wka, prope, wvp)


RW_CHUNK_GROUP = 2
RW_TIME_TILE = 512


def _pair_sum(x, ones_bd):
    hi, lo = _split2(x)
    return _dot(hi, ones_bd) + _dot(lo, ones_bd)


def _rwkv_kernel(pr_ref, sh0_ref, s0_ref, mu_ref, w0_ref, wl_ref, a0_ref, al_ref, g2_ref,
                 kk_ref, ka_ref, rk_ref, lnw_ref, lnb_ref, y_ref, st_ref,
                 prev_scr, s_scr, rt_scr, at_scr, bt_scr, kt_scr, bv_scr, k2_scr, v_scr,
                 cum_scr, yc_scr, *, tt):
    t_id = pl.program_id(1)
    n_pairs = RW_HEADS // 2

    @pl.when(t_id == 0)
    def _():
        s_scr[...] = s0_ref[0]
        prev_scr[0:1, :] = sh0_ref[0]

    n_chunks = tt // CHUNK
    groups = [list(range(c0, min(c0 + RW_CHUNK_GROUP, n_chunks))) for c0 in range(0, n_chunks, RW_CHUNK_GROUP)]
    gate, bonus = {}, {}

    carried_prev = prev_scr[0:1, :]
    prev_scr[0:1, :] = pr_ref[0, tt - 1:tt, :]

    li = lax.broadcasted_iota(jnp.int32, (LANES, LANES), 0)
    lj = lax.broadcasted_iota(jnp.int32, (LANES, LANES), 1)
    ones_bd = jnp.where((li // RW_N) == (lj // RW_N), 1.0, 0.0).astype(BF16)

    def head_sum(x):
        return jnp.concatenate([_pair_sum(x[:, p * LANES:(p + 1) * LANES], ones_bd)
                                for p in range(n_pairs)], axis=1)

    def prep_phase(gi):
        r0 = groups[gi][0] * CHUNK
        gr = len(groups[gi]) * CHUNK
        rows = slice(r0, r0 + gr)
        pr = pr_ref[0, rows, :]
        before = carried_prev if gi == 0 else pr_ref[0, r0 - 1:r0, :]
        row = lax.broadcasted_iota(jnp.int32, (gr, 1), 0)
        prev = jnp.where(row == 0, before, pltpu.roll(pr, 1, axis=0))
        pm = pr + (prev - pr) * mu_ref[...]
        r = pm[:, 0:512]
        k = pm[:, 512:1024]
        v = pm[:, 1024:1536]
        wa = pm[:, 1536:1664]
        g_in = pm[:, 1664:1792]
        yield
        z = w0_ref[...] + _dot(jnp.tanh(wa).astype(BF16), wl_ref[...])
        nz = -z
        w = -(jnp.maximum(nz, 0.0) + jnp.log(1.0 + jnp.exp(-jnp.abs(nz)))) - 0.5
        ld = -jnp.exp(w)
        yield
        a_sig = jax.nn.sigmoid(a0_ref[...] + _dot(wa.astype(BF16), al_ref[...]))
        gate[gi] = _dot(jax.nn.sigmoid(g_in).astype(BF16), g2_ref[...])
        yield
        kk = k * kk_ref[...]
        kk = kk * lax.rsqrt(jnp.maximum(head_sum(kk * kk), 1e-24))
        k2 = k * (1.0 + (a_sig - 1.0) * ka_ref[...])
        yield
        bonus[gi] = head_sum(r * k2 * rk_ref[...]) * v
        yield
        ti = lax.broadcasted_iota(jnp.int32, (gr, gr), 0)
        tj = lax.broadcasted_iota(jnp.int32, (gr, gr), 1)
        tri = jnp.where(((ti // CHUNK) == (tj // CHUNK)) & (tj <= ti), 1.0, 0.0).astype(BF16)
        l1 = ld.astype(BF16)
        rem = ld - l1.astype(F32)
        l2 = rem.astype(BF16)
        l3 = (rem - l2.astype(F32)).astype(BF16)
        cum = (_dot(tri, l3) + _dot(tri, l2)) + _dot(tri, l1)
        yield
        winv = jnp.exp(-cum)
        rt_scr[rows, :] = r * jnp.exp(cum)
        at_scr[rows, :] = -kk * jnp.exp(cum - ld)
        yield
        bv = kk * a_sig
        bt_scr[rows, :] = bv * winv
        kt_scr[rows, :] = k2 * winv
        bv_scr[rows, :] = bv
        k2_scr[rows, :] = k2
        v_scr[rows, :] = v
        cum_scr[rows, :] = cum

    lane = lax.broadcasted_iota(jnp.int32, (CHUNK, LANES), 1)
    even = lane < RW_N
    strict = (lj % RW_N) < (li % RW_N)
    incl = (lj % RW_N) <= (li % RW_N)
    eye = jnp.where(li == lj, 1.0, 0.0).astype(F32)
    pairs = range(n_pairs)

    def same_block(m):
        return (li // m) == (lj // m)

    def stack_f32(x):
        return jnp.concatenate([jnp.where(even, x, 0.0), jnp.where(even, 0.0, x)], axis=0)

    def stack(x):
        return stack_f32(x).astype(BF16)

    mm = lambda a, b: _dot(a.astype(BF16), b.astype(BF16))
    rows_of = lambda c: slice(c * CHUNK, (c + 1) * CHUNK)
    lanes_of = lambda p: slice(p * LANES, (p + 1) * LANES)

    def independent_phase(chunks, res):
        insts = [(c, p) for c in chunks for p in pairs]
        load = lambda scr: [scr[rows_of(c), lanes_of(p)] for c, p in insts]
        cum_c = load(cum_scr)
        cum_l = [a[CHUNK - 1:CHUNK, :] for a in cum_c]
        w2 = [jnp.exp(a - b) for a, b in zip(cum_l, cum_c)]
        rs = [stack(a) for a in load(rt_scr)]
        as_ = [stack(a) for a in load(at_scr)]
        bs = [stack(a) for a in load(bt_scr)]
        ks = [stack(a) for a in load(kt_scr)]
        v_f = [stack_f32(a) for a in load(v_scr)]
        vs = [a.astype(BF16) for a in v_f]
        b2s = [stack(a * w) for a, w in zip(load(bv_scr), w2)]
        k2s = [stack(a * w) for a, w in zip(load(k2_scr), w2)]
        yield
        n_m = [jnp.where(strict, _dot_nt(a, b), 0.0) for a, b in zip(as_, bs)]
        mk = [jnp.where(strict, _dot_nt(a, b), 0.0).astype(BF16) for a, b in zip(as_, ks)]
        cb = [jnp.where(incl, _dot_nt(a, b), 0.0).astype(BF16) for a, b in zip(rs, bs)]
        ck = [jnp.where(incl, _dot_nt(a, b), 0.0).astype(BF16) for a, b in zip(rs, ks)]
        yield
        n8 = [jnp.where(same_block(8), n, 0.0) for n in n_m]
        t = [eye + a for a in n8]
        p2 = [mm(a, a) for a in n8]
        mv = [_dot(a, b) for a, b in zip(mk, vs)]
        yield
        t = [a + mm(a, b) for a, b in zip(t, p2)]
        p4 = [mm(a, a) for a in p2]
        cv = [_dot(a, b) for a, b in zip(ck, vs)]
        yield
        t = [a + mm(a, b) for a, b in zip(t, p4)]
        vk = [_dot(a.T.astype(BF16), b) for a, b in zip(v_f, k2s)]
        yield
        for m in (8, 16, 32):
            off = same_block(2 * m) & jnp.logical_not(same_block(m))
            x = [mm(jnp.where(off, n, 0.0), a) for n, a in zip(n_m, t)]
            yield
            t = [a + mm(a, b) for a, b in zip(t, x)]
            yield
        for i, key in enumerate(insts):
            res[key] = dict(as_=as_[i], rs=rs[i], b2s=b2s[i], cb=cb[i], mv=mv[i], cv=cv[i], vk=vk[i],
                            tinv=t[i].astype(BF16), wl=jnp.exp(cum_l[i]))

    def dependent_phase(chunks, res):
        for c in chunks:
            rc = [res[(c, p)] for p in pairs]
            s = [s_scr[p] for p in pairs]
            sb = [a.astype(BF16) for a in s]
            x = [_dot_nt(r_['as_'], b) + r_['mv'] for r_, b in zip(rc, sb)]
            rs_s = [_dot_nt(r_['rs'], b) + r_['cv'] for r_, b in zip(rc, sb)]
            yield
            u = [_dot(r_['tinv'], a.astype(BF16)) for r_, a in zip(rc, x)]
            yield
            ys = [a + _dot(r_['cb'], b.astype(BF16)) for a, r_, b in zip(rs_s, rc, u)]
            for p in pairs:
                yc_scr[rows_of(c), lanes_of(p)] = ys[p][0:CHUNK] + ys[p][CHUNK:2 * CHUNK]
                s_scr[p] = s[p] * rc[p]['wl'] + _dot(u[p].T.astype(BF16), rc[p]['b2s']) + rc[p]['vk']
            yield

    def emit(*gens):
        gens = list(gens)
        while gens:
            for gen in list(gens):
                try:
                    next(gen)
                except StopIteration:
                    gens.remove(gen)

    def output_phase(gi):
        r0 = groups[gi][0] * CHUNK
        rows = slice(r0, r0 + len(groups[gi]) * CHUNK)
        y = yc_scr[rows, :]
        mean = head_sum(y) * (1.0 / RW_N)
        dlt = y - mean
        yield
        var = head_sum(dlt * dlt) * (1.0 / RW_N)
        yield
        yn = dlt * lax.rsqrt(var + RW_GN_EPS) * lnw_ref[...] + lnb_ref[...]
        y_ref[0, rows, :] = ((yn + bonus[gi]) * gate[gi]).astype(BF16)

    res = {}
    n_g = len(groups)
    for step in range(n_g + 3):
        live = []
        if step < n_g:
            live.append(prep_phase(step))
        if 0 <= step - 1 < n_g:
            live.append(independent_phase(groups[step - 1], res))
        if 0 <= step - 2 < n_g:
            live.append(dependent_phase(groups[step - 2], res))
        if 0 <= step - 3 < n_g:
            live.append(output_phase(step - 3))
        emit(*live)

    @pl.when(t_id == pl.num_programs(1) - 1)
    def _():
        st_ref[0] = s_scr[...]


def _rwkv(prw, sh0, s0_bd, wts):
    b, t, _ = prw.shape
    tt = min(RW_TIME_TILE, t)
    assert t % tt == 0 and tt % CHUNK == 0
    c512 = _const_spec((1, RW_C))
    scr = lambda: pltpu.VMEM((tt, RW_C), F32)
    return pl.pallas_call(
        functools.partial(_rwkv_kernel, tt=tt),
        grid=(b, t // tt),
        in_specs=[pl.BlockSpec((1, tt, RW_COLS), lambda bb, i: (bb, i, 0)),
                  pl.BlockSpec((1, 1, RW_COLS), lambda bb, i: (bb, 0, 0)),
                  pl.BlockSpec((1, 4, LANES, LANES), lambda bb, i: (bb, 0, 0, 0)),
                  _const_spec((1, RW_COLS)), c512, _const_spec((LANES, RW_C)), c512,
                  _const_spec((LANES, RW_C)), _const_spec((LANES, RW_C)),
                  c512, c512, c512, c512, c512],
        out_specs=[pl.BlockSpec((1, tt, RW_C), lambda bb, i: (bb, i, 0)),
                   pl.BlockSpec((1, 4, LANES, LANES), lambda bb, i: (bb, 0, 0, 0))],
        out_shape=[jax.ShapeDtypeStruct((b, t, RW_C), BF16),
                   jax.ShapeDtypeStruct((b, 4, LANES, LANES), F32)],
        scratch_shapes=[pltpu.VMEM((8, RW_COLS), F32), pltpu.VMEM((4, LANES, LANES), F32)]
                       + [scr() for _ in range(9)],
        compiler_params=_cparams(("parallel", "arbitrary")),
    )(prw, sh0, s0_bd, *wts)


def _state_to_bd(s):
    b = s.shape[0]
    s = s.reshape(b, 4, 2, RW_N, RW_N)
    z = jnp.zeros_like(s[:, :, 0])
    top = jnp.concatenate([s[:, :, 0], z], axis=-1)
    bot = jnp.concatenate([z, s[:, :, 1]], axis=-1)
    return jnp.concatenate([top, bot], axis=-2)


def _state_from_bd(s):
    b = s.shape[0]
    return jnp.stack([s[:, :, :RW_N, :RW_N], s[:, :, RW_N:, RW_N:]], axis=2).reshape(b, RW_HEADS, RW_N, RW_N)


_ODD_SEGS = ([(1024 + g * 64, 1088 + g * 64) for g in range(SW_KV_HEADS) for _ in range(2)]
             + [(1280 + g * 64, 1344 + g * 64) for g in range(SW_KV_HEADS) for _ in range(2)])


def _odd_in_kernel(x_ref, mods_ref, ng_ref, w_ref, b_ref, q_out, k_out, v_out, *, nb):
    h = _norm_mod(x_ref[...], ng_ref[...], mods_ref[:, 3, :], mods_ref[:, 4, :], nb).astype(BF16)
    q = _dot(h, w_ref[:, 0:1024]) + b_ref[:, 0:1024]
    q_out[...] = (q * (SW_HD ** -0.5 * LOG2E)).astype(BF16)
    k_out[...] = _dot(h, w_ref[:, 1024:1536]) + b_ref[:, 1024:1536]
    v_out[...] = _dot(h, w_ref[:, 1536:2048]) + b_ref[:, 1536:2048]


def _odd_in(x, mods, norm_g, w_perm, b_perm, rows_per_seq):
    n, d = x.shape
    tm, nb = _row_tiling(n, rows_per_seq, 512)
    row = lambda w: pl.BlockSpec((tm, w), lambda i: (i, 0))
    return pl.pallas_call(
        functools.partial(_odd_in_kernel, nb=nb),
        grid=(n // tm,),
        in_specs=[row(d), _mods_spec(tm, nb, rows_per_seq), _const_spec((1, d)),
                  _const_spec(w_perm.shape), _const_spec((1, 2048))],
        out_specs=[row(1024), row(512), row(512)],
        out_shape=[jax.ShapeDtypeStruct((n, 1024), BF16), jax.ShapeDtypeStruct((n, 512), F32),
                   jax.ShapeDtypeStruct((n, 512), F32)],
        compiler_params=_cparams(("parallel",)),
    )(x, mods, norm_g.reshape(1, d), w_perm, b_perm.reshape(1, 2048))


def _swa_kernel(sinks_ref, q_ref, kp_ref, kc_ref, vp_ref, vc_ref, o_ref, *, tq, mask_first_prev):
    nk = WINDOW + tq
    qi = lax.broadcasted_iota(jnp.int32, (tq, nk), 0)
    kj = lax.broadcasted_iota(jnp.int32, (tq, nk), 1)
    kc = kj // CHUNK - WINDOW // CHUNK
    qc = qi // CHUNK
    vis = (kc <= qc) & (kc >= qc - WINDOW // CHUNK)
    if mask_first_prev:
        vis = vis & ((kj >= WINDOW) | (pl.program_id(1) > 0))
    ndist = jnp.where(vis, -jnp.abs(qi + WINDOW - kj).astype(F32), NEG)
    keys = jnp.concatenate([kp_ref[0], kc_ref[0]], axis=0).astype(BF16)
    vals = jnp.concatenate([vp_ref[0], vc_ref[0]], axis=0)
    klane = lax.broadcasted_iota(jnp.int32, (nk, LANES), 1)
    v_slots = [jnp.where(klane < SW_HD, vals[:, g * LANES:(g + 1) * LANES], 1.0).astype(BF16)
               for g in range(SW_KV_HEADS)]
    lane = lax.broadcasted_iota(jnp.int32, (tq, LANES), 1)
    low = lane < SW_HD
    sb = min(SWA_SUB_BLOCK, tq)

    def scores(h):
        qp = q_ref[0, :, (h // 2) * LANES:(h // 2 + 1) * LANES]
        qh = jnp.where(low if h % 2 == 0 else jnp.logical_not(low), qp, jnp.zeros_like(qp))
        return _dot_nt(qh, keys[:, (h // SW_GROUP) * LANES:(h // SW_GROUP + 1) * LANES])

    outs = []
    ahead = [scores(h) for h in range(SWA_SCORES_AHEAD)]
    for h in range(SW_HEADS):
        s = ahead.pop(0)
        if h + SWA_SCORES_AHEAD < SW_HEADS:
            ahead.append(scores(h + SWA_SCORES_AHEAD))
        slope = (2.0 ** (-8.0 * (h + 1) / SW_HEADS)) * LOG2E
        sk = sinks_ref[h] * LOG2E
        es, ms = [], []
        for r1 in range(0, tq, sb):
            z = s[r1:r1 + sb] + slope * ndist[r1:r1 + sb]
            m = jnp.maximum(jnp.broadcast_to(jnp.max(z, axis=-1, keepdims=True), (sb, LANES)), sk)
            m_wide = jnp.concatenate([m] * (nk // LANES), axis=1) if nk % LANES == 0 else m[:, 0:1]
            es.append(jnp.exp2(z - m_wide).astype(BF16))
            ms.append(m)
        e = jnp.concatenate(es, axis=0) if len(es) > 1 else es[0]
        m = jnp.concatenate(ms, axis=0) if len(ms) > 1 else ms[0]
        pv = _dot(e, v_slots[h // SW_GROUP])
        outs.append(pv / (pltpu.roll(pv, SW_HD, axis=1) + jnp.exp2(sk - m)))
    for pr in range(SW_HEADS // 2):
        o_ref[0, :, pr * LANES:(pr + 1) * LANES] = jnp.where(
            low, outs[2 * pr], pltpu.roll(outs[2 * pr + 1], SW_HD, axis=1)).astype(BF16)


SWA_SUB_BLOCK = 32
SWA_SCORES_AHEAD = 2


def _swa(q, k_prev, k_cur, v_prev, v_cur, sinks, tq, same_array):
    b, t, _ = q.shape
    nt = t // tq
    per = tq // WINDOW
    if same_array:
        prev_map = lambda bb, i: (bb, jnp.maximum(i * per - 1, 0), 0)
    else:
        prev_map = lambda bb, i: (bb, 0, 0)
    cur = lambda w: pl.BlockSpec((1, tq, w), lambda bb, i: (bb, i, 0))
    prev = pl.BlockSpec((1, WINDOW, 512), prev_map)
    return pl.pallas_call(
        functools.partial(_swa_kernel, tq=tq, mask_first_prev=same_array),
        grid=(b, nt),
        in_specs=[pl.BlockSpec(memory_space=pltpu.SMEM), cur(1024), prev, cur(512), prev, cur(512)],
        out_specs=cur(1024),
        out_shape=jax.ShapeDtypeStruct((b, t, 1024), BF16),
        compiler_params=_cparams(("parallel", "parallel")),
    )(sinks, q, k_prev, k_cur, v_prev, v_cur)


def _undup(a):
    return a.reshape(a.shape[:-1] + (SW_KV_HEADS, 2, SW_HD))[..., 0, :]


def _dup(a):
    return jnp.concatenate([a, a], axis=-1).reshape(a.shape[:-2] + (512,))


def _prep_weights(p):
    depth = p['w_ada'].shape[0]
    n_even, n_odd = (depth + 1) // 2, depth // 2
    w = {}
    w['ffn_in'] = p['ffn_w_in'].astype(BF16)
    w['ffn_out'] = p['ffn_w_out'].astype(BF16)
    wi = p['even_w_in'].astype(BF16)
    w['even_in'] = jnp.concatenate(
        [wi[:, :, 0:1024]] + [wi[:, :, _RW_OFF + a:_RW_OFF + b] for a, b in _RW_SEGS]
        + [wi[:, :, 1024:1056], jnp.zeros((n_even, 1024, 96), BF16)], axis=2)
    w['wuq'] = jnp.pad(p['mla_w_uq'], ((0, 0), (0, 0), (0, 0), (0, 32))).reshape(n_even, 768, 1024).astype(BF16)
    w['wk'] = jnp.pad(p['mla_w_ukv'][..., :MLA_NOPE], ((0, 0), (0, 0), (0, 0), (0, 64))
                      ).reshape(n_even, 256, 1024).astype(BF16)
    w['wv'] = jnp.swapaxes(jnp.pad(p['mla_w_ukv'][..., MLA_NOPE:], ((0, 0), (0, 0), (0, 0), (0, 64))
                                   ).reshape(n_even, 256, 1024), 1, 2).astype(BF16)
    sel = np.zeros((MLA_ROPE, 1024), np.float32)
    one = np.zeros((1024, 1), np.float32)
    for h in range(MLA_HEADS):
        sel[np.arange(MLA_ROPE), h * LANES + MLA_NOPE + np.arange(MLA_ROPE)] = 1.0
        one[h * LANES + MLA_V, 0] = 1.0
    w['sel'] = jnp.asarray(sel).astype(BF16)
    w['one'] = jnp.asarray(one)
    wk_t = jnp.transpose(p['mla_w_ukv'][..., :MLA_NOPE], (0, 2, 3, 1))
    w['wka'] = jnp.pad(wk_t, ((0, 0), (0, 0), (0, 64), (0, 0))).astype(BF16)
    prope = np.zeros((LANES, LANES), np.float32)
    prope[MLA_NOPE + np.arange(MLA_ROPE), np.arange(MLA_ROPE)] = 1.0
    w['prope'] = jnp.asarray(prope).astype(BF16)
    wv_h = jnp.transpose(p['mla_w_ukv'][..., MLA_NOPE:], (0, 2, 1, 3))
    wv_even = jnp.pad(wv_h[:, 0::2], ((0, 0), (0, 0), (0, 0), (0, 64)))
    wv_odd = jnp.pad(wv_h[:, 1::2], ((0, 0), (0, 0), (0, 0), (64, 0)))
    w['wvp'] = jnp.stack([wv_even, wv_odd], axis=2).reshape(n_even, MLA_HEADS, 256, LANES).astype(BF16)
    z64 = jnp.zeros((n_even, 64, RW_C), F32)
    w['wl'] = jnp.concatenate([p['rw_w2'], z64], axis=1).astype(BF16)
    w['al'] = jnp.concatenate([z64, p['rw_a2']], axis=1).astype(BF16)
    w['g2'] = p['rw_g2'].astype(BF16)
    w['mu'] = _rw_permute(p['rw_mu'])
    w['even_out'] = p['even_w_out'].astype(BF16)
    wo = p['odd_w_qkv'].astype(BF16)
    w['odd_in'] = jnp.concatenate([wo[:, :, 0:1024]] + [wo[:, :, a:b] for a, b in _ODD_SEGS], axis=2)
    bo = p['odd_b_qkv']
    w['odd_b'] = jnp.concatenate([bo[:, 0:1024]] + [bo[:, a:b] for a, b in _ODD_SEGS], axis=1)
    w['odd_out'] = p['odd_w_out'].astype(BF16)
    return w


def _trunk(x3, mods_all, start, past, p, w):
    b, t, d = x3.shape
    n = b * t
    depth = mods_all.shape[0]
    x = x3.reshape(n, d)
    rows = t
    tm_even, _ = _row_tiling(n, rows, 512)
    pos = start + jnp.arange(t)
    tabs = _rope_tables(pos, tm_even)
    even_states, odd_states = [], []
    for i in range(depth):
        mods = mods_all[i]
        j = i // 2
        x = _ffn(x, mods, p['norm_g'][i, 0], w['ffn_in'], w['ffn_out'], i, 0, 0, rows)
        if i % 2 == 0:
            q, ckv, kr, prw = _even_in(x, mods, p['norm_g'][i, 1], w['even_in'][j], p['mla_q_norm'][j],
                                       p['mla_kv_norm'][j], w['wuq'][j], tabs, rows)
            if past is None:
                kx, vtx = _kv_expand(ckv, kr, w['wk'][j], w['sel'], w['wv'][j], w['one'], b)
                tk = min(MLA_KV_TILE, t)
                tq = min(MLA_Q_TILE, t)
                att = _mla_attn(q.reshape(b, t, 1024), kx.reshape(b, t, 1024), vtx,
                                _causal_pairs(t // tq, tq // tk), tq, tk, 0)
                s0 = jnp.zeros((b, RW_HEADS, RW_N, RW_N), F32)
                sh0 = jnp.zeros((b, RW_COLS), F32)
            else:
                s0, sh0 = past[2][j], past[3][j]
                att = _mla_decode(q.reshape(b, t, 1024), past[0], past[1], j, ckv.reshape(b, t, 256),
                                  kr.reshape(b, t, MLA_ROPE), w['wka'][j], w['prope'], w['wvp'][j])
            rw_wts = (w['mu'][j].reshape(1, RW_COLS), p['rw_w0'][j].reshape(1, RW_C), w['wl'][j],
                      p['rw_a0'][j].reshape(1, RW_C), w['al'][j], w['g2'][j],
                      p['rw_k_k'][j].reshape(1, RW_C), p['rw_k_a'][j].reshape(1, RW_C),
                      p['rw_r_k'][j].reshape(1, RW_C), p['rw_ln_w'][j].reshape(1, RW_C),
                      p['rw_ln_b'][j].reshape(1, RW_C))
            prw3 = prw.reshape(b, t, RW_COLS)
            y_rw, s_bd = _rwkv(prw3, _rw_permute(sh0).reshape(b, 1, RW_COLS), _state_to_bd(s0), rw_wts)
            mix = ((att.reshape(n, 512), w['even_out'][j][:512]), (y_rw.reshape(n, RW_C), w['even_out'][j][512:]))
            even_states.append((ckv.reshape(b, t, 256), kr.reshape(b, t, MLA_ROPE), _state_from_bd(s_bd),
                                _rw_unpermute(prw3[:, t - 1, :])))
        else:
            q, kd, vd = _odd_in(x, mods, p['norm_g'][i, 1], w['odd_in'][j], w['odd_b'][j], rows)
            q3, kd3, vd3 = q.reshape(b, t, 1024), kd.reshape(b, t, 512), vd.reshape(b, t, 512)
            if past is None:
                tq = min(256, t)
                o = _swa(q3, kd3, kd3, vd3, vd3, p['swa_sinks'][j], tq, True)
                keep = min(WINDOW, t)
                k_new, v_new = _undup(kd3[:, t - keep:]), _undup(vd3[:, t - keep:])
            else:
                k_past, v_past = past[4][j], past[5][j]
                o = _swa(q3, _dup(k_past), kd3, _dup(v_past), vd3, p['swa_sinks'][j], t, False)
                k_new = jnp.concatenate([k_past, _undup(kd3)], axis=1)[:, t:]
                v_new = jnp.concatenate([v_past, _undup(vd3)], axis=1)[:, t:]
            mix = ((o.reshape(n, 1024), w['odd_out'][j]),)
            odd_states.append((k_new, v_new))
        fg = p['final_norm_g'] if i == depth - 1 else None
        x = _ffn(x, mods, p['norm_g'][i, 2], w['ffn_in'], w['ffn_out'], i, 1, 2, rows, final_g=fg, mix=mix)
    es = [jnp.stack([st[k] for st in even_states]) for k in range(4)]
    os_ = [jnp.stack([st[k] for st in odd_states]) for k in range(2)]
    return x.reshape(b, t, d), es + os_


def kernel(x_prompt, x_sample, cache_mla_ckv, cache_mla_krope, state_rwkv, state_rwkv_shift, cache_swa_k, cache_swa_v, c_prompt, c_sample, w_ada, b_ada, norm_g, ffn_w_in, ffn_w_out, even_w_in, even_w_out, mla_q_norm, mla_kv_norm, mla_w_uq, mla_w_ukv, rw_mu, rw_w0, rw_w2, rw_a0, rw_a2, rw_g2, rw_k_k, rw_k_a, rw_r_k, rw_ln_w, rw_ln_b, odd_w_qkv, odd_b_qkv, odd_w_out, swa_sinks, final_norm_g):
    p = dict(w_ada=w_ada, b_ada=b_ada, norm_g=norm_g, ffn_w_in=ffn_w_in, ffn_w_out=ffn_w_out,
             even_w_in=even_w_in, even_w_out=even_w_out, mla_q_norm=mla_q_norm, mla_kv_norm=mla_kv_norm,
             mla_w_uq=mla_w_uq, mla_w_ukv=mla_w_ukv, rw_mu=rw_mu, rw_w0=rw_w0, rw_w2=rw_w2, rw_a0=rw_a0,
             rw_a2=rw_a2, rw_g2=rw_g2, rw_k_k=rw_k_k, rw_k_a=rw_k_a, rw_r_k=rw_r_k, rw_ln_w=rw_ln_w,
             rw_ln_b=rw_ln_b, odd_w_qkv=odd_w_qkv, odd_b_qkv=odd_b_qkv, odd_w_out=odd_w_out,
             swa_sinks=swa_sinks, final_norm_g=final_norm_g)
    w = _prep_weights(p)
    depth = w_ada.shape[0]
    bp, bs = c_prompt.shape[0], c_sample.shape[0]
    d = c_prompt.shape[1]
    b_pad = -(-(bp + bs) // 8) * 8
    c_all = jnp.concatenate([c_prompt, c_sample, jnp.zeros((b_pad - bp - bs, d), F32)], axis=0)
    mods = _ada(c_all, w_ada, b_ada).reshape(depth, b_pad, 3 * N_SUB, d)
    y_prompt, sp = _trunk(x_prompt, mods[:, :bp], 0, None, p, w)
    past = (cache_mla_ckv, cache_mla_krope, state_rwkv, state_rwkv_shift, cache_swa_k, cache_swa_v)
    y_sample, ss = _trunk(x_sample, mods[:, bp:bp + bs], cache_mla_ckv.shape[2], past, p, w)
    return (y_prompt, y_sample, sp[0], sp[1], sp[2], sp[3], sp[4], sp[5],
            ss[0], ss[1], ss[2], ss[3], ss[4], ss[5])
```

```python
import functools
import math

import jax
import jax.numpy as jnp
import numpy as np
from jax import lax
from jax.experimental import pallas as pl
from jax.experimental.pallas import tpu as pltpu

F32 = jnp.float32
BF16 = jnp.bfloat16

CHUNK = 64
EPS = 1e-6
NEG = -1e30
N_SUB = 3
MLA_HEADS = 8
MLA_NOPE = 64
MLA_ROPE = 32
MLA_V = 64
MLA_Q_LORA = 768
MLA_KV_LORA = 256
MLA_SCALE = (MLA_NOPE + MLA_ROPE) ** -0.5
MLA_V_SLOT = 80
MLA_V_ROWS = MLA_HEADS * MLA_V_SLOT
LOG2E = math.log2(math.e)
ROPE_BASE = 10000.0
RW_HEADS = 8
RW_N = 64
RW_C = RW_HEADS * RW_N
RW_GN_EPS = 64e-5
RW_COLS = 3 * RW_C + 64 + 64 + 128
SW_HEADS = 16
SW_KV_HEADS = 4
SW_GROUP = 4
SW_HD = 64
WINDOW = 128

LANES = 128
VMEM_LIMIT = 56 * 1024 * 1024


def _cparams(sem):
    return pltpu.CompilerParams(dimension_semantics=sem, vmem_limit_bytes=VMEM_LIMIT)


def _const_spec(shape):
    nd = len(shape)
    return pl.BlockSpec(shape, lambda *_: (0,) * nd, pipeline_mode=pl.Buffered(1))


def _dot(a, b):
    return jnp.dot(a, b, preferred_element_type=F32)


def _dot_nt(a, b):
    return lax.dot_general(a, b, (((1,), (1,)), ((), ())), preferred_element_type=F32)


def _split2(x):
    hi = x.astype(BF16)
    lo = (x - hi.astype(F32)).astype(BF16)
    return hi, lo


def _norm_mod(x, g, sh, sc, nb):
    y = x * lax.rsqrt(jnp.mean(x * x, axis=-1, keepdims=True) + EPS) * g
    if nb == 1:
        return y * (1.0 + sc) + sh
    tm, d = x.shape
    y3 = y.reshape(nb, tm // nb, d)
    return (y3 * (1.0 + sc[:, None, :]) + sh[:, None, :]).reshape(tm, d)


def _gate_rows(g, y, nb):
    if nb == 1:
        return g * y
    tm, d = y.shape
    return (y.reshape(nb, tm // nb, d) * g[:, None, :]).reshape(tm, d)


def _row_tiling(n_rows, rows_per_seq, pref):
    if rows_per_seq >= pref:
        assert rows_per_seq % pref == 0
        return pref, 1
    tm = min(pref, n_rows)
    assert tm % rows_per_seq == 0 and n_rows % tm == 0
    return tm, tm // rows_per_seq


def _mods_spec(tm, nb, rows_per_seq):
    if nb == 1:
        tiles_per_seq = rows_per_seq // tm
        return pl.BlockSpec((1, 3 * N_SUB, 1024), lambda i: (i // tiles_per_seq, 0, 0))
    return pl.BlockSpec((nb, 3 * N_SUB, 1024), lambda i: (i, 0, 0))


def _ada_kernel(c_ref, w_ref, b_ref, o_ref):
    c = c_ref[...]
    cs = (c * jax.nn.sigmoid(c)).astype(BF16)
    o_ref[0] = _dot(cs, w_ref[0].astype(BF16)) + b_ref[0]


def _ada(c_all, w_ada, b_ada):
    depth, d, n = w_ada.shape
    bp = c_all.shape[0]
    tn = n // 4
    return pl.pallas_call(
        _ada_kernel,
        grid=(depth, n // tn),
        in_specs=[pl.BlockSpec((bp, d), lambda l, j: (0, 0)),
                  pl.BlockSpec((1, d, tn), lambda l, j: (l, 0, j)),
                  pl.BlockSpec((1, 1, tn), lambda l, j: (l, 0, j))],
        out_specs=pl.BlockSpec((1, bp, tn), lambda l, j: (l, 0, j)),
        out_shape=jax.ShapeDtypeStruct((depth, bp, n), F32),
        compiler_params=_cparams(("parallel", "parallel")),
    )(c_all, w_ada, b_ada.reshape(depth, 1, n))


FF_CHUNK = 256


def _ffn_kernel(x_ref, mods_ref, ng_ref, win_ref, wout_ref, *rest, sub, nb, d_ff, final, n_mix):
    mix_a, mix_w, rest = rest[:n_mix], rest[n_mix:2 * n_mix], rest[2 * n_mix:]
    if final:
        fg_ref, o_ref, a_scr = rest
    else:
        o_ref, a_scr = rest
    x = x_ref[...]
    if n_mix:
        ym = _dot(mix_a[0][...], mix_w[0][...])
        for a, w in zip(mix_a[1:], mix_w[1:]):
            ym = ym + _dot(a[...], w[...])
        x = x + _gate_rows(mods_ref[:, 5, :], ym, nb)
    sh = mods_ref[:, 3 * sub, :]
    sc = mods_ref[:, 3 * sub + 1, :]
    gt = mods_ref[:, 3 * sub + 2, :]
    h = _norm_mod(x, ng_ref[...], sh, sc, nb).astype(BF16)
    for c in range(d_ff // FF_CHUNK):
        lo = c * FF_CHUNK
        g = _dot(h, win_ref[:, lo:lo + FF_CHUNK])
        u = _dot(h, win_ref[:, d_ff + lo:d_ff + lo + FF_CHUNK])
        a_scr[:, lo:lo + FF_CHUNK] = (g * jax.nn.sigmoid(g) * u).astype(BF16)
    y = _dot(a_scr[...], wout_ref[...])
    out = x + _gate_rows(0.5 * gt, y, nb)
    if final:
        out = out * lax.rsqrt(jnp.mean(out * out, axis=-1, keepdims=True) + EPS) * fg_ref[...]
    o_ref[...] = out


def _ffn(x, mods, norm_g, w_in, w_out, layer, half, sub, rows_per_seq, final_g=None, mix=()):
    n, d = x.shape
    d_ff = w_out.shape[2]
    tm, nb = _row_tiling(n, rows_per_seq, 512)
    final = final_g is not None
    stacked = lambda a: pl.BlockSpec((None, None) + a.shape[2:], lambda i: (layer, half, 0, 0),
                                     pipeline_mode=pl.Buffered(1))
    in_specs = [pl.BlockSpec((tm, d), lambda i: (i, 0)),
                _mods_spec(tm, nb, rows_per_seq),
                _const_spec((1, d)), stacked(w_in), stacked(w_out)]
    args = [x, mods, norm_g.reshape(1, d), w_in, w_out]
    in_specs += [pl.BlockSpec((tm, a.shape[1]), lambda i: (i, 0)) for a, _ in mix]
    in_specs += [_const_spec(wm.shape) for _, wm in mix]
    args += [a for a, _ in mix] + [wm for _, wm in mix]
    if final:
        in_specs.append(_const_spec((1, d)))
        args.append(final_g.reshape(1, d))
    return pl.pallas_call(
        functools.partial(_ffn_kernel, sub=sub, nb=nb, d_ff=d_ff, final=final, n_mix=len(mix)),
        grid=(n // tm,),
        in_specs=in_specs,
        out_specs=pl.BlockSpec((tm, d), lambda i: (i, 0)),
        out_shape=jax.ShapeDtypeStruct((n, d), F32),
        scratch_shapes=[pltpu.VMEM((tm, d_ff), BF16)],
        compiler_params=_cparams(("parallel",)),
    )(*args)


_RW_OFF = 1056
_RW_SEGS = [(0, 512), (576, 1088), (1088, 1600), (512, 576), (1600, 1664), (1664, 1792)]
_RW_INV_SEGS = [(0, 512), (1536, 1600), (512, 1024), (1024, 1536), (1600, 1664), (1664, 1792)]


def _rw_permute(a):
    return jnp.concatenate([a[..., s:e] for s, e in _RW_SEGS], axis=-1)


def _rw_unpermute(a):
    return jnp.concatenate([a[..., s:e] for s, e in _RW_INV_SEGS], axis=-1)


def _rope_slot(v, c, s1, s2):
    w = v.shape[-1]
    return v * c + pltpu.roll(v, w - 16, axis=1) * s1 + pltpu.roll(v, 16, axis=1) * s2


def _even_in_kernel(x_ref, mods_ref, ng_ref, w_ref, qn_ref, kvn_ref, wuq_ref,
                    cq_ref, s1q_ref, s2q_ref, ck_ref, s1k_ref, s2k_ref,
                    q_out, ckv_out, kr_out, prw_out, *, nb):
    x = x_ref[...]
    h = _norm_mod(x, ng_ref[...], mods_ref[:, 3, :], mods_ref[:, 4, :], nb).astype(BF16)
    cq = _dot(h, w_ref[:, 0:768])
    cqn = (cq * lax.rsqrt(jnp.mean(cq * cq, axis=-1, keepdims=True) + EPS) * qn_ref[...]).astype(BF16)
    q = _dot(cqn, wuq_ref[...])
    rep = lambda t: jnp.concatenate([t] * MLA_HEADS, axis=1)
    q = _rope_slot(q, rep(cq_ref[...]), rep(s1q_ref[...]), rep(s2q_ref[...]))
    q_out[...] = (q * (MLA_SCALE * LOG2E)).astype(BF16)
    ckv = _dot(h, w_ref[:, 768:1024])
    ckv_out[...] = ckv * lax.rsqrt(jnp.mean(ckv * ckv, axis=-1, keepdims=True) + EPS) * kvn_ref[...]
    prw_out[...] = _dot(h, w_ref[:, 1024:2816])
    krs = _dot(h, w_ref[:, 2816:2944])
    krs = _rope_slot(krs, ck_ref[...], s1k_ref[...], s2k_ref[...])
    kr_out[...] = krs[:, 0:MLA_ROPE]


def _even_in(x, mods, norm_g, w_perm, q_norm, kv_norm, wuq_slot, tabs, rows_per_seq):
    n, d = x.shape
    tm, nb = _row_tiling(n, rows_per_seq, 512)
    ttab = tabs[0].shape[0]
    ntab = ttab // tm
    tab_spec = pl.BlockSpec((tm, LANES), lambda i: (i % ntab, 0))
    row = lambda w: pl.BlockSpec((tm, w), lambda i: (i, 0))
    return pl.pallas_call(
        functools.partial(_even_in_kernel, nb=nb),
        grid=(n // tm,),
        in_specs=[row(d), _mods_spec(tm, nb, rows_per_seq), _const_spec((1, d)),
                  _const_spec(w_perm.shape), _const_spec((1, 768)), _const_spec((1, 256)),
                  _const_spec(wuq_slot.shape)] + [tab_spec] * 6,
        out_specs=[row(1024), row(256), row(MLA_ROPE), row(RW_COLS)],
        out_shape=[jax.ShapeDtypeStruct((n, 1024), BF16), jax.ShapeDtypeStruct((n, 256), F32),
                   jax.ShapeDtypeStruct((n, MLA_ROPE), F32), jax.ShapeDtypeStruct((n, RW_COLS), F32)],
        compiler_params=_cparams(("parallel",)),
    )(x, mods, norm_g.reshape(1, d), w_perm, q_norm.reshape(1, 768), kv_norm.reshape(1, 256),
      wuq_slot, *tabs)


def _rope_tables(pos, tile_to):
    half = MLA_ROPE // 2
    freqs = ROPE_BASE ** (-jnp.arange(half, dtype=F32) / half)
    ang = pos.astype(F32)[:, None] * freqs[None, :]
    cos, sin = jnp.cos(ang), jnp.sin(ang)
    t = pos.shape[0]
    z = lambda w: jnp.zeros((t, w), F32)
    o = lambda w: jnp.ones((t, w), F32)
    cq = jnp.concatenate([o(64), cos, cos, z(32)], axis=1)
    s1q = jnp.concatenate([z(64), -sin, z(48)], axis=1)
    s2q = jnp.concatenate([z(80), sin, z(32)], axis=1)
    ck = jnp.concatenate([cos, cos, z(96)], axis=1)
    s1k = jnp.concatenate([-sin, z(112)], axis=1)
    s2k = jnp.concatenate([z(16), sin, z(96)], axis=1)
    tabs = [cq, s1q, s2q, ck, s1k, s2k]
    if tile_to > t:
        tabs = [jnp.tile(a, (tile_to // t, 1)) for a in tabs]
    return tabs


def _kv_expand_kernel(ckv_ref, kr_ref, wk_ref, sel_ref, wv_ref, one_ref, k_out, v_out):
    c = ckv_ref[...].astype(BF16)
    k = _dot(c, wk_ref[...]) + _dot(kr_ref[...].astype(BF16), sel_ref[...])
    k_out[...] = k.astype(BF16)
    v_out[0] = (_dot_nt(wv_ref[...], c) + one_ref[...]).astype(BF16)


def _kv_expand(ckv, kr, wk_slot, sel, wvt_slot, one_col, batch):
    n = ckv.shape[0]
    t_k = n // batch
    tm = 1024 if t_k % 1024 == 0 else 512
    assert t_k % tm == 0
    per = t_k // tm
    row = lambda w: pl.BlockSpec((tm, w), lambda i: (i, 0))
    return pl.pallas_call(
        _kv_expand_kernel,
        grid=(n // tm,),
        in_specs=[row(256), row(MLA_ROPE), _const_spec(wk_slot.shape), _const_spec(sel.shape),
                  _const_spec(wvt_slot.shape), _const_spec(one_col.shape)],
        out_specs=[row(1024), pl.BlockSpec((1, MLA_V_ROWS, tm), lambda i: (i // per, 0, i % per))],
        out_shape=[jax.ShapeDtypeStruct((n, 1024), BF16),
                   jax.ShapeDtypeStruct((batch, MLA_V_ROWS, t_k), BF16)],
        compiler_params=_cparams(("parallel",)),
    )(ckv, kr, wk_slot, sel, wvt_slot, one_col)


def _mla_attn_kernel(qi_ref, ki_ref, fl_ref, q_ref, k_ref, vt_ref, o_ref, m_scr, acc_scr,
                     *, tq, tk):
    p_id = pl.program_id(1)
    flags = fl_ref[p_id]
    first = (flags & 1) != 0
    last = (flags & 2) != 0
    masked = (flags & 4) != 0

    @pl.when(first)
    def _():
        m_scr[...] = jnp.full(m_scr.shape, NEG, F32)
        acc_scr[...] = jnp.zeros(acc_scr.shape, F32)

    qb = min(MLA_QUERY_BLOCK, tq)

    def scores_t(h, c0):
        sl = slice(h * LANES, (h + 1) * LANES)
        return _dot_nt(k_ref[0, :, sl], q_ref[0, c0:c0 + qb, sl])

    def body(use_mask, first_query=0):
        insts = [(h, c0) for h in range(MLA_HEADS) for c0 in range(first_query, tq, qb)]
        if use_mask:
            q_chunk0 = (qi_ref[p_id] * tq) // CHUNK
            k_chunk0 = (ki_ref[p_id] * tk) // CHUNK
            kc = k_chunk0 + lax.broadcasted_iota(jnp.int32, (tk, qb), 0) // CHUNK
            qc_local = lax.broadcasted_iota(jnp.int32, (tk, qb), 1) // CHUNK
        ahead = [scores_t(*insts[i]) for i in range(min(MLA_SCORES_AHEAD, len(insts)))]
        for idx, (h, c0) in enumerate(insts):
            s = ahead.pop(0)
            if idx + MLA_SCORES_AHEAD < len(insts):
                ahead.append(scores_t(*insts[idx + MLA_SCORES_AHEAD]))
            if use_mask:
                s = jnp.where(kc <= qc_local + (q_chunk0 + c0 // CHUNK), s, NEG)
            m_prev = m_scr[h:h + 1, c0:c0 + qb]
            m_new = jnp.maximum(m_prev, jnp.max(s, axis=0, keepdims=True))
            m_scr[h:h + 1, c0:c0 + qb] = m_new
            alpha = jnp.exp2(m_prev - m_new)
            p_t = jnp.exp2(s - m_new).astype(BF16)
            vt_h = vt_ref[0, h * MLA_V_SLOT:(h + 1) * MLA_V_SLOT, :]
            acc_scr[h, :, c0:c0 + qb] = alpha * acc_scr[h, :, c0:c0 + qb] + _dot(vt_h, p_t)

    diag = flags >> 3
    for j in range(max(tq // tk, 1)):
        @pl.when(masked & (diag == j))
        def _(j=j):
            body(True, first_query=j * tk)

    @pl.when(jnp.logical_not(masked))
    def _():
        body(False)

    @pl.when(last)
    def _():
        outs = []
        for h in range(MLA_HEADS):
            a = acc_scr[h]
            outs.append(a[0:MLA_V] / a[MLA_V:MLA_V + 1])
        o_ref[0] = jnp.concatenate(outs, axis=0).T.astype(BF16)


MLA_Q_TILE = 1024
MLA_KV_TILE = 512
MLA_QUERY_BLOCK = 256
MLA_SCORES_AHEAD = 3


def _mla_attn(q, k, vt, pairs, tq, tk):
    b, t_q, _ = q.shape
    qi, ki, fl = pairs
    grid_spec = pltpu.PrefetchScalarGridSpec(
        num_scalar_prefetch=3,
        grid=(b, qi.shape[0]),
        in_specs=[pl.BlockSpec((1, tq, 1024), lambda bb, p, qi, ki, fl: (bb, qi[p], 0)),
                  pl.BlockSpec((1, tk, 1024), lambda bb, p, qi, ki, fl: (bb, ki[p], 0)),
                  pl.BlockSpec((1, MLA_V_ROWS, tk), lambda bb, p, qi, ki, fl: (bb, 0, ki[p]))],
        out_specs=pl.BlockSpec((1, tq, 512), lambda bb, p, qi, ki, fl: (bb, qi[p], 0)),
        scratch_shapes=[pltpu.VMEM((MLA_HEADS, tq), F32),
                        pltpu.VMEM((MLA_HEADS, MLA_V_SLOT, tq), F32)])
    return pl.pallas_call(
        functools.partial(_mla_attn_kernel, tq=tq, tk=tk),
        grid_spec=grid_spec,
        out_shape=jax.ShapeDtypeStruct((b, t_q, 512), BF16),
        compiler_params=_cparams(("parallel", "arbitrary")),
    )(qi, ki, fl, q, k, vt)


def _causal_pairs(nq, ratio):
    qi, ki, fl = [], [], []
    for a in range(nq):
        n_kv = (a + 1) * ratio
        for c in range(n_kv):
            qi.append(a)
            ki.append(c)
            diag = max(c - a * ratio, 0)
            fl.append((1 if c == 0 else 0) | (2 if c == n_kv - 1 else 0) | (4 if c >= a * ratio else 0)
                      | (diag << 3))
    return tuple(jnp.asarray(np.array(z, np.int32)) for z in (qi, ki, fl))


def _mla_decode_kernel(q_ref, cc_ref, kc_ref, cn_ref, kn_ref, wka_ref, prope_ref, wvp_ref, o_ref,
                       qa_scr, qr_scr, m_scr, l_scr, acc_scr, *, n_cache_blocks, t):
    k_id = pl.program_id(1)

    @pl.when(k_id == 0)
    def _():
        for h in range(MLA_HEADS):
            qs = q_ref[0, :, h * LANES:(h + 1) * LANES]
            qa_scr[h * t:(h + 1) * t, :] = _dot(qs, wka_ref[h]).astype(BF16)
            qr_scr[h * t:(h + 1) * t, :] = _dot(qs, prope_ref[...]).astype(BF16)
        m_scr[...] = jnp.full(m_scr.shape, NEG, F32)
        l_scr[...] = jnp.zeros(l_scr.shape, F32)
        acc_scr[...] = jnp.zeros(acc_scr.shape, F32)

    def update(ckv, kr):
        cb = ckv.astype(BF16)
        s = _dot_nt(qa_scr[...], cb) + _dot_nt(qr_scr[:, 0:MLA_ROPE], kr.astype(BF16))
        m_prev = m_scr[...]
        m_new = jnp.maximum(m_prev, jnp.max(s, axis=-1, keepdims=True))
        alpha = jnp.exp2(m_prev - m_new)
        p = jnp.exp2(s - m_new[:, 0:1])
        l_scr[...] = alpha * l_scr[...] + jnp.sum(p, axis=-1, keepdims=True)
        m_scr[...] = m_new
        acc_scr[...] = jnp.concatenate([alpha, alpha], axis=1) * acc_scr[...] + _dot(p.astype(BF16), cb)

    @pl.when(k_id < n_cache_blocks)
    def _():
        update(cc_ref[0], kc_ref[0])

    @pl.when(k_id == n_cache_blocks)
    def _():
        update(cn_ref[0], kn_ref[0])
        l = l_scr[...]
        o_lat = (acc_scr[...] / jnp.concatenate([l, l], axis=1)).astype(BF16)
        for pr in range(MLA_HEADS // 2):
            oe = o_lat[(2 * pr) * t:(2 * pr + 1) * t]
            oo = o_lat[(2 * pr + 1) * t:(2 * pr + 2) * t]
            o_ref[0, :, pr * LANES:(pr + 1) * LANES] = (
                _dot(oe, wvp_ref[2 * pr]) + _dot(oo, wvp_ref[2 * pr + 1])).astype(BF16)


MLA_DECODE_KV_TILE = 2048


def _mla_decode(q, ckv_cache, kr_cache, layer, ckv_new, kr_new, wka, prope, wvp):
    b, t, _ = q.shape
    n_past = ckv_cache.shape[2]
    kb = math.gcd(n_past, MLA_DECODE_KV_TILE)
    ncb = n_past // kb
    rows = MLA_HEADS * t
    per_b = lambda shape: pl.BlockSpec((1,) + shape, lambda bb, k: (bb, 0, 0))
    cache = lambda w: pl.BlockSpec((None, 1, kb, w), lambda bb, k: (layer, bb, jnp.minimum(k, ncb - 1), 0))
    return pl.pallas_call(
        functools.partial(_mla_decode_kernel, n_cache_blocks=ncb, t=t),
        grid=(b, ncb + 1),
        in_specs=[per_b((t, 1024)), cache(256), cache(MLA_ROPE), per_b((t, 256)), per_b((t, MLA_ROPE)),
                  _const_spec(wka.shape), _const_spec(prope.shape), _const_spec(wvp.shape)],
        out_specs=per_b((t, 512)),
        out_shape=jax.ShapeDtypeStruct((b, t, 512), BF16),
        scratch_shapes=[pltpu.VMEM((rows, 256), BF16), pltpu.VMEM((rows, LANES), BF16),
                        pltpu.VMEM((rows, LANES), F32), pltpu.VMEM((rows, LANES), F32),
                        pltpu.VMEM((rows, 256), F32)],
        compiler_params=_cparams(("parallel", "arbitrary")),
    )(q, ckv_cache, kr_cache, ckv_new, kr_new, wka, prope, wvp)


RW_CHUNK_GROUP = 2
RW_TIME_TILE = 512


def _pair_sum(x, ones_bd):
    hi, lo = _split2(x)
    return _dot(hi, ones_bd) + _dot(lo, ones_bd)


def _rwkv_kernel(pr_ref, sh0_ref, s0_ref, mu_ref, w0_ref, wl_ref, a0_ref, al_ref, g2_ref,
                 kk_ref, ka_ref, rk_ref, lnw_ref, lnb_ref, y_ref, st_ref,
                 prev_scr, s_scr, rt_scr, at_scr, bt_scr, kt_scr, bv_scr, k2_scr, v_scr,
                 cum_scr, yc_scr, *, tt):
    t_id = pl.program_id(1)
    n_pairs = RW_HEADS // 2

    @pl.when(t_id == 0)
    def _():
        s_scr[...] = s0_ref[0]
        prev_scr[0:1, :] = sh0_ref[0]

    n_chunks = tt // CHUNK
    groups = [list(range(c0, min(c0 + RW_CHUNK_GROUP, n_chunks))) for c0 in range(0, n_chunks, RW_CHUNK_GROUP)]
    gate, bonus = {}, {}

    carried_prev = prev_scr[0:1, :]
    prev_scr[0:1, :] = pr_ref[0, tt - 1:tt, :]

    li = lax.broadcasted_iota(jnp.int32, (LANES, LANES), 0)
    lj = lax.broadcasted_iota(jnp.int32, (LANES, LANES), 1)
    ones_bd = jnp.where((li // RW_N) == (lj // RW_N), 1.0, 0.0).astype(BF16)

    def head_sum(x):
        return jnp.concatenate([_pair_sum(x[:, p * LANES:(p + 1) * LANES], ones_bd)
                                for p in range(n_pairs)], axis=1)

    def prep_phase(gi):
        r0 = groups[gi][0] * CHUNK
        gr = len(groups[gi]) * CHUNK
        rows = slice(r0, r0 + gr)
        pr = pr_ref[0, rows, :]
        before = carried_prev if gi == 0 else pr_ref[0, r0 - 1:r0, :]
        row = lax.broadcasted_iota(jnp.int32, (gr, 1), 0)
        prev = jnp.where(row == 0, before, pltpu.roll(pr, 1, axis=0))
        pm = pr + (prev - pr) * mu_ref[...]
        r = pm[:, 0:512]
        k = pm[:, 512:1024]
        v = pm[:, 1024:1536]
        wa = pm[:, 1536:1664]
        g_in = pm[:, 1664:1792]
        yield
        z = w0_ref[...] + _dot(jnp.tanh(wa).astype(BF16), wl_ref[...])
        nz = -z
        w = -(jnp.maximum(nz, 0.0) + jnp.log(1.0 + jnp.exp(-jnp.abs(nz)))) - 0.5
        ld = -jnp.exp(w)
        yield
        a_sig = jax.nn.sigmoid(a0_ref[...] + _dot(wa.astype(BF16), al_ref[...]))
        gate[gi] = _dot(jax.nn.sigmoid(g_in).astype(BF16), g2_ref[...])
        yield
        kk = k * kk_ref[...]
        kk = kk * lax.rsqrt(jnp.maximum(head_sum(kk * kk), 1e-24))
        k2 = k * (1.0 + (a_sig - 1.0) * ka_ref[...])
        yield
        bonus[gi] = head_sum(r * k2 * rk_ref[...]) * v
        yield
        ti = lax.broadcasted_iota(jnp.int32, (gr, gr), 0)
        tj = lax.broadcasted_iota(jnp.int32, (gr, gr), 1)
        tri = jnp.where(((ti // CHUNK) == (tj // CHUNK)) & (tj <= ti), 1.0, 0.0).astype(BF16)
        l1 = ld.astype(BF16)
        rem = ld - l1.astype(F32)
        l2 = rem.astype(BF16)
        l3 = (rem - l2.astype(F32)).astype(BF16)
        cum = (_dot(tri, l3) + _dot(tri, l2)) + _dot(tri, l1)
        yield
        winv = jnp.exp(-cum)
        rt_scr[rows, :] = r * jnp.exp(cum)
        at_scr[rows, :] = -kk * jnp.exp(cum - ld)
        yield
        bv = kk * a_sig
        bt_scr[rows, :] = bv * winv
        kt_scr[rows, :] = k2 * winv
        bv_scr[rows, :] = bv
        k2_scr[rows, :] = k2
        v_scr[rows, :] = v
        cum_scr[rows, :] = cum

    lane = lax.broadcasted_iota(jnp.int32, (CHUNK, LANES), 1)
    even = lane < RW_N
    strict = (lj % RW_N) < (li % RW_N)
    incl = (lj % RW_N) <= (li % RW_N)
    eye = jnp.where(li == lj, 1.0, 0.0).astype(F32)
    pairs = range(n_pairs)

    def same_block(m):
        return (li // m) == (lj // m)

    def stack_f32(x):
        return jnp.concatenate([jnp.where(even, x, 0.0), jnp.where(even, 0.0, x)], axis=0)

    def stack(x):
        return stack_f32(x).astype(BF16)

    mm = lambda a, b: _dot(a.astype(BF16), b.astype(BF16))
    rows_of = lambda c: slice(c * CHUNK, (c + 1) * CHUNK)
    lanes_of = lambda p: slice(p * LANES, (p + 1) * LANES)

    def independent_phase(chunks, res):
        insts = [(c, p) for c in chunks for p in pairs]
        load = lambda scr: [scr[rows_of(c), lanes_of(p)] for c, p in insts]
        cum_c = load(cum_scr)
        cum_l = [a[CHUNK - 1:CHUNK, :] for a in cum_c]
        w2 = [jnp.exp(a - b) for a, b in zip(cum_l, cum_c)]
        rs = [stack(a) for a in load(rt_scr)]
        as_ = [stack(a) for a in load(at_scr)]
        bs = [stack(a) for a in load(bt_scr)]
        ks = [stack(a) for a in load(kt_scr)]
        v_f = [stack_f32(a) for a in load(v_scr)]
        vs = [a.astype(BF16) for a in v_f]
        b2s = [stack(a * w) for a, w in zip(load(bv_scr), w2)]
        k2s = [stack(a * w) for a, w in zip(load(k2_scr), w2)]
        yield
        n_m = [jnp.where(strict, _dot_nt(a, b), 0.0) for a, b in zip(as_, bs)]
        mk = [jnp.where(strict, _dot_nt(a, b), 0.0).astype(BF16) for a, b in zip(as_, ks)]
        cb = [jnp.where(incl, _dot_nt(a, b), 0.0).astype(BF16) for a, b in zip(rs, bs)]
        ck = [jnp.where(incl, _dot_nt(a, b), 0.0).astype(BF16) for a, b in zip(rs, ks)]
        yield
        n8 = [jnp.where(same_block(8), n, 0.0) for n in n_m]
        t = [eye + a for a in n8]
        p2 = [mm(a, a) for a in n8]
        mv = [_dot(a, b) for a, b in zip(mk, vs)]
        yield
        t = [a + mm(a, b) for a, b in zip(t, p2)]
        p4 = [mm(a, a) for a in p2]
        cv = [_dot(a, b) for a, b in zip(ck, vs)]
        yield
        t = [a + mm(a, b) for a, b in zip(t, p4)]
        vk = [_dot(a.T.astype(BF16), b) for a, b in zip(v_f, k2s)]
        yield
        for m in (8, 16, 32):
            off = same_block(2 * m) & jnp.logical_not(same_block(m))
            x = [mm(jnp.where(off, n, 0.0), a) for n, a in zip(n_m, t)]
            yield
            t = [a + mm(a, b) for a, b in zip(t, x)]
            yield
        for i, key in enumerate(insts):
            res[key] = dict(as_=as_[i], rs=rs[i], b2s=b2s[i], cb=cb[i], mv=mv[i], cv=cv[i], vk=vk[i],
                            tinv=t[i].astype(BF16), wl=jnp.exp(cum_l[i]))

    def dependent_phase(chunks, res):
        for c in chunks:
            rc = [res[(c, p)] for p in pairs]
            s = [s_scr[p] for p in pairs]
            sb = [a.astype(BF16) for a in s]
            x = [_dot_nt(r_['as_'], b) + r_['mv'] for r_, b in zip(rc, sb)]
            rs_s = [_dot_nt(r_['rs'], b) + r_['cv'] for r_, b in zip(rc, sb)]
            yield
            u = [_dot(r_['tinv'], a.astype(BF16)) for r_, a in zip(rc, x)]
            yield
            ys = [a + _dot(r_['cb'], b.astype(BF16)) for a, r_, b in zip(rs_s, rc, u)]
            for p in pairs:
                yc_scr[rows_of(c), lanes_of(p)] = ys[p][0:CHUNK] + ys[p][CHUNK:2 * CHUNK]
                s_scr[p] = s[p] * rc[p]['wl'] + _dot(u[p].T.astype(BF16), rc[p]['b2s']) + rc[p]['vk']
            yield

    def emit(*gens):
        gens = list(gens)
        while gens:
            for gen in list(gens):
                try:
                    next(gen)
                except StopIteration:
                    gens.remove(gen)

    def output_phase(gi):
        r0 = groups[gi][0] * CHUNK
        rows = slice(r0, r0 + len(groups[gi]) * CHUNK)
        y = yc_scr[rows, :]
        mean = head_sum(y) * (1.0 / RW_N)
        dlt = y - mean
        yield
        var = head_sum(dlt * dlt) * (1.0 / RW_N)
        yield
        yn = dlt * lax.rsqrt(var + RW_GN_EPS) * lnw_ref[...] + lnb_ref[...]
        y_ref[0, rows, :] = ((yn + bonus[gi]) * gate[gi]).astype(BF16)

    res = {}
    n_g = len(groups)
    for step in range(n_g + 3):
        live = []
        if step < n_g:
            live.append(prep_phase(step))
        if 0 <= step - 1 < n_g:
            live.append(independent_phase(groups[step - 1], res))
        if 0 <= step - 2 < n_g:
            live.append(dependent_phase(groups[step - 2], res))
        if 0 <= step - 3 < n_g:
            live.append(output_phase(step - 3))
        emit(*live)

    @pl.when(t_id == pl.num_programs(1) - 1)
    def _():
        st_ref[0] = s_scr[...]


def _rwkv(prw, sh0, s0_bd, wts):
    b, t, _ = prw.shape
    tt = min(RW_TIME_TILE, t)
    assert t % tt == 0 and tt % CHUNK == 0
    c512 = _const_spec((1, RW_C))
    scr = lambda: pltpu.VMEM((tt, RW_C), F32)
    return pl.pallas_call(
        functools.partial(_rwkv_kernel, tt=tt),
        grid=(b, t // tt),
        in_specs=[pl.BlockSpec((1, tt, RW_COLS), lambda bb, i: (bb, i, 0)),
                  pl.BlockSpec((1, 1, RW_COLS), lambda bb, i: (bb, 0, 0)),
                  pl.BlockSpec((1, 4, LANES, LANES), lambda bb, i: (bb, 0, 0, 0)),
                  _const_spec((1, RW_COLS)), c512, _const_spec((LANES, RW_C)), c512,
                  _const_spec((LANES, RW_C)), _const_spec((LANES, RW_C)),
                  c512, c512, c512, c512, c512],
        out_specs=[pl.BlockSpec((1, tt, RW_C), lambda bb, i: (bb, i, 0)),
                   pl.BlockSpec((1, 4, LANES, LANES), lambda bb, i: (bb, 0, 0, 0))],
        out_shape=[jax.ShapeDtypeStruct((b, t, RW_C), BF16),
                   jax.ShapeDtypeStruct((b, 4, LANES, LANES), F32)],
        scratch_shapes=[pltpu.VMEM((8, RW_COLS), F32), pltpu.VMEM((4, LANES, LANES), F32)]
                       + [scr() for _ in range(9)],
        compiler_params=_cparams(("parallel", "arbitrary")),
    )(prw, sh0, s0_bd, *wts)


def _state_to_bd(s):
    b = s.shape[0]
    s = s.reshape(b, 4, 2, RW_N, RW_N)
    z = jnp.zeros_like(s[:, :, 0])
    top = jnp.concatenate([s[:, :, 0], z], axis=-1)
    bot = jnp.concatenate([z, s[:, :, 1]], axis=-1)
    return jnp.concatenate([top, bot], axis=-2)


def _state_from_bd(s):
    b = s.shape[0]
    return jnp.stack([s[:, :, :RW_N, :RW_N], s[:, :, RW_N:, RW_N:]], axis=2).reshape(b, RW_HEADS, RW_N, RW_N)


_ODD_SEGS = ([(1024 + g * 64, 1088 + g * 64) for g in range(SW_KV_HEADS) for _ in range(2)]
             + [(1280 + g * 64, 1344 + g * 64) for g in range(SW_KV_HEADS) for _ in range(2)])


def _odd_in_kernel(x_ref, mods_ref, ng_ref, w_ref, b_ref, q_out, k_out, v_out, *, nb):
    h = _norm_mod(x_ref[...], ng_ref[...], mods_ref[:, 3, :], mods_ref[:, 4, :], nb).astype(BF16)
    q = _dot(h, w_ref[:, 0:1024]) + b_ref[:, 0:1024]
    q_out[...] = (q * (SW_HD ** -0.5 * LOG2E)).astype(BF16)
    k_out[...] = _dot(h, w_ref[:, 1024:1536]) + b_ref[:, 1024:1536]
    v_out[...] = _dot(h, w_ref[:, 1536:2048]) + b_ref[:, 1536:2048]


def _odd_in(x, mods, norm_g, w_perm, b_perm, rows_per_seq):
    n, d = x.shape
    tm, nb = _row_tiling(n, rows_per_seq, 512)
    row = lambda w: pl.BlockSpec((tm, w), lambda i: (i, 0))
    return pl.pallas_call(
        functools.partial(_odd_in_kernel, nb=nb),
        grid=(n // tm,),
        in_specs=[row(d), _mods_spec(tm, nb, rows_per_seq), _const_spec((1, d)),
                  _const_spec(w_perm.shape), _const_spec((1, 2048))],
        out_specs=[row(1024), row(512), row(512)],
        out_shape=[jax.ShapeDtypeStruct((n, 1024), BF16), jax.ShapeDtypeStruct((n, 512), F32),
                   jax.ShapeDtypeStruct((n, 512), F32)],
        compiler_params=_cparams(("parallel",)),
    )(x, mods, norm_g.reshape(1, d), w_perm, b_perm.reshape(1, 2048))


def _swa_kernel(sinks_ref, q_ref, kp_ref, kc_ref, vp_ref, vc_ref, o_ref, *, tq, mask_first_prev):
    nk = WINDOW + tq
    qi = lax.broadcasted_iota(jnp.int32, (tq, nk), 0)
    kj = lax.broadcasted_iota(jnp.int32, (tq, nk), 1)
    kc = kj // CHUNK - WINDOW // CHUNK
    qc = qi // CHUNK
    vis = (kc <= qc) & (kc >= qc - WINDOW // CHUNK)
    if mask_first_prev:
        vis = vis & ((kj >= WINDOW) | (pl.program_id(1) > 0))
    ndist = jnp.where(vis, -jnp.abs(qi + WINDOW - kj).astype(F32), NEG)
    keys = jnp.concatenate([kp_ref[0], kc_ref[0]], axis=0).astype(BF16)
    vals = jnp.concatenate([vp_ref[0], vc_ref[0]], axis=0)
    klane = lax.broadcasted_iota(jnp.int32, (nk, LANES), 1)
    v_slots = [jnp.where(klane < SW_HD, vals[:, g * LANES:(g + 1) * LANES], 1.0).astype(BF16)
               for g in range(SW_KV_HEADS)]
    lane = lax.broadcasted_iota(jnp.int32, (tq, LANES), 1)
    low = lane < SW_HD
    sb = min(SWA_SUB_BLOCK, tq)

    def scores(h):
        qp = q_ref[0, :, (h // 2) * LANES:(h // 2 + 1) * LANES]
        qh = jnp.where(low if h % 2 == 0 else jnp.logical_not(low), qp, jnp.zeros_like(qp))
        return _dot_nt(qh, keys[:, (h // SW_GROUP) * LANES:(h // SW_GROUP + 1) * LANES])

    outs = []
    ahead = [scores(h) for h in range(SWA_SCORES_AHEAD)]
    for h in range(SW_HEADS):
        s = ahead.pop(0)
        if h + SWA_SCORES_AHEAD < SW_HEADS:
            ahead.append(scores(h + SWA_SCORES_AHEAD))
        slope = (2.0 ** (-8.0 * (h + 1) / SW_HEADS)) * LOG2E
        sk = sinks_ref[h] * LOG2E
        es, ms = [], []
        for r1 in range(0, tq, sb):
            z = s[r1:r1 + sb] + slope * ndist[r1:r1 + sb]
            m = jnp.maximum(jnp.broadcast_to(jnp.max(z, axis=-1, keepdims=True), (sb, LANES)), sk)
            m_wide = jnp.concatenate([m] * (nk // LANES), axis=1) if nk % LANES == 0 else m[:, 0:1]
            es.append(jnp.exp2(z - m_wide).astype(BF16))
            ms.append(m)
        e = jnp.concatenate(es, axis=0) if len(es) > 1 else es[0]
        m = jnp.concatenate(ms, axis=0) if len(ms) > 1 else ms[0]
        pv = _dot(e, v_slots[h // SW_GROUP])
        outs.append(pv / (pltpu.roll(pv, SW_HD, axis=1) + jnp.exp2(sk - m)))
    for pr in range(SW_HEADS // 2):
        o_ref[0, :, pr * LANES:(pr + 1) * LANES] = jnp.where(
            low, outs[2 * pr], pltpu.roll(outs[2 * pr + 1], SW_HD, axis=1)).astype(BF16)


SWA_Q_TILE = 256
SWA_SUB_BLOCK = 32
SWA_SCORES_AHEAD = 2


def _swa(q, k_prev, k_cur, v_prev, v_cur, sinks, tq, same_array):
    b, t, _ = q.shape
    nt = t // tq
    per = tq // WINDOW
    if same_array:
        prev_map = lambda bb, i: (bb, jnp.maximum(i * per - 1, 0), 0)
    else:
        prev_map = lambda bb, i: (bb, 0, 0)
    cur = lambda w: pl.BlockSpec((1, tq, w), lambda bb, i: (bb, i, 0))
    prev = pl.BlockSpec((1, WINDOW, 512), prev_map)
    return pl.pallas_call(
        functools.partial(_swa_kernel, tq=tq, mask_first_prev=same_array),
        grid=(b, nt),
        in_specs=[pl.BlockSpec(memory_space=pltpu.SMEM), cur(1024), prev, cur(512), prev, cur(512)],
        out_specs=cur(1024),
        out_shape=jax.ShapeDtypeStruct((b, t, 1024), BF16),
        compiler_params=_cparams(("parallel", "parallel")),
    )(sinks, q, k_prev, k_cur, v_prev, v_cur)


def _undup(a):
    return a.reshape(a.shape[:-1] + (SW_KV_HEADS, 2, SW_HD))[..., 0, :]


def _dup(a):
    return jnp.concatenate([a, a], axis=-1).reshape(a.shape[:-2] + (512,))


def _prep_weights(p):
    depth = p['w_ada'].shape[0]
    n_even, n_odd = (depth + 1) // 2, depth // 2
    w = {}
    w['ffn_in'] = p['ffn_w_in'].astype(BF16)
    w['ffn_out'] = p['ffn_w_out'].astype(BF16)
    wi = p['even_w_in'].astype(BF16)
    w['even_in'] = jnp.concatenate(
        [wi[:, :, 0:1024]] + [wi[:, :, _RW_OFF + a:_RW_OFF + b] for a, b in _RW_SEGS]
        + [wi[:, :, 1024:1056], jnp.zeros((n_even, 1024, 96), BF16)], axis=2)
    w['wuq'] = jnp.pad(p['mla_w_uq'], ((0, 0), (0, 0), (0, 0), (0, 32))).reshape(n_even, 768, 1024).astype(BF16)
    w['wk'] = jnp.pad(p['mla_w_ukv'][..., :MLA_NOPE], ((0, 0), (0, 0), (0, 0), (0, 64))
                      ).reshape(n_even, 256, 1024).astype(BF16)
    wv_t = jnp.transpose(p['mla_w_ukv'][..., MLA_NOPE:], (0, 2, 3, 1))
    w['wv'] = jnp.pad(wv_t, ((0, 0), (0, 0), (0, MLA_V_SLOT - MLA_V), (0, 0))
                      ).reshape(n_even, MLA_V_ROWS, 256).astype(BF16)
    sel = np.zeros((MLA_ROPE, 1024), np.float32)
    one = np.zeros((MLA_V_ROWS, 1), np.float32)
    for h in range(MLA_HEADS):
        sel[np.arange(MLA_ROPE), h * LANES + MLA_NOPE + np.arange(MLA_ROPE)] = 1.0
        one[h * MLA_V_SLOT + MLA_V, 0] = 1.0
    w['sel'] = jnp.asarray(sel).astype(BF16)
    w['one'] = jnp.asarray(one)
    wk_t = jnp.transpose(p['mla_w_ukv'][..., :MLA_NOPE], (0, 2, 3, 1))
    w['wka'] = jnp.pad(wk_t, ((0, 0), (0, 0), (0, 64), (0, 0))).astype(BF16)
    prope = np.zeros((LANES, LANES), np.float32)
    prope[MLA_NOPE + np.arange(MLA_ROPE), np.arange(MLA_ROPE)] = 1.0
    w['prope'] = jnp.asarray(prope).astype(BF16)
    wv_h = jnp.transpose(p['mla_w_ukv'][..., MLA_NOPE:], (0, 2, 1, 3))
    wv_even = jnp.pad(wv_h[:, 0::2], ((0, 0), (0, 0), (0, 0), (0, 64)))
    wv_odd = jnp.pad(wv_h[:, 1::2], ((0, 0), (0, 0), (0, 0), (64, 0)))
    w['wvp'] = jnp.stack([wv_even, wv_odd], axis=2).reshape(n_even, MLA_HEADS, 256, LANES).astype(BF16)
    z64 = jnp.zeros((n_even, 64, RW_C), F32)
    w['wl'] = jnp.concatenate([p['rw_w2'], z64], axis=1).astype(BF16)
    w['al'] = jnp.concatenate([z64, p['rw_a2']], axis=1).astype(BF16)
    w['g2'] = p['rw_g2'].astype(BF16)
    w['mu'] = _rw_permute(p['rw_mu'])
    w['even_out'] = p['even_w_out'].astype(BF16)
    wo = p['odd_w_qkv'].astype(BF16)
    w['odd_in'] = jnp.concatenate([wo[:, :, 0:1024]] + [wo[:, :, a:b] for a, b in _ODD_SEGS], axis=2)
    bo = p['odd_b_qkv']
    w['odd_b'] = jnp.concatenate([bo[:, 0:1024]] + [bo[:, a:b] for a, b in _ODD_SEGS], axis=1)
    w['odd_out'] = p['odd_w_out'].astype(BF16)
    return w


def _trunk(x3, mods_all, start, past, p, w):
    b, t, d = x3.shape
    n = b * t
    depth = mods_all.shape[0]
    x = x3.reshape(n, d)
    rows = t
    tm_even, _ = _row_tiling(n, rows, 512)
    pos = start + jnp.arange(t)
    tabs = _rope_tables(pos, tm_even)
    even_states, odd_states = [], []
    for i in range(depth):
        mods = mods_all[i]
        j = i // 2
        x = _ffn(x, mods, p['norm_g'][i, 0], w['ffn_in'], w['ffn_out'], i, 0, 0, rows)
        if i % 2 == 0:
            q, ckv, kr, prw = _even_in(x, mods, p['norm_g'][i, 1], w['even_in'][j], p['mla_q_norm'][j],
                                       p['mla_kv_norm'][j], w['wuq'][j], tabs, rows)
            if past is None:
                kx, vtx = _kv_expand(ckv, kr, w['wk'][j], w['sel'], w['wv'][j], w['one'], b)
                tk = min(MLA_KV_TILE, t)
                tq = min(MLA_Q_TILE, t)
                att = _mla_attn(q.reshape(b, t, 1024), kx.reshape(b, t, 1024), vtx,
                                _causal_pairs(t // tq, tq // tk), tq, tk)
                s0 = jnp.zeros((b, RW_HEADS, RW_N, RW_N), F32)
                sh0 = jnp.zeros((b, RW_COLS), F32)
            else:
                s0, sh0 = past[2][j], past[3][j]
                att = _mla_decode(q.reshape(b, t, 1024), past[0], past[1], j, ckv.reshape(b, t, 256),
                                  kr.reshape(b, t, MLA_ROPE), w['wka'][j], w['prope'], w['wvp'][j])
            rw_wts = (w['mu'][j].reshape(1, RW_COLS), p['rw_w0'][j].reshape(1, RW_C), w['wl'][j],
                      p['rw_a0'][j].reshape(1, RW_C), w['al'][j], w['g2'][j],
                      p['rw_k_k'][j].reshape(1, RW_C), p['rw_k_a'][j].reshape(1, RW_C),
                      p['rw_r_k'][j].reshape(1, RW_C), p['rw_ln_w'][j].reshape(1, RW_C),
                      p['rw_ln_b'][j].reshape(1, RW_C))
            prw3 = prw.reshape(b, t, RW_COLS)
            y_rw, s_bd = _rwkv(prw3, _rw_permute(sh0).reshape(b, 1, RW_COLS), _state_to_bd(s0), rw_wts)
            mix = ((att.reshape(n, 512), w['even_out'][j][:512]), (y_rw.reshape(n, RW_C), w['even_out'][j][512:]))
            even_states.append((ckv.reshape(b, t, 256), kr.reshape(b, t, MLA_ROPE), _state_from_bd(s_bd),
                                _rw_unpermute(prw3[:, t - 1, :])))
        else:
            q, kd, vd = _odd_in(x, mods, p['norm_g'][i, 1], w['odd_in'][j], w['odd_b'][j], rows)
            q3, kd3, vd3 = q.reshape(b, t, 1024), kd.reshape(b, t, 512), vd.reshape(b, t, 512)
            if past is None:
                tq = min(SWA_Q_TILE, t)
                o = _swa(q3, kd3, kd3, vd3, vd3, p['swa_sinks'][j], tq, True)
                keep = min(WINDOW, t)
                k_new, v_new = _undup(kd3[:, t - keep:]), _undup(vd3[:, t - keep:])
            else:
                k_past, v_past = past[4][j], past[5][j]
                o = _swa(q3, _dup(k_past), kd3, _dup(v_past), vd3, p['swa_sinks'][j], t, False)
                k_new = jnp.concatenate([k_past, _undup(kd3)], axis=1)[:, t:]
                v_new = jnp.concatenate([v_past, _undup(vd3)], axis=1)[:, t:]
            mix = ((o.reshape(n, 1024), w['odd_out'][j]),)
            odd_states.append((k_new, v_new))
        fg = p['final_norm_g'] if i == depth - 1 else None
        x = _ffn(x, mods, p['norm_g'][i, 2], w['ffn_in'], w['ffn_out'], i, 1, 2, rows, final_g=fg, mix=mix)
    es = [jnp.stack([st[k] for st in even_states]) for k in range(4)]
    os_ = [jnp.stack([st[k] for st in odd_states]) for k in range(2)]
    return x.reshape(b, t, d), es + os_


def kernel(x_prompt, x_sample, cache_mla_ckv, cache_mla_krope, state_rwkv, state_rwkv_shift, cache_swa_k, cache_swa_v, c_prompt, c_sample, w_ada, b_ada, norm_g, ffn_w_in, ffn_w_out, even_w_in, even_w_out, mla_q_norm, mla_kv_norm, mla_w_uq, mla_w_ukv, rw_mu, rw_w0, rw_w2, rw_a0, rw_a2, rw_g2, rw_k_k, rw_k_a, rw_r_k, rw_ln_w, rw_ln_b, odd_w_qkv, odd_b_qkv, odd_w_out, swa_sinks, final_norm_g):
    p = dict(w_ada=w_ada, b_ada=b_ada, norm_g=norm_g, ffn_w_in=ffn_w_in, ffn_w_out=ffn_w_out,
             even_w_in=even_w_in, even_w_out=even_w_out, mla_q_norm=mla_q_norm, mla_kv_norm=mla_kv_norm,
             mla_w_uq=mla_w_uq, mla_w_ukv=mla_w_ukv, rw_mu=rw_mu, rw_w0=rw_w0, rw_w2=rw_w2, rw_a0=rw_a0,
             rw_a2=rw_a2, rw_g2=rw_g2, rw_k_k=rw_k_k, rw_k_a=rw_k_a, rw_r_k=rw_r_k, rw_ln_w=rw_ln_w,
             rw_ln_b=rw_ln_b, odd_w_qkv=odd_w_qkv, odd_b_qkv=odd_b_qkv, odd_w_out=odd_w_out,
             swa_sinks=swa_sinks, final_norm_g=final_norm_g)
    w = _prep_weights(p)
    depth = w_ada.shape[0]
    bp, bs = c_prompt.shape[0], c_sample.shape[0]
    d = c_prompt.shape[1]
    b_pad = -(-(bp + bs) // 8) * 8
    c_all = jnp.concatenate([c_prompt, c_sample, jnp.zeros((b_pad - bp - bs, d), F32)], axis=0)
    mods = _ada(c_all, w_ada, b_ada).reshape(depth, b_pad, 3 * N_SUB, d)
    y_prompt, sp = _trunk(x_prompt, mods[:, :bp], 0, None, p, w)
    past = (cache_mla_ckv, cache_mla_krope, state_rwkv, state_rwkv_shift, cache_swa_k, cache_swa_v)
    y_sample, ss = _trunk(x_sample, mods[:, bp:bp + bs], cache_mla_ckv.shape[2], past, p, w)
    return (y_prompt, y_sample, sp[0], sp[1], sp[2], sp[3], sp[4], sp[5],
            ss[0], ss[1], ss[2], ss[3], ss[4], ss[5])
```

```python
import functools
import math

import jax
import jax.numpy as jnp
import numpy as np
from jax import lax
from jax.experimental import pallas as pl
from jax.experimental.pallas import tpu as pltpu

F32 = jnp.float32
BF16 = jnp.bfloat16

CHUNK = 64
EPS = 1e-6
NEG = -1e30
N_SUB = 3
MLA_HEADS = 8
MLA_NOPE = 64
MLA_ROPE = 32
MLA_V = 64
MLA_Q_LORA = 768
MLA_KV_LORA = 256
MLA_SCALE = (MLA_NOPE + MLA_ROPE) ** -0.5
MLA_V_SLOT = 128
MLA_V_ROWS = MLA_HEADS * MLA_V_SLOT
LOG2E = math.log2(math.e)
ROPE_BASE = 10000.0
RW_HEADS = 8
RW_N = 64
RW_C = RW_HEADS * RW_N
RW_GN_EPS = 64e-5
RW_COLS = 3 * RW_C + 64 + 64 + 128
SW_HEADS = 16
SW_KV_HEADS = 4
SW_GROUP = 4
SW_HD = 64
WINDOW = 128

LANES = 128
VMEM_LIMIT = 56 * 1024 * 1024


def _cparams(sem):
    return pltpu.CompilerParams(dimension_semantics=sem, vmem_limit_bytes=VMEM_LIMIT)


def _const_spec(shape):
    nd = len(shape)
    return pl.BlockSpec(shape, lambda *_: (0,) * nd, pipeline_mode=pl.Buffered(1))


def _dot(a, b):
    return jnp.dot(a, b, preferred_element_type=F32)


def _dot_nt(a, b):
    return lax.dot_general(a, b, (((1,), (1,)), ((), ())), preferred_element_type=F32)


def _split2(x):
    hi = x.astype(BF16)
    lo = (x - hi.astype(F32)).astype(BF16)
    return hi, lo


def _norm_mod(x, g, sh, sc, nb):
    y = x * lax.rsqrt(jnp.mean(x * x, axis=-1, keepdims=True) + EPS) * g
    if nb == 1:
        return y * (1.0 + sc) + sh
    tm, d = x.shape
    y3 = y.reshape(nb, tm // nb, d)
    return (y3 * (1.0 + sc[:, None, :]) + sh[:, None, :]).reshape(tm, d)


def _gate_rows(g, y, nb):
    if nb == 1:
        return g * y
    tm, d = y.shape
    return (y.reshape(nb, tm // nb, d) * g[:, None, :]).reshape(tm, d)


def _row_tiling(n_rows, rows_per_seq, pref):
    if rows_per_seq >= pref:
        assert rows_per_seq % pref == 0
        return pref, 1
    tm = min(pref, n_rows)
    assert tm % rows_per_seq == 0 and n_rows % tm == 0
    return tm, tm // rows_per_seq


def _mods_spec(tm, nb, rows_per_seq):
    if nb == 1:
        tiles_per_seq = rows_per_seq // tm
        return pl.BlockSpec((1, 3 * N_SUB, 1024), lambda i: (i // tiles_per_seq, 0, 0))
    return pl.BlockSpec((nb, 3 * N_SUB, 1024), lambda i: (i, 0, 0))


def _ada_kernel(c_ref, w_ref, b_ref, o_ref):
    c = c_ref[...]
    cs = (c * jax.nn.sigmoid(c)).astype(BF16)
    o_ref[0] = _dot(cs, w_ref[0].astype(BF16)) + b_ref[0]


def _ada(c_all, w_ada, b_ada):
    depth, d, n = w_ada.shape
    bp = c_all.shape[0]
    tn = n // 4
    return pl.pallas_call(
        _ada_kernel,
        grid=(depth, n // tn),
        in_specs=[pl.BlockSpec((bp, d), lambda l, j: (0, 0)),
                  pl.BlockSpec((1, d, tn), lambda l, j: (l, 0, j)),
                  pl.BlockSpec((1, 1, tn), lambda l, j: (l, 0, j))],
        out_specs=pl.BlockSpec((1, bp, tn), lambda l, j: (l, 0, j)),
        out_shape=jax.ShapeDtypeStruct((depth, bp, n), F32),
        compiler_params=_cparams(("parallel", "parallel")),
    )(c_all, w_ada, b_ada.reshape(depth, 1, n))


FF_CHUNK = 256


def _ffn_kernel(x_ref, mods_ref, ng_ref, win_ref, wout_ref, *rest, sub, nb, d_ff, final, n_mix):
    mix_a, mix_w, rest = rest[:n_mix], rest[n_mix:2 * n_mix], rest[2 * n_mix:]
    if final:
        fg_ref, o_ref, a_scr = rest
    else:
        o_ref, a_scr = rest
    x = x_ref[...]
    if n_mix:
        ym = _dot(mix_a[0][...], mix_w[0][...])
        for a, w in zip(mix_a[1:], mix_w[1:]):
            ym = ym + _dot(a[...], w[...])
        x = x + _gate_rows(mods_ref[:, 5, :], ym, nb)
    sh = mods_ref[:, 3 * sub, :]
    sc = mods_ref[:, 3 * sub + 1, :]
    gt = mods_ref[:, 3 * sub + 2, :]
    h = _norm_mod(x, ng_ref[...], sh, sc, nb).astype(BF16)
    for c in range(d_ff // FF_CHUNK):
        lo = c * FF_CHUNK
        g = _dot(h, win_ref[:, lo:lo + FF_CHUNK])
        u = _dot(h, win_ref[:, d_ff + lo:d_ff + lo + FF_CHUNK])
        a_scr[:, lo:lo + FF_CHUNK] = (g * jax.nn.sigmoid(g) * u).astype(BF16)
    y = _dot(a_scr[...], wout_ref[...])
    out = x + _gate_rows(0.5 * gt, y, nb)
    if final:
        out = out * lax.rsqrt(jnp.mean(out * out, axis=-1, keepdims=True) + EPS) * fg_ref[...]
    o_ref[...] = out


def _ffn(x, mods, norm_g, w_in, w_out, layer, half, sub, rows_per_seq, final_g=None, mix=()):
    n, d = x.shape
    d_ff = w_out.shape[2]
    tm, nb = _row_tiling(n, rows_per_seq, 512)
    final = final_g is not None
    stacked = lambda a: pl.BlockSpec((None, None) + a.shape[2:], lambda i: (layer, half, 0, 0),
                                     pipeline_mode=pl.Buffered(1))
    in_specs = [pl.BlockSpec((tm, d), lambda i: (i, 0)),
                _mods_spec(tm, nb, rows_per_seq),
                _const_spec((1, d)), stacked(w_in), stacked(w_out)]
    args = [x, mods, norm_g.reshape(1, d), w_in, w_out]
    in_specs += [pl.BlockSpec((tm, a.shape[1]), lambda i: (i, 0)) for a, _ in mix]
    in_specs += [_const_spec(wm.shape) for _, wm in mix]
    args += [a for a, _ in mix] + [wm for _, wm in mix]
    if final:
        in_specs.append(_const_spec((1, d)))
        args.append(final_g.reshape(1, d))
    return pl.pallas_call(
        functools.partial(_ffn_kernel, sub=sub, nb=nb, d_ff=d_ff, final=final, n_mix=len(mix)),
        grid=(n // tm,),
        in_specs=in_specs,
        out_specs=pl.BlockSpec((tm, d), lambda i: (i, 0)),
        out_shape=jax.ShapeDtypeStruct((n, d), F32),
        scratch_shapes=[pltpu.VMEM((tm, d_ff), BF16)],
        compiler_params=_cparams(("parallel",)),
    )(*args)


_RW_OFF = 1056
_RW_SEGS = [(0, 512), (576, 1088), (1088, 1600), (512, 576), (1600, 1664), (1664, 1792)]
_RW_INV_SEGS = [(0, 512), (1536, 1600), (512, 1024), (1024, 1536), (1600, 1664), (1664, 1792)]


def _rw_permute(a):
    return jnp.concatenate([a[..., s:e] for s, e in _RW_SEGS], axis=-1)


def _rw_unpermute(a):
    return jnp.concatenate([a[..., s:e] for s, e in _RW_INV_SEGS], axis=-1)


def _rope_slot(v, c, s1, s2):
    w = v.shape[-1]
    return v * c + pltpu.roll(v, w - 16, axis=1) * s1 + pltpu.roll(v, 16, axis=1) * s2


def _even_in_kernel(x_ref, mods_ref, ng_ref, w_ref, qn_ref, kvn_ref, wuq_ref,
                    cq_ref, s1q_ref, s2q_ref, ck_ref, s1k_ref, s2k_ref,
                    q_out, ckv_out, kr_out, prw_out, *, nb):
    x = x_ref[...]
    h = _norm_mod(x, ng_ref[...], mods_ref[:, 3, :], mods_ref[:, 4, :], nb).astype(BF16)
    cq = _dot(h, w_ref[:, 0:768])
    cqn = (cq * lax.rsqrt(jnp.mean(cq * cq, axis=-1, keepdims=True) + EPS) * qn_ref[...]).astype(BF16)
    q = _dot(cqn, wuq_ref[...])
    rep = lambda t: jnp.concatenate([t] * MLA_HEADS, axis=1)
    q = _rope_slot(q, rep(cq_ref[...]), rep(s1q_ref[...]), rep(s2q_ref[...]))
    q_out[...] = (q * (MLA_SCALE * LOG2E)).astype(BF16)
    ckv = _dot(h, w_ref[:, 768:1024])
    ckv_out[...] = ckv * lax.rsqrt(jnp.mean(ckv * ckv, axis=-1, keepdims=True) + EPS) * kvn_ref[...]
    prw_out[...] = _dot(h, w_ref[:, 1024:2816])
    krs = _dot(h, w_ref[:, 2816:2944])
    krs = _rope_slot(krs, ck_ref[...], s1k_ref[...], s2k_ref[...])
    kr_out[...] = krs[:, 0:MLA_ROPE]


def _even_in(x, mods, norm_g, w_perm, q_norm, kv_norm, wuq_slot, tabs, rows_per_seq):
    n, d = x.shape
    tm, nb = _row_tiling(n, rows_per_seq, 512)
    ttab = tabs[0].shape[0]
    ntab = ttab // tm
    tab_spec = pl.BlockSpec((tm, LANES), lambda i: (i % ntab, 0))
    row = lambda w: pl.BlockSpec((tm, w), lambda i: (i, 0))
    return pl.pallas_call(
        functools.partial(_even_in_kernel, nb=nb),
        grid=(n // tm,),
        in_specs=[row(d), _mods_spec(tm, nb, rows_per_seq), _const_spec((1, d)),
                  _const_spec(w_perm.shape), _const_spec((1, 768)), _const_spec((1, 256)),
                  _const_spec(wuq_slot.shape)] + [tab_spec] * 6,
        out_specs=[row(1024), row(256), row(MLA_ROPE), row(RW_COLS)],
        out_shape=[jax.ShapeDtypeStruct((n, 1024), BF16), jax.ShapeDtypeStruct((n, 256), F32),
                   jax.ShapeDtypeStruct((n, MLA_ROPE), F32), jax.ShapeDtypeStruct((n, RW_COLS), F32)],
        compiler_params=_cparams(("parallel",)),
    )(x, mods, norm_g.reshape(1, d), w_perm, q_norm.reshape(1, 768), kv_norm.reshape(1, 256),
      wuq_slot, *tabs)


def _rope_tables(pos, tile_to):
    half = MLA_ROPE // 2
    freqs = ROPE_BASE ** (-jnp.arange(half, dtype=F32) / half)
    ang = pos.astype(F32)[:, None] * freqs[None, :]
    cos, sin = jnp.cos(ang), jnp.sin(ang)
    t = pos.shape[0]
    z = lambda w: jnp.zeros((t, w), F32)
    o = lambda w: jnp.ones((t, w), F32)
    cq = jnp.concatenate([o(64), cos, cos, z(32)], axis=1)
    s1q = jnp.concatenate([z(64), -sin, z(48)], axis=1)
    s2q = jnp.concatenate([z(80), sin, z(32)], axis=1)
    ck = jnp.concatenate([cos, cos, z(96)], axis=1)
    s1k = jnp.concatenate([-sin, z(112)], axis=1)
    s2k = jnp.concatenate([z(16), sin, z(96)], axis=1)
    tabs = [cq, s1q, s2q, ck, s1k, s2k]
    if tile_to > t:
        tabs = [jnp.tile(a, (tile_to // t, 1)) for a in tabs]
    return tabs


def _kv_expand_kernel(ckv_ref, kr_ref, wk_ref, sel_ref, wv_ref, one_ref, k_out, v_out):
    c = ckv_ref[...].astype(BF16)
    k = _dot(c, wk_ref[...]) + _dot(kr_ref[...].astype(BF16), sel_ref[...])
    k_out[...] = k.astype(BF16)
    v_out[0] = (_dot_nt(wv_ref[...], c) + one_ref[...]).astype(BF16)


def _kv_expand(ckv, kr, wk_slot, sel, wvt_slot, one_col, batch):
    n = ckv.shape[0]
    t_k = n // batch
    tm = 1024 if t_k % 1024 == 0 else 512
    assert t_k % tm == 0
    per = t_k // tm
    row = lambda w: pl.BlockSpec((tm, w), lambda i: (i, 0))
    return pl.pallas_call(
        _kv_expand_kernel,
        grid=(n // tm,),
        in_specs=[row(256), row(MLA_ROPE), _const_spec(wk_slot.shape), _const_spec(sel.shape),
                  _const_spec(wvt_slot.shape), _const_spec(one_col.shape)],
        out_specs=[row(1024), pl.BlockSpec((1, MLA_V_ROWS, tm), lambda i: (i // per, 0, i % per))],
        out_shape=[jax.ShapeDtypeStruct((n, 1024), BF16),
                   jax.ShapeDtypeStruct((batch, MLA_V_ROWS, t_k), BF16)],
        compiler_params=_cparams(("parallel",)),
    )(ckv, kr, wk_slot, sel, wvt_slot, one_col)


def _mla_attn_kernel(qi_ref, ki_ref, fl_ref, q_ref, k_ref, vt_ref, o_ref, m_scr, acc_scr,
                     *, tq, tk):
    p_id = pl.program_id(1)
    flags = fl_ref[p_id]
    first = (flags & 1) != 0
    last = (flags & 2) != 0
    masked = (flags & 4) != 0

    @pl.when(first)
    def _():
        m_scr[...] = jnp.full(m_scr.shape, NEG, F32)
        acc_scr[...] = jnp.zeros(acc_scr.shape, F32)

    qb = min(MLA_QUERY_BLOCK, tq)

    def scores_t(h, c0):
        sl = slice(h * LANES, (h + 1) * LANES)
        return _dot_nt(k_ref[0, :, sl], q_ref[0, c0:c0 + qb, sl])

    def body(use_mask, first_query=0):
        insts = [(h, c0) for h in range(MLA_HEADS) for c0 in range(first_query, tq, qb)]
        if use_mask:
            q_chunk0 = (qi_ref[p_id] * tq) // CHUNK
            k_chunk0 = (ki_ref[p_id] * tk) // CHUNK
            kc = k_chunk0 + lax.broadcasted_iota(jnp.int32, (tk, qb), 0) // CHUNK
            qc_local = lax.broadcasted_iota(jnp.int32, (tk, qb), 1) // CHUNK
        ahead = [scores_t(*insts[i]) for i in range(min(MLA_SCORES_AHEAD, len(insts)))]
        for idx, (h, c0) in enumerate(insts):
            s = ahead.pop(0)
            if idx + MLA_SCORES_AHEAD < len(insts):
                ahead.append(scores_t(*insts[idx + MLA_SCORES_AHEAD]))
            if use_mask:
                s = jnp.where(kc <= qc_local + (q_chunk0 + c0 // CHUNK), s, NEG)
            m_prev = m_scr[h:h + 1, c0:c0 + qb]
            m_new = jnp.maximum(m_prev, jnp.max(s, axis=0, keepdims=True))
            m_scr[h:h + 1, c0:c0 + qb] = m_new
            alpha = jnp.exp2(m_prev - m_new)
            p_t = jnp.exp2(s - m_new).astype(BF16)
            vt_h = vt_ref[0, h * MLA_V_SLOT:(h + 1) * MLA_V_SLOT, :]
            acc_scr[h, :, c0:c0 + qb] = alpha * acc_scr[h, :, c0:c0 + qb] + _dot(vt_h, p_t)

    diag = flags >> 3
    for j in range(max(tq // tk, 1)):
        @pl.when(masked & (diag == j))
        def _(j=j):
            body(True, first_query=j * tk)

    @pl.when(jnp.logical_not(masked))
    def _():
        body(False)

    @pl.when(last)
    def _():
        outs = []
        for h in range(MLA_HEADS):
            a = acc_scr[h]
            outs.append(a[0:MLA_V] / a[MLA_V:MLA_V + 1])
        o_ref[0] = jnp.concatenate(outs, axis=0).T.astype(BF16)


MLA_Q_TILE = 1024
MLA_KV_TILE = 512
MLA_QUERY_BLOCK = 256
MLA_SCORES_AHEAD = 3


def _mla_attn(q, k, vt, pairs, tq, tk):
    b, t_q, _ = q.shape
    qi, ki, fl = pairs
    grid_spec = pltpu.PrefetchScalarGridSpec(
        num_scalar_prefetch=3,
        grid=(b, qi.shape[0]),
        in_specs=[pl.BlockSpec((1, tq, 1024), lambda bb, p, qi, ki, fl: (bb, qi[p], 0)),
                  pl.BlockSpec((1, tk, 1024), lambda bb, p, qi, ki, fl: (bb, ki[p], 0)),
                  pl.BlockSpec((1, MLA_V_ROWS, tk), lambda bb, p, qi, ki, fl: (bb, 0, ki[p]))],
        out_specs=pl.BlockSpec((1, tq, 512), lambda bb, p, qi, ki, fl: (bb, qi[p], 0)),
        scratch_shapes=[pltpu.VMEM((MLA_HEADS, tq), F32),
                        pltpu.VMEM((MLA_HEADS, MLA_V_SLOT, tq), F32)])
    return pl.pallas_call(
        functools.partial(_mla_attn_kernel, tq=tq, tk=tk),
        grid_spec=grid_spec,
        out_shape=jax.ShapeDtypeStruct((b, t_q, 512), BF16),
        compiler_params=_cparams(("parallel", "arbitrary")),
    )(qi, ki, fl, q, k, vt)


def _causal_pairs(nq, ratio):
    qi, ki, fl = [], [], []
    for a in range(nq):
        n_kv = (a + 1) * ratio
        for c in range(n_kv):
            qi.append(a)
            ki.append(c)
            diag = max(c - a * ratio, 0)
            fl.append((1 if c == 0 else 0) | (2 if c == n_kv - 1 else 0) | (4 if c >= a * ratio else 0)
                      | (diag << 3))
    return tuple(jnp.asarray(np.array(z, np.int32)) for z in (qi, ki, fl))


def _mla_decode_kernel(q_ref, cc_ref, kc_ref, cn_ref, kn_ref, wka_ref, prope_ref, wvp_ref, o_ref,
                       qa_scr, qr_scr, m_scr, l_scr, acc_scr, *, n_cache_blocks, t):
    k_id = pl.program_id(1)

    @pl.when(k_id == 0)
    def _():
        for h in range(MLA_HEADS):
            qs = q_ref[0, :, h * LANES:(h + 1) * LANES]
            qa_scr[h * t:(h + 1) * t, :] = _dot(qs, wka_ref[h]).astype(BF16)
            qr_scr[h * t:(h + 1) * t, :] = _dot(qs, prope_ref[...]).astype(BF16)
        m_scr[...] = jnp.full(m_scr.shape, NEG, F32)
        l_scr[...] = jnp.zeros(l_scr.shape, F32)
        acc_scr[...] = jnp.zeros(acc_scr.shape, F32)

    def update(ckv, kr):
        cb = ckv.astype(BF16)
        s = _dot_nt(qa_scr[...], cb) + _dot_nt(qr_scr[:, 0:MLA_ROPE], kr.astype(BF16))
        m_prev = m_scr[...]
        m_new = jnp.maximum(m_prev, jnp.max(s, axis=-1, keepdims=True))
        alpha = jnp.exp2(m_prev - m_new)
        p = jnp.exp2(s - m_new[:, 0:1])
        l_scr[...] = alpha * l_scr[...] + jnp.sum(p, axis=-1, keepdims=True)
        m_scr[...] = m_new
        acc_scr[...] = jnp.concatenate([alpha, alpha], axis=1) * acc_scr[...] + _dot(p.astype(BF16), cb)

    @pl.when(k_id < n_cache_blocks)
    def _():
        update(cc_ref[0], kc_ref[0])

    @pl.when(k_id == n_cache_blocks)
    def _():
        update(cn_ref[0], kn_ref[0])
        l = l_scr[...]
        o_lat = (acc_scr[...] / jnp.concatenate([l, l], axis=1)).astype(BF16)
        for pr in range(MLA_HEADS // 2):
            oe = o_lat[(2 * pr) * t:(2 * pr + 1) * t]
            oo = o_lat[(2 * pr + 1) * t:(2 * pr + 2) * t]
            o_ref[0, :, pr * LANES:(pr + 1) * LANES] = (
                _dot(oe, wvp_ref[2 * pr]) + _dot(oo, wvp_ref[2 * pr + 1])).astype(BF16)


MLA_DECODE_KV_TILE = 2048


def _mla_decode(q, ckv_cache, kr_cache, layer, ckv_new, kr_new, wka, prope, wvp):
    b, t, _ = q.shape
    n_past = ckv_cache.shape[2]
    kb = math.gcd(n_past, MLA_DECODE_KV_TILE)
    ncb = n_past // kb
    rows = MLA_HEADS * t
    per_b = lambda shape: pl.BlockSpec((1,) + shape, lambda bb, k: (bb, 0, 0))
    cache = lambda w: pl.BlockSpec((None, 1, kb, w), lambda bb, k: (layer, bb, jnp.minimum(k, ncb - 1), 0))
    return pl.pallas_call(
        functools.partial(_mla_decode_kernel, n_cache_blocks=ncb, t=t),
        grid=(b, ncb + 1),
        in_specs=[per_b((t, 1024)), cache(256), cache(MLA_ROPE), per_b((t, 256)), per_b((t, MLA_ROPE)),
                  _const_spec(wka.shape), _const_spec(prope.shape), _const_spec(wvp.shape)],
        out_specs=per_b((t, 512)),
        out_shape=jax.ShapeDtypeStruct((b, t, 512), BF16),
        scratch_shapes=[pltpu.VMEM((rows, 256), BF16), pltpu.VMEM((rows, LANES), BF16),
                        pltpu.VMEM((rows, LANES), F32), pltpu.VMEM((rows, LANES), F32),
                        pltpu.VMEM((rows, 256), F32)],
        compiler_params=_cparams(("parallel", "arbitrary")),
    )(q, ckv_cache, kr_cache, ckv_new, kr_new, wka, prope, wvp)


RW_CHUNK_GROUP = 2
RW_TIME_TILE = 512


def _pair_sum(x, ones_bd):
    hi, lo = _split2(x)
    return _dot(hi, ones_bd) + _dot(lo, ones_bd)


def _rwkv_kernel(pr_ref, sh0_ref, s0_ref, mu_ref, w0_ref, wl_ref, a0_ref, al_ref, g2_ref,
                 kk_ref, ka_ref, rk_ref, lnw_ref, lnb_ref, y_ref, st_ref,
                 prev_scr, s_scr, rt_scr, at_scr, bt_scr, kt_scr, bv_scr, k2_scr, v_scr,
                 cum_scr, yc_scr, *, tt):
    t_id = pl.program_id(1)
    n_pairs = RW_HEADS // 2

    @pl.when(t_id == 0)
    def _():
        s_scr[...] = s0_ref[0]
        prev_scr[0:1, :] = sh0_ref[0]

    n_chunks = tt // CHUNK
    groups = [list(range(c0, min(c0 + RW_CHUNK_GROUP, n_chunks))) for c0 in range(0, n_chunks, RW_CHUNK_GROUP)]
    gate, bonus = {}, {}

    carried_prev = prev_scr[0:1, :]
    prev_scr[0:1, :] = pr_ref[0, tt - 1:tt, :]

    li = lax.broadcasted_iota(jnp.int32, (LANES, LANES), 0)
    lj = lax.broadcasted_iota(jnp.int32, (LANES, LANES), 1)
    ones_bd = jnp.where((li // RW_N) == (lj // RW_N), 1.0, 0.0).astype(BF16)

    def head_sum(x):
        return jnp.concatenate([_pair_sum(x[:, p * LANES:(p + 1) * LANES], ones_bd)
                                for p in range(n_pairs)], axis=1)

    def prep_phase(gi):
        r0 = groups[gi][0] * CHUNK
        gr = len(groups[gi]) * CHUNK
        rows = slice(r0, r0 + gr)
        pr = pr_ref[0, rows, :]
        before = carried_prev if gi == 0 else pr_ref[0, r0 - 1:r0, :]
        row = lax.broadcasted_iota(jnp.int32, (gr, 1), 0)
        prev = jnp.where(row == 0, before, pltpu.roll(pr, 1, axis=0))
        pm = pr + (prev - pr) * mu_ref[...]
        r = pm[:, 0:512]
        k = pm[:, 512:1024]
        v = pm[:, 1024:1536]
        wa = pm[:, 1536:1664]
        g_in = pm[:, 1664:1792]
        yield
        z = w0_ref[...] + _dot(jnp.tanh(wa).astype(BF16), wl_ref[...])
        nz = -z
        w = -(jnp.maximum(nz, 0.0) + jnp.log(1.0 + jnp.exp(-jnp.abs(nz)))) - 0.5
        ld = -jnp.exp(w)
        yield
        a_sig = jax.nn.sigmoid(a0_ref[...] + _dot(wa.astype(BF16), al_ref[...]))
        gate[gi] = _dot(jax.nn.sigmoid(g_in).astype(BF16), g2_ref[...])
        yield
        kk = k * kk_ref[...]
        kk = kk * lax.rsqrt(jnp.maximum(head_sum(kk * kk), 1e-24))
        k2 = k * (1.0 + (a_sig - 1.0) * ka_ref[...])
        yield
        bonus[gi] = head_sum(r * k2 * rk_ref[...]) * v
        yield
        ti = lax.broadcasted_iota(jnp.int32, (gr, gr), 0)
        tj = lax.broadcasted_iota(jnp.int32, (gr, gr), 1)
        tri = jnp.where(((ti // CHUNK) == (tj // CHUNK)) & (tj <= ti), 1.0, 0.0).astype(BF16)
        l1 = ld.astype(BF16)
        rem = ld - l1.astype(F32)
        l2 = rem.astype(BF16)
        l3 = (rem - l2.astype(F32)).astype(BF16)
        cum = (_dot(tri, l3) + _dot(tri, l2)) + _dot(tri, l1)
        yield
        winv = jnp.exp(-cum)
        rt_scr[rows, :] = r * jnp.exp(cum)
        at_scr[rows, :] = -kk * jnp.exp(cum - ld)
        yield
        bv = kk * a_sig
        bt_scr[rows, :] = bv * winv
        kt_scr[rows, :] = k2 * winv
        bv_scr[rows, :] = bv
        k2_scr[rows, :] = k2
        v_scr[rows, :] = v
        cum_scr[rows, :] = cum

    lane = lax.broadcasted_iota(jnp.int32, (CHUNK, LANES), 1)
    even = lane < RW_N
    strict = (lj % RW_N) < (li % RW_N)
    incl = (lj % RW_N) <= (li % RW_N)
    eye = jnp.where(li == lj, 1.0, 0.0).astype(F32)
    pairs = range(n_pairs)

    def same_block(m):
        return (li // m) == (lj // m)

    def stack_f32(x):
        return jnp.concatenate([jnp.where(even, x, 0.0), jnp.where(even, 0.0, x)], axis=0)

    def stack(x):
        return stack_f32(x).astype(BF16)

    mm = lambda a, b: _dot(a.astype(BF16), b.astype(BF16))
    rows_of = lambda c: slice(c * CHUNK, (c + 1) * CHUNK)
    lanes_of = lambda p: slice(p * LANES, (p + 1) * LANES)

    def independent_phase(chunks, res):
        insts = [(c, p) for c in chunks for p in pairs]
        load = lambda scr: [scr[rows_of(c), lanes_of(p)] for c, p in insts]
        cum_c = load(cum_scr)
        cum_l = [a[CHUNK - 1:CHUNK, :] for a in cum_c]
        w2 = [jnp.exp(a - b) for a, b in zip(cum_l, cum_c)]
        rs = [stack(a) for a in load(rt_scr)]
        as_ = [stack(a) for a in load(at_scr)]
        bs = [stack(a) for a in load(bt_scr)]
        ks = [stack(a) for a in load(kt_scr)]
        v_f = [stack_f32(a) for a in load(v_scr)]
        vs = [a.astype(BF16) for a in v_f]
        b2s = [stack(a * w) for a, w in zip(load(bv_scr), w2)]
        k2s = [stack(a * w) for a, w in zip(load(k2_scr), w2)]
        yield
        n_m = [jnp.where(strict, _dot_nt(a, b), 0.0) for a, b in zip(as_, bs)]
        mk = [jnp.where(strict, _dot_nt(a, b), 0.0).astype(BF16) for a, b in zip(as_, ks)]
        cb = [jnp.where(incl, _dot_nt(a, b), 0.0).astype(BF16) for a, b in zip(rs, bs)]
        ck = [jnp.where(incl, _dot_nt(a, b), 0.0).astype(BF16) for a, b in zip(rs, ks)]
        yield
        n8 = [jnp.where(same_block(8), n, 0.0) for n in n_m]
        t = [eye + a for a in n8]
        p2 = [mm(a, a) for a in n8]
        mv = [_dot(a, b) for a, b in zip(mk, vs)]
        yield
        t = [a + mm(a, b) for a, b in zip(t, p2)]
        p4 = [mm(a, a) for a in p2]
        cv = [_dot(a, b) for a, b in zip(ck, vs)]
        yield
        t = [a + mm(a, b) for a, b in zip(t, p4)]
        vk = [_dot(a.T.astype(BF16), b) for a, b in zip(v_f, k2s)]
        yield
        for m in (8, 16, 32):
            off = same_block(2 * m) & jnp.logical_not(same_block(m))
            x = [mm(jnp.where(off, n, 0.0), a) for n, a in zip(n_m, t)]
            yield
            t = [a + mm(a, b) for a, b in zip(t, x)]
            yield
        for i, key in enumerate(insts):
            res[key] = dict(as_=as_[i], rs=rs[i], b2s=b2s[i], cb=cb[i], mv=mv[i], cv=cv[i], vk=vk[i],
                            tinv=t[i].astype(BF16), wl=jnp.exp(cum_l[i]))

    def dependent_phase(chunks, res):
        for c in chunks:
            rc = [res[(c, p)] for p in pairs]
            s = [s_scr[p] for p in pairs]
            sb = [a.astype(BF16) for a in s]
            x = [_dot_nt(r_['as_'], b) + r_['mv'] for r_, b in zip(rc, sb)]
            rs_s = [_dot_nt(r_['rs'], b) + r_['cv'] for r_, b in zip(rc, sb)]
            yield
            u = [_dot(r_['tinv'], a.astype(BF16)) for r_, a in zip(rc, x)]
            yield
            ys = [a + _dot(r_['cb'], b.astype(BF16)) for a, r_, b in zip(rs_s, rc, u)]
            for p in pairs:
                yc_scr[rows_of(c), lanes_of(p)] = ys[p][0:CHUNK] + ys[p][CHUNK:2 * CHUNK]
                s_scr[p] = s[p] * rc[p]['wl'] + _dot(u[p].T.astype(BF16), rc[p]['b2s']) + rc[p]['vk']
            yield

    def emit(*gens):
        gens = list(gens)
        while gens:
            for gen in list(gens):
                try:
                    next(gen)
                except StopIteration:
                    gens.remove(gen)

    def output_phase(gi):
        r0 = groups[gi][0] * CHUNK
        rows = slice(r0, r0 + len(groups[gi]) * CHUNK)
        y = yc_scr[rows, :]
        mean = head_sum(y) * (1.0 / RW_N)
        dlt = y - mean
        yield
        var = head_sum(dlt * dlt) * (1.0 / RW_N)
        yield
        yn = dlt * lax.rsqrt(var + RW_GN_EPS) * lnw_ref[...] + lnb_ref[...]
        y_ref[0, rows, :] = ((yn + bonus[gi]) * gate[gi]).astype(BF16)

    res = {}
    n_g = len(groups)
    for step in range(n_g + 3):
        live = []
        if step < n_g:
            live.append(prep_phase(step))
        if 0 <= step - 1 < n_g:
            live.append(independent_phase(groups[step - 1], res))
        if 0 <= step - 2 < n_g:
            live.append(dependent_phase(groups[step - 2], res))
        if 0 <= step - 3 < n_g:
            live.append(output_phase(step - 3))
        emit(*live)

    @pl.when(t_id == pl.num_programs(1) - 1)
    def _():
        st_ref[0] = s_scr[...]


def _rwkv(prw, sh0, s0_bd, wts):
    b, t, _ = prw.shape
    tt = min(RW_TIME_TILE, t)
    assert t % tt == 0 and tt % CHUNK == 0
    c512 = _const_spec((1, RW_C))
    scr = lambda: pltpu.VMEM((tt, RW_C), F32)
    return pl.pallas_call(
        functools.partial(_rwkv_kernel, tt=tt),
        grid=(b, t // tt),
        in_specs=[pl.BlockSpec((1, tt, RW_COLS), lambda bb, i: (bb, i, 0)),
                  pl.BlockSpec((1, 1, RW_COLS), lambda bb, i: (bb, 0, 0)),
                  pl.BlockSpec((1, 4, LANES, LANES), lambda bb, i: (bb, 0, 0, 0)),
                  _const_spec((1, RW_COLS)), c512, _const_spec((LANES, RW_C)), c512,
                  _const_spec((LANES, RW_C)), _const_spec((LANES, RW_C)),
                  c512, c512, c512, c512, c512],
        out_specs=[pl.BlockSpec((1, tt, RW_C), lambda bb, i: (bb, i, 0)),
                   pl.BlockSpec((1, 4, LANES, LANES), lambda bb, i: (bb, 0, 0, 0))],
        out_shape=[jax.ShapeDtypeStruct((b, t, RW_C), BF16),
                   jax.ShapeDtypeStruct((b, 4, LANES, LANES), F32)],
        scratch_shapes=[pltpu.VMEM((8, RW_COLS), F32), pltpu.VMEM((4, LANES, LANES), F32)]
                       + [scr() for _ in range(9)],
        compiler_params=_cparams(("parallel", "arbitrary")),
    )(prw, sh0, s0_bd, *wts)


def _state_to_bd(s):
    b = s.shape[0]
    s = s.reshape(b, 4, 2, RW_N, RW_N)
    z = jnp.zeros_like(s[:, :, 0])
    top = jnp.concatenate([s[:, :, 0], z], axis=-1)
    bot = jnp.concatenate([z, s[:, :, 1]], axis=-1)
    return jnp.concatenate([top, bot], axis=-2)


def _state_from_bd(s):
    b = s.shape[0]
    return jnp.stack([s[:, :, :RW_N, :RW_N], s[:, :, RW_N:, RW_N:]], axis=2).reshape(b, RW_HEADS, RW_N, RW_N)


_ODD_SEGS = ([(1024 + g * 64, 1088 + g * 64) for g in range(SW_KV_HEADS) for _ in range(2)]
             + [(1280 + g * 64, 1344 + g * 64) for g in range(SW_KV_HEADS) for _ in range(2)])


def _odd_in_kernel(x_ref, mods_ref, ng_ref, w_ref, b_ref, q_out, k_out, v_out, *, nb):
    h = _norm_mod(x_ref[...], ng_ref[...], mods_ref[:, 3, :], mods_ref[:, 4, :], nb).astype(BF16)
    q = _dot(h, w_ref[:, 0:1024]) + b_ref[:, 0:1024]
    q_out[...] = (q * (SW_HD ** -0.5 * LOG2E)).astype(BF16)
    k_out[...] = _dot(h, w_ref[:, 1024:1536]) + b_ref[:, 1024:1536]
    v_out[...] = _dot(h, w_ref[:, 1536:2048]) + b_ref[:, 1536:2048]


def _odd_in(x, mods, norm_g, w_perm, b_perm, rows_per_seq):
    n, d = x.shape
    tm, nb = _row_tiling(n, rows_per_seq, 512)
    row = lambda w: pl.BlockSpec((tm, w), lambda i: (i, 0))
    return pl.pallas_call(
        functools.partial(_odd_in_kernel, nb=nb),
        grid=(n // tm,),
        in_specs=[row(d), _mods_spec(tm, nb, rows_per_seq), _const_spec((1, d)),
                  _const_spec(w_perm.shape), _const_spec((1, 2048))],
        out_specs=[row(1024), row(512), row(512)],
        out_shape=[jax.ShapeDtypeStruct((n, 1024), BF16), jax.ShapeDtypeStruct((n, 512), F32),
                   jax.ShapeDtypeStruct((n, 512), F32)],
        compiler_params=_cparams(("parallel",)),
    )(x, mods, norm_g.reshape(1, d), w_perm, b_perm.reshape(1, 2048))


def _swa_kernel(sinks_ref, q_ref, kp_ref, kc_ref, vp_ref, vc_ref, o_ref, *, tq, mask_first_prev):
    nk = WINDOW + tq
    qi = lax.broadcasted_iota(jnp.int32, (tq, nk), 0)
    kj = lax.broadcasted_iota(jnp.int32, (tq, nk), 1)
    kc = kj // CHUNK - WINDOW // CHUNK
    qc = qi // CHUNK
    vis = (kc <= qc) & (kc >= qc - WINDOW // CHUNK)
    if mask_first_prev:
        vis = vis & ((kj >= WINDOW) | (pl.program_id(1) > 0))
    ndist = jnp.where(vis, -jnp.abs(qi + WINDOW - kj).astype(F32), NEG)
    keys = jnp.concatenate([kp_ref[0], kc_ref[0]], axis=0).astype(BF16)
    vals = jnp.concatenate([vp_ref[0], vc_ref[0]], axis=0)
    klane = lax.broadcasted_iota(jnp.int32, (nk, LANES), 1)
    v_slots = [jnp.where(klane < SW_HD, vals[:, g * LANES:(g + 1) * LANES], 1.0).astype(BF16)
               for g in range(SW_KV_HEADS)]
    lane = lax.broadcasted_iota(jnp.int32, (tq, LANES), 1)
    low = lane < SW_HD
    sb = min(SWA_SUB_BLOCK, tq)

    def scores(h):
        qp = q_ref[0, :, (h // 2) * LANES:(h // 2 + 1) * LANES]
        qh = jnp.where(low if h % 2 == 0 else jnp.logical_not(low), qp, jnp.zeros_like(qp))
        return _dot_nt(qh, keys[:, (h // SW_GROUP) * LANES:(h // SW_GROUP + 1) * LANES])

    outs = []
    ahead = [scores(h) for h in range(SWA_SCORES_AHEAD)]
    for h in range(SW_HEADS):
        s = ahead.pop(0)
        if h + SWA_SCORES_AHEAD < SW_HEADS:
            ahead.append(scores(h + SWA_SCORES_AHEAD))
        slope = (2.0 ** (-8.0 * (h + 1) / SW_HEADS)) * LOG2E
        sk = sinks_ref[h] * LOG2E
        es, ms = [], []
        for r1 in range(0, tq, sb):
            z = s[r1:r1 + sb] + slope * ndist[r1:r1 + sb]
            m = jnp.maximum(jnp.broadcast_to(jnp.max(z, axis=-1, keepdims=True), (sb, LANES)), sk)
            m_wide = jnp.concatenate([m] * (nk // LANES), axis=1) if nk % LANES == 0 else m[:, 0:1]
            es.append(jnp.exp2(z - m_wide).astype(BF16))
            ms.append(m)
        e = jnp.concatenate(es, axis=0) if len(es) > 1 else es[0]
        m = jnp.concatenate(ms, axis=0) if len(ms) > 1 else ms[0]
        pv = _dot(e, v_slots[h // SW_GROUP])
        outs.append(pv / (pltpu.roll(pv, SW_HD, axis=1) + jnp.exp2(sk - m)))
    for pr in range(SW_HEADS // 2):
        o_ref[0, :, pr * LANES:(pr + 1) * LANES] = jnp.where(
            low, outs[2 * pr], pltpu.roll(outs[2 * pr + 1], SW_HD, axis=1)).astype(BF16)


SWA_Q_TILE = 256
SWA_SUB_BLOCK = 32
SWA_SCORES_AHEAD = 2


def _swa(q, k_prev, k_cur, v_prev, v_cur, sinks, tq, same_array):
    b, t, _ = q.shape
    nt = t // tq
    per = tq // WINDOW
    if same_array:
        prev_map = lambda bb, i: (bb, jnp.maximum(i * per - 1, 0), 0)
    else:
        prev_map = lambda bb, i: (bb, 0, 0)
    cur = lambda w: pl.BlockSpec((1, tq, w), lambda bb, i: (bb, i, 0))
    prev = pl.BlockSpec((1, WINDOW, 512), prev_map)
    return pl.pallas_call(
        functools.partial(_swa_kernel, tq=tq, mask_first_prev=same_array),
        grid=(b, nt),
        in_specs=[pl.BlockSpec(memory_space=pltpu.SMEM), cur(1024), prev, cur(512), prev, cur(512)],
        out_specs=cur(1024),
        out_shape=jax.ShapeDtypeStruct((b, t, 1024), BF16),
        compiler_params=_cparams(("parallel", "parallel")),
    )(sinks, q, k_prev, k_cur, v_prev, v_cur)


def _undup(a):
    return a.reshape(a.shape[:-1] + (SW_KV_HEADS, 2, SW_HD))[..., 0, :]


def _dup(a):
    return jnp.concatenate([a, a], axis=-1).reshape(a.shape[:-2] + (512,))


def _prep_weights(p):
    depth = p['w_ada'].shape[0]
    n_even, n_odd = (depth + 1) // 2, depth // 2
    w = {}
    w['ffn_in'] = p['ffn_w_in'].astype(BF16)
    w['ffn_out'] = p['ffn_w_out'].astype(BF16)
    wi = p['even_w_in'].astype(BF16)
    w['even_in'] = jnp.concatenate(
        [wi[:, :, 0:1024]] + [wi[:, :, _RW_OFF + a:_RW_OFF + b] for a, b in _RW_SEGS]
        + [wi[:, :, 1024:1056], jnp.zeros((n_even, 1024, 96), BF16)], axis=2)
    w['wuq'] = jnp.pad(p['mla_w_uq'], ((0, 0), (0, 0), (0, 0), (0, 32))).reshape(n_even, 768, 1024).astype(BF16)
    w['wk'] = jnp.pad(p['mla_w_ukv'][..., :MLA_NOPE], ((0, 0), (0, 0), (0, 0), (0, 64))
                      ).reshape(n_even, 256, 1024).astype(BF16)
    wv_t = jnp.transpose(p['mla_w_ukv'][..., MLA_NOPE:], (0, 2, 3, 1))
    w['wv'] = jnp.pad(wv_t, ((0, 0), (0, 0), (0, MLA_V_SLOT - MLA_V), (0, 0))
                      ).reshape(n_even, MLA_V_ROWS, 256).astype(BF16)
    sel = np.zeros((MLA_ROPE, 1024), np.float32)
    one = np.zeros((MLA_V_ROWS, 1), np.float32)
    for h in range(MLA_HEADS):
        sel[np.arange(MLA_ROPE), h * LANES + MLA_NOPE + np.arange(MLA_ROPE)] = 1.0
        one[h * MLA_V_SLOT + MLA_V, 0] = 1.0
    w['sel'] = jnp.asarray(sel).astype(BF16)
    w['one'] = jnp.asarray(one)
    wk_t = jnp.transpose(p['mla_w_ukv'][..., :MLA_NOPE], (0, 2, 3, 1))
    w['wka'] = jnp.pad(wk_t, ((0, 0), (0, 0), (0, 64), (0, 0))).astype(BF16)
    prope = np.zeros((LANES, LANES), np.float32)
    prope[MLA_NOPE + np.arange(MLA_ROPE), np.arange(MLA_ROPE)] = 1.0
    w['prope'] = jnp.asarray(prope).astype(BF16)
    wv_h = jnp.transpose(p['mla_w_ukv'][..., MLA_NOPE:], (0, 2, 1, 3))
    wv_even = jnp.pad(wv_h[:, 0::2], ((0, 0), (0, 0), (0, 0), (0, 64)))
    wv_odd = jnp.pad(wv_h[:, 1::2], ((0, 0), (0, 0), (0, 0), (64, 0)))
    w['wvp'] = jnp.stack([wv_even, wv_odd], axis=2).reshape(n_even, MLA_HEADS, 256, LANES).astype(BF16)
    z64 = jnp.zeros((n_even, 64, RW_C), F32)
    w['wl'] = jnp.concatenate([p['rw_w2'], z64], axis=1).astype(BF16)
    w['al'] = jnp.concatenate([z64, p['rw_a2']], axis=1).astype(BF16)
    w['g2'] = p['rw_g2'].astype(BF16)
    w['mu'] = _rw_permute(p['rw_mu'])
    w['even_out'] = p['even_w_out'].astype(BF16)
    wo = p['odd_w_qkv'].astype(BF16)
    w['odd_in'] = jnp.concatenate([wo[:, :, 0:1024]] + [wo[:, :, a:b] for a, b in _ODD_SEGS], axis=2)
    bo = p['odd_b_qkv']
    w['odd_b'] = jnp.concatenate([bo[:, 0:1024]] + [bo[:, a:b] for a, b in _ODD_SEGS], axis=1)
    w['odd_out'] = p['odd_w_out'].astype(BF16)
    return w


def _trunk(x3, mods_all, start, past, p, w):
    b, t, d = x3.shape
    n = b * t
    depth = mods_all.shape[0]
    x = x3.reshape(n, d)
    rows = t
    tm_even, _ = _row_tiling(n, rows, 512)
    pos = start + jnp.arange(t)
    tabs = _rope_tables(pos, tm_even)
    even_states, odd_states = [], []
    for i in range(depth):
        mods = mods_all[i]
        j = i // 2
        x = _ffn(x, mods, p['norm_g'][i, 0], w['ffn_in'], w['ffn_out'], i, 0, 0, rows)
        if i % 2 == 0:
            q, ckv, kr, prw = _even_in(x, mods, p['norm_g'][i, 1], w['even_in'][j], p['mla_q_norm'][j],
                                       p['mla_kv_norm'][j], w['wuq'][j], tabs, rows)
            if past is None:
                kx, vtx = _kv_expand(ckv, kr, w['wk'][j], w['sel'], w['wv'][j], w['one'], b)
                tk = min(MLA_KV_TILE, t)
                tq = min(MLA_Q_TILE, t)
                att = _mla_attn(q.reshape(b, t, 1024), kx.reshape(b, t, 1024), vtx,
                                _causal_pairs(t // tq, tq // tk), tq, tk)
                s0 = jnp.zeros((b, RW_HEADS, RW_N, RW_N), F32)
                sh0 = jnp.zeros((b, RW_COLS), F32)
            else:
                s0, sh0 = past[2][j], past[3][j]
                att = _mla_decode(q.reshape(b, t, 1024), past[0], past[1], j, ckv.reshape(b, t, 256),
                                  kr.reshape(b, t, MLA_ROPE), w['wka'][j], w['prope'], w['wvp'][j])
            rw_wts = (w['mu'][j].reshape(1, RW_COLS), p['rw_w0'][j].reshape(1, RW_C), w['wl'][j],
                      p['rw_a0'][j].reshape(1, RW_C), w['al'][j], w['g2'][j],
                      p['rw_k_k'][j].reshape(1, RW_C), p['rw_k_a'][j].reshape(1, RW_C),
                      p['rw_r_k'][j].reshape(1, RW_C), p['rw_ln_w'][j].reshape(1, RW_C),
                      p['rw_ln_b'][j].reshape(1, RW_C))
            prw3 = prw.reshape(b, t, RW_COLS)
            y_rw, s_bd = _rwkv(prw3, _rw_permute(sh0).reshape(b, 1, RW_COLS), _state_to_bd(s0), rw_wts)
            mix = ((att.reshape(n, 512), w['even_out'][j][:512]), (y_rw.reshape(n, RW_C), w['even_out'][j][512:]))
            even_states.append((ckv.reshape(b, t, 256), kr.reshape(b, t, MLA_ROPE), _state_from_bd(s_bd),
                                _rw_unpermute(prw3[:, t - 1, :])))
        else:
            q, kd, vd = _odd_in(x, mods, p['norm_g'][i, 1], w['odd_in'][j], w['odd_b'][j], rows)
            q3, kd3, vd3 = q.reshape(b, t, 1024), kd.reshape(b, t, 512), vd.reshape(b, t, 512)
            if past is None:
                tq = min(SWA_Q_TILE, t)
                o = _swa(q3, kd3, kd3, vd3, vd3, p['swa_sinks'][j], tq, True)
                keep = min(WINDOW, t)
                k_new, v_new = _undup(kd3[:, t - keep:]), _undup(vd3[:, t - keep:])
            else:
                k_past, v_past = past[4][j], past[5][j]
                o = _swa(q3, _dup(k_past), kd3, _dup(v_past), vd3, p['swa_sinks'][j], t, False)
                k_new = jnp.concatenate([k_past, _undup(kd3)], axis=1)[:, t:]
                v_new = jnp.concatenate([v_past, _undup(vd3)], axis=1)[:, t:]
            mix = ((o.reshape(n, 1024), w['odd_out'][j]),)
            odd_states.append((k_new, v_new))
        fg = p['final_norm_g'] if i == depth - 1 else None
        x = _ffn(x, mods, p['norm_g'][i, 2], w['ffn_in'], w['ffn_out'], i, 1, 2, rows, final_g=fg, mix=mix)
    es = [jnp.stack([st[k] for st in even_states]) for k in range(4)]
    os_ = [jnp.stack([st[k] for st in odd_states]) for k in range(2)]
    return x.reshape(b, t, d), es + os_


def kernel(x_prompt, x_sample, cache_mla_ckv, cache_mla_krope, state_rwkv, state_rwkv_shift, cache_swa_k, cache_swa_v, c_prompt, c_sample, w_ada, b_ada, norm_g, ffn_w_in, ffn_w_out, even_w_in, even_w_out, mla_q_norm, mla_kv_norm, mla_w_uq, mla_w_ukv, rw_mu, rw_w0, rw_w2, rw_a0, rw_a2, rw_g2, rw_k_k, rw_k_a, rw_r_k, rw_ln_w, rw_ln_b, odd_w_qkv, odd_b_qkv, odd_w_out, swa_sinks, final_norm_g):
    p = dict(w_ada=w_ada, b_ada=b_ada, norm_g=norm_g, ffn_w_in=ffn_w_in, ffn_w_out=ffn_w_out,
             even_w_in=even_w_in, even_w_out=even_w_out, mla_q_norm=mla_q_norm, mla_kv_norm=mla_kv_norm,
             mla_w_uq=mla_w_uq, mla_w_ukv=mla_w_ukv, rw_mu=rw_mu, rw_w0=rw_w0, rw_w2=rw_w2, rw_a0=rw_a0,
             rw_a2=rw_a2, rw_g2=rw_g2, rw_k_k=rw_k_k, rw_k_a=rw_k_a, rw_r_k=rw_r_k, rw_ln_w=rw_ln_w,
             rw_ln_b=rw_ln_b, odd_w_qkv=odd_w_qkv, odd_b_qkv=odd_b_qkv, odd_w_out=odd_w_out,
             swa_sinks=swa_sinks, final_norm_g=final_norm_g)
    w = _prep_weights(p)
    depth = w_ada.shape[0]
    bp, bs = c_prompt.shape[0], c_sample.shape[0]
    d = c_prompt.shape[1]
    b_pad = -(-(bp + bs) // 8) * 8
    c_all = jnp.concatenate([c_prompt, c_sample, jnp.zeros((b_pad - bp - bs, d), F32)], axis=0)
    mods = _ada(c_all, w_ada, b_ada).reshape(depth, b_pad, 3 * N_SUB, d)
    y_prompt, sp = _trunk(x_prompt, mods[:, :bp], 0, None, p, w)
    past = (cache_mla_ckv, cache_mla_krope, state_rwkv, state_rwkv_shift, cache_swa_k, cache_swa_v)
    y_sample, ss = _trunk(x_sample, mods[:, bp:bp + bs], cache_mla_ckv.shape[2], past, p, w)
    return (y_prompt, y_sample, sp[0], sp[1], sp[2], sp[3], sp[4], sp[5],
            ss[0], ss[1], ss[2], ss[3], ss[4], ss[5])
```

```python
import functools
import math

import jax
import jax.numpy as jnp
import numpy as np
from jax import lax
from jax.experimental import pallas as pl
from jax.experimental.pallas import tpu as pltpu

F32 = jnp.float32
BF16 = jnp.bfloat16

CHUNK = 64
EPS = 1e-6
NEG = -1e30
N_SUB = 3
MLA_HEADS = 8
MLA_NOPE = 64
MLA_ROPE = 32
MLA_V = 64
MLA_Q_LORA = 768
MLA_KV_LORA = 256
MLA_SCALE = (MLA_NOPE + MLA_ROPE) ** -0.5
MLA_V_SLOT = 128
MLA_V_ROWS = MLA_HEADS * MLA_V_SLOT
LOG2E = math.log2(math.e)
ROPE_BASE = 10000.0
RW_HEADS = 8
RW_N = 64
RW_C = RW_HEADS * RW_N
RW_GN_EPS = 64e-5
RW_COLS = 3 * RW_C + 64 + 64 + 128
SW_HEADS = 16
SW_KV_HEADS = 4
SW_GROUP = 4
SW_HD = 64
WINDOW = 128

LANES = 128
VMEM_LIMIT = 56 * 1024 * 1024


def _cparams(sem):
    return pltpu.CompilerParams(dimension_semantics=sem, vmem_limit_bytes=VMEM_LIMIT)


def _const_spec(shape):
    nd = len(shape)
    return pl.BlockSpec(shape, lambda *_: (0,) * nd, pipeline_mode=pl.Buffered(1))


def _dot(a, b):
    return jnp.dot(a, b, preferred_element_type=F32)


def _dot_nt(a, b):
    return lax.dot_general(a, b, (((1,), (1,)), ((), ())), preferred_element_type=F32)


def _split2(x):
    hi = x.astype(BF16)
    lo = (x - hi.astype(F32)).astype(BF16)
    return hi, lo


def _norm_mod(x, g, sh, sc, nb):
    y = x * lax.rsqrt(jnp.mean(x * x, axis=-1, keepdims=True) + EPS) * g
    if nb == 1:
        return y * (1.0 + sc) + sh
    tm, d = x.shape
    y3 = y.reshape(nb, tm // nb, d)
    return (y3 * (1.0 + sc[:, None, :]) + sh[:, None, :]).reshape(tm, d)


def _gate_rows(g, y, nb):
    if nb == 1:
        return g * y
    tm, d = y.shape
    return (y.reshape(nb, tm // nb, d) * g[:, None, :]).reshape(tm, d)


def _row_tiling(n_rows, rows_per_seq, pref):
    if rows_per_seq >= pref:
        assert rows_per_seq % pref == 0
        return pref, 1
    tm = min(pref, n_rows)
    assert tm % rows_per_seq == 0 and n_rows % tm == 0
    return tm, tm // rows_per_seq


def _mods_spec(tm, nb, rows_per_seq):
    if nb == 1:
        tiles_per_seq = rows_per_seq // tm
        return pl.BlockSpec((1, 3 * N_SUB, 1024), lambda i: (i // tiles_per_seq, 0, 0))
    return pl.BlockSpec((nb, 3 * N_SUB, 1024), lambda i: (i, 0, 0))


def _ada_kernel(c_ref, w_ref, b_ref, o_ref):
    c = c_ref[...]
    cs = (c * jax.nn.sigmoid(c)).astype(BF16)
    o_ref[0] = _dot(cs, w_ref[0].astype(BF16)) + b_ref[0]


def _ada(c_all, w_ada, b_ada):
    depth, d, n = w_ada.shape
    bp = c_all.shape[0]
    tn = n // 4
    return pl.pallas_call(
        _ada_kernel,
        grid=(depth, n // tn),
        in_specs=[pl.BlockSpec((bp, d), lambda l, j: (0, 0)),
                  pl.BlockSpec((1, d, tn), lambda l, j: (l, 0, j)),
                  pl.BlockSpec((1, 1, tn), lambda l, j: (l, 0, j))],
        out_specs=pl.BlockSpec((1, bp, tn), lambda l, j: (l, 0, j)),
        out_shape=jax.ShapeDtypeStruct((depth, bp, n), F32),
        compiler_params=_cparams(("parallel", "parallel")),
    )(c_all, w_ada, b_ada.reshape(depth, 1, n))


FF_CHUNK = 256


def _ffn_kernel(x_ref, mods_ref, ng_ref, win_ref, wout_ref, *rest, sub, nb, d_ff, final, n_mix):
    mix_a, mix_w, rest = rest[:n_mix], rest[n_mix:2 * n_mix], rest[2 * n_mix:]
    if final:
        fg_ref, o_ref, a_scr = rest
    else:
        o_ref, a_scr = rest
    x = x_ref[...]
    if n_mix:
        ym = _dot(mix_a[0][...], mix_w[0][...])
        for a, w in zip(mix_a[1:], mix_w[1:]):
            ym = ym + _dot(a[...], w[...])
        x = x + _gate_rows(mods_ref[:, 5, :], ym, nb)
    sh = mods_ref[:, 3 * sub, :]
    sc = mods_ref[:, 3 * sub + 1, :]
    gt = mods_ref[:, 3 * sub + 2, :]
    h = _norm_mod(x, ng_ref[...], sh, sc, nb).astype(BF16)
    for c in range(d_ff // FF_CHUNK):
        lo = c * FF_CHUNK
        g = _dot(h, win_ref[:, lo:lo + FF_CHUNK])
        u = _dot(h, win_ref[:, d_ff + lo:d_ff + lo + FF_CHUNK])
        a_scr[:, lo:lo + FF_CHUNK] = (g * jax.nn.sigmoid(g) * u).astype(BF16)
    y = _dot(a_scr[...], wout_ref[...])
    out = x + _gate_rows(0.5 * gt, y, nb)
    if final:
        out = out * lax.rsqrt(jnp.mean(out * out, axis=-1, keepdims=True) + EPS) * fg_ref[...]
    o_ref[...] = out


def _ffn(x, mods, norm_g, w_in, w_out, layer, half, sub, rows_per_seq, final_g=None, mix=()):
    n, d = x.shape
    d_ff = w_out.shape[2]
    tm, nb = _row_tiling(n, rows_per_seq, 512)
    final = final_g is not None
    stacked = lambda a: pl.BlockSpec((None, None) + a.shape[2:], lambda i: (layer, half, 0, 0),
                                     pipeline_mode=pl.Buffered(1))
    in_specs = [pl.BlockSpec((tm, d), lambda i: (i, 0)),
                _mods_spec(tm, nb, rows_per_seq),
                _const_spec((1, d)), stacked(w_in), stacked(w_out)]
    args = [x, mods, norm_g.reshape(1, d), w_in, w_out]
    in_specs += [pl.BlockSpec((tm, a.shape[1]), lambda i: (i, 0)) for a, _ in mix]
    in_specs += [_const_spec(wm.shape) for _, wm in mix]
    args += [a for a, _ in mix] + [wm for _, wm in mix]
    if final:
        in_specs.append(_const_spec((1, d)))
        args.append(final_g.reshape(1, d))
    return pl.pallas_call(
        functools.partial(_ffn_kernel, sub=sub, nb=nb, d_ff=d_ff, final=final, n_mix=len(mix)),
        grid=(n // tm,),
        in_specs=in_specs,
        out_specs=pl.BlockSpec((tm, d), lambda i: (i, 0)),
        out_shape=jax.ShapeDtypeStruct((n, d), F32),
        scratch_shapes=[pltpu.VMEM((tm, d_ff), BF16)],
        compiler_params=_cparams(("parallel",)),
    )(*args)


_RW_OFF = 1056
_RW_SEGS = [(0, 512), (576, 1088), (1088, 1600), (512, 576), (1600, 1664), (1664, 1792)]
_RW_INV_SEGS = [(0, 512), (1536, 1600), (512, 1024), (1024, 1536), (1600, 1664), (1664, 1792)]


def _rw_permute(a):
    return jnp.concatenate([a[..., s:e] for s, e in _RW_SEGS], axis=-1)


def _rw_unpermute(a):
    return jnp.concatenate([a[..., s:e] for s, e in _RW_INV_SEGS], axis=-1)


def _rope_slot(v, c, s1, s2):
    w = v.shape[-1]
    return v * c + pltpu.roll(v, w - 16, axis=1) * s1 + pltpu.roll(v, 16, axis=1) * s2


def _even_in_kernel(x_ref, mods_ref, ng_ref, w_ref, qn_ref, kvn_ref, wuq_ref,
                    cq_ref, s1q_ref, s2q_ref, ck_ref, s1k_ref, s2k_ref,
                    q_out, ckv_out, kr_out, prw_out, *, nb):
    x = x_ref[...]
    h = _norm_mod(x, ng_ref[...], mods_ref[:, 3, :], mods_ref[:, 4, :], nb).astype(BF16)
    cq = _dot(h, w_ref[:, 0:768])
    cqn = (cq * lax.rsqrt(jnp.mean(cq * cq, axis=-1, keepdims=True) + EPS) * qn_ref[...]).astype(BF16)
    q = _dot(cqn, wuq_ref[...])
    rep = lambda t: jnp.concatenate([t] * MLA_HEADS, axis=1)
    q = _rope_slot(q, rep(cq_ref[...]), rep(s1q_ref[...]), rep(s2q_ref[...]))
    q_out[...] = (q * (MLA_SCALE * LOG2E)).astype(BF16)
    ckv = _dot(h, w_ref[:, 768:1024])
    ckv_out[...] = ckv * lax.rsqrt(jnp.mean(ckv * ckv, axis=-1, keepdims=True) + EPS) * kvn_ref[...]
    prw_out[...] = _dot(h, w_ref[:, 1024:2816])
    krs = _dot(h, w_ref[:, 2816:2944])
    krs = _rope_slot(krs, ck_ref[...], s1k_ref[...], s2k_ref[...])
    kr_out[...] = krs[:, 0:MLA_ROPE]


def _even_in(x, mods, norm_g, w_perm, q_norm, kv_norm, wuq_slot, tabs, rows_per_seq):
    n, d = x.shape
    tm, nb = _row_tiling(n, rows_per_seq, 512)
    ttab = tabs[0].shape[0]
    ntab = ttab // tm
    tab_spec = pl.BlockSpec((tm, LANES), lambda i: (i % ntab, 0))
    row = lambda w: pl.BlockSpec((tm, w), lambda i: (i, 0))
    return pl.pallas_call(
        functools.partial(_even_in_kernel, nb=nb),
        grid=(n // tm,),
        in_specs=[row(d), _mods_spec(tm, nb, rows_per_seq), _const_spec((1, d)),
                  _const_spec(w_perm.shape), _const_spec((1, 768)), _const_spec((1, 256)),
                  _const_spec(wuq_slot.shape)] + [tab_spec] * 6,
        out_specs=[row(1024), row(256), row(MLA_ROPE), row(RW_COLS)],
        out_shape=[jax.ShapeDtypeStruct((n, 1024), BF16), jax.ShapeDtypeStruct((n, 256), F32),
                   jax.ShapeDtypeStruct((n, MLA_ROPE), F32), jax.ShapeDtypeStruct((n, RW_COLS), F32)],
        compiler_params=_cparams(("parallel",)),
    )(x, mods, norm_g.reshape(1, d), w_perm, q_norm.reshape(1, 768), kv_norm.reshape(1, 256),
      wuq_slot, *tabs)


def _rope_tables(pos, tile_to):
    half = MLA_ROPE // 2
    freqs = ROPE_BASE ** (-jnp.arange(half, dtype=F32) / half)
    ang = pos.astype(F32)[:, None] * freqs[None, :]
    cos, sin = jnp.cos(ang), jnp.sin(ang)
    t = pos.shape[0]
    z = lambda w: jnp.zeros((t, w), F32)
    o = lambda w: jnp.ones((t, w), F32)
    cq = jnp.concatenate([o(64), cos, cos, z(32)], axis=1)
    s1q = jnp.concatenate([z(64), -sin, z(48)], axis=1)
    s2q = jnp.concatenate([z(80), sin, z(32)], axis=1)
    ck = jnp.concatenate([cos, cos, z(96)], axis=1)
    s1k = jnp.concatenate([-sin, z(112)], axis=1)
    s2k = jnp.concatenate([z(16), sin, z(96)], axis=1)
    tabs = [cq, s1q, s2q, ck, s1k, s2k]
    if tile_to > t:
        tabs = [jnp.tile(a, (tile_to // t, 1)) for a in tabs]
    return tabs


def _kv_expand_kernel(ckv_ref, kr_ref, wk_ref, sel_ref, wv_ref, one_ref, k_out, v_out):
    c = ckv_ref[...].astype(BF16)
    k = _dot(c, wk_ref[...]) + _dot(kr_ref[...].astype(BF16), sel_ref[...])
    k_out[...] = k.astype(BF16)
    v_out[0] = (_dot_nt(wv_ref[...], c) + one_ref[...]).astype(BF16)


def _kv_expand(ckv, kr, wk_slot, sel, wvt_slot, one_col, batch):
    n = ckv.shape[0]
    t_k = n // batch
    tm = 1024 if t_k % 1024 == 0 else 512
    assert t_k % tm == 0
    per = t_k // tm
    row = lambda w: pl.BlockSpec((tm, w), lambda i: (i, 0))
    return pl.pallas_call(
        _kv_expand_kernel,
        grid=(n // tm,),
        in_specs=[row(256), row(MLA_ROPE), _const_spec(wk_slot.shape), _const_spec(sel.shape),
                  _const_spec(wvt_slot.shape), _const_spec(one_col.shape)],
        out_specs=[row(1024), pl.BlockSpec((1, MLA_V_ROWS, tm), lambda i: (i // per, 0, i % per))],
        out_shape=[jax.ShapeDtypeStruct((n, 1024), BF16),
                   jax.ShapeDtypeStruct((batch, MLA_V_ROWS, t_k), BF16)],
        compiler_params=_cparams(("parallel",)),
    )(ckv, kr, wk_slot, sel, wvt_slot, one_col)


def _mla_attn_kernel(qi_ref, ki_ref, fl_ref, q_ref, k_ref, vt_ref, o_ref, m_scr, acc_scr,
                     *, tq, tk):
    p_id = pl.program_id(1)
    flags = fl_ref[p_id]
    first = (flags & 1) != 0
    last = (flags & 2) != 0
    masked = (flags & 4) != 0

    @pl.when(first)
    def _():
        m_scr[...] = jnp.full(m_scr.shape, NEG, F32)
        acc_scr[...] = jnp.zeros(acc_scr.shape, F32)

    qb = min(MLA_QUERY_BLOCK, tq)

    def scores_t(h, c0):
        sl = slice(h * LANES, (h + 1) * LANES)
        return _dot_nt(k_ref[0, :, sl], q_ref[0, c0:c0 + qb, sl])

    def body(use_mask, first_query=0):
        insts = [(h, c0) for h in range(MLA_HEADS) for c0 in range(first_query, tq, qb)]
        if use_mask:
            q_chunk0 = (qi_ref[p_id] * tq) // CHUNK
            k_chunk0 = (ki_ref[p_id] * tk) // CHUNK
            kc = k_chunk0 + lax.broadcasted_iota(jnp.int32, (tk, qb), 0) // CHUNK
            qc_local = lax.broadcasted_iota(jnp.int32, (tk, qb), 1) // CHUNK
        ahead = [scores_t(*insts[i]) for i in range(min(MLA_SCORES_AHEAD, len(insts)))]
        for idx, (h, c0) in enumerate(insts):
            s = ahead.pop(0)
            if idx + MLA_SCORES_AHEAD < len(insts):
                ahead.append(scores_t(*insts[idx + MLA_SCORES_AHEAD]))
            if use_mask:
                s = jnp.where(kc <= qc_local + (q_chunk0 + c0 // CHUNK), s, NEG)
            m_prev = m_scr[h:h + 1, c0:c0 + qb]
            m_new = jnp.maximum(m_prev, jnp.max(s, axis=0, keepdims=True))
            m_scr[h:h + 1, c0:c0 + qb] = m_new
            alpha = jnp.exp2(m_prev - m_new)
            p_t = jnp.exp2(s - m_new).astype(BF16)
            vt_h = vt_ref[0, h * MLA_V_SLOT:(h + 1) * MLA_V_SLOT, :]
            acc_scr[h, :, c0:c0 + qb] = alpha * acc_scr[h, :, c0:c0 + qb] + _dot(vt_h, p_t)

    diag = flags >> 3
    for j in range(max(tq // tk, 1)):
        @pl.when(masked & (diag == j))
        def _(j=j):
            body(True, first_query=j * tk)

    @pl.when(jnp.logical_not(masked))
    def _():
        body(False)

    @pl.when(last)
    def _():
        outs = []
        for h in range(MLA_HEADS):
            a = acc_scr[h]
            outs.append(a[0:MLA_V] / a[MLA_V:MLA_V + 1])
        o_ref[0] = jnp.concatenate(outs, axis=0).T.astype(BF16)


MLA_Q_TILE = 1024
MLA_KV_TILE = 512
MLA_QUERY_BLOCK = 256
MLA_SCORES_AHEAD = 3


def _mla_attn(q, k, vt, pairs, tq, tk):
    b, t_q, _ = q.shape
    qi, ki, fl = pairs
    grid_spec = pltpu.PrefetchScalarGridSpec(
        num_scalar_prefetch=3,
        grid=(b, qi.shape[0]),
        in_specs=[pl.BlockSpec((1, tq, 1024), lambda bb, p, qi, ki, fl: (bb, qi[p], 0)),
                  pl.BlockSpec((1, tk, 1024), lambda bb, p, qi, ki, fl: (bb, ki[p], 0)),
                  pl.BlockSpec((1, MLA_V_ROWS, tk), lambda bb, p, qi, ki, fl: (bb, 0, ki[p]))],
        out_specs=pl.BlockSpec((1, tq, 512), lambda bb, p, qi, ki, fl: (bb, qi[p], 0)),
        scratch_shapes=[pltpu.VMEM((MLA_HEADS, tq), F32),
                        pltpu.VMEM((MLA_HEADS, MLA_V_SLOT, tq), F32)])
    return pl.pallas_call(
        functools.partial(_mla_attn_kernel, tq=tq, tk=tk),
        grid_spec=grid_spec,
        out_shape=jax.ShapeDtypeStruct((b, t_q, 512), BF16),
        compiler_params=_cparams(("parallel", "arbitrary")),
    )(qi, ki, fl, q, k, vt)


def _causal_pairs(nq, ratio):
    qi, ki, fl = [], [], []
    for a in range(nq):
        n_kv = (a + 1) * ratio
        for c in range(n_kv):
            qi.append(a)
            ki.append(c)
            diag = max(c - a * ratio, 0)
            fl.append((1 if c == 0 else 0) | (2 if c == n_kv - 1 else 0) | (4 if c >= a * ratio else 0)
                      | (diag << 3))
    return tuple(jnp.asarray(np.array(z, np.int32)) for z in (qi, ki, fl))


def _mla_decode_kernel(q_ref, cc_ref, kc_ref, cn_ref, kn_ref, wka_ref, prope_ref, wvp_ref, o_ref,
                       qa_scr, qr_scr, m_scr, l_scr, acc_scr, *, n_cache_blocks, t):
    k_id = pl.program_id(1)

    @pl.when(k_id == 0)
    def _():
        for h in range(MLA_HEADS):
            qs = q_ref[0, :, h * LANES:(h + 1) * LANES]
            qa_scr[h * t:(h + 1) * t, :] = _dot(qs, wka_ref[h]).astype(BF16)
            qr_scr[h * t:(h + 1) * t, :] = _dot(qs, prope_ref[...]).astype(BF16)
        m_scr[...] = jnp.full(m_scr.shape, NEG, F32)
        l_scr[...] = jnp.zeros(l_scr.shape, F32)
        acc_scr[...] = jnp.zeros(acc_scr.shape, F32)

    def update(ckv, kr):
        cb = ckv.astype(BF16)
        s = _dot_nt(qa_scr[...], cb) + _dot_nt(qr_scr[:, 0:MLA_ROPE], kr.astype(BF16))
        m_prev = m_scr[...]
        m_new = jnp.maximum(m_prev, jnp.max(s, axis=-1, keepdims=True))
        alpha = jnp.exp2(m_prev - m_new)
        p = jnp.exp2(s - m_new[:, 0:1])
        l_scr[...] = alpha * l_scr[...] + jnp.sum(p, axis=-1, keepdims=True)
        m_scr[...] = m_new
        acc_scr[...] = jnp.concatenate([alpha, alpha], axis=1) * acc_scr[...] + _dot(p.astype(BF16), cb)

    @pl.when(k_id < n_cache_blocks)
    def _():
        update(cc_ref[0], kc_ref[0])

    @pl.when(k_id == n_cache_blocks)
    def _():
        update(cn_ref[0], kn_ref[0])
        l = l_scr[...]
        o_lat = (acc_scr[...] / jnp.concatenate([l, l], axis=1)).astype(BF16)
        for pr in range(MLA_HEADS // 2):
            oe = o_lat[(2 * pr) * t:(2 * pr + 1) * t]
            oo = o_lat[(2 * pr + 1) * t:(2 * pr + 2) * t]
            o_ref[0, :, pr * LANES:(pr + 1) * LANES] = (
                _dot(oe, wvp_ref[2 * pr]) + _dot(oo, wvp_ref[2 * pr + 1])).astype(BF16)


MLA_DECODE_KV_TILE = 2048


def _mla_decode(q, ckv_cache, kr_cache, layer, ckv_new, kr_new, wka, prope, wvp):
    b, t, _ = q.shape
    n_past = ckv_cache.shape[2]
    kb = math.gcd(n_past, MLA_DECODE_KV_TILE)
    ncb = n_past // kb
    rows = MLA_HEADS * t
    per_b = lambda shape: pl.BlockSpec((1,) + shape, lambda bb, k: (bb, 0, 0))
    cache = lambda w: pl.BlockSpec((None, 1, kb, w), lambda bb, k: (layer, bb, jnp.minimum(k, ncb - 1), 0))
    return pl.pallas_call(
        functools.partial(_mla_decode_kernel, n_cache_blocks=ncb, t=t),
        grid=(b, ncb + 1),
        in_specs=[per_b((t, 1024)), cache(256), cache(MLA_ROPE), per_b((t, 256)), per_b((t, MLA_ROPE)),
                  _const_spec(wka.shape), _const_spec(prope.shape), _const_spec(wvp.shape)],
        out_specs=per_b((t, 512)),
        out_shape=jax.ShapeDtypeStruct((b, t, 512), BF16),
        scratch_shapes=[pltpu.VMEM((rows, 256), BF16), pltpu.VMEM((rows, LANES), BF16),
                        pltpu.VMEM((rows, LANES), F32), pltpu.VMEM((rows, LANES), F32),
                        pltpu.VMEM((rows, 256), F32)],
        compiler_params=_cparams(("parallel", "arbitrary")),
    )(q, ckv_cache, kr_cache, ckv_new, kr_new, wka, prope, wvp)


RW_CHUNK_GROUP = 2
RW_TIME_TILE = 512


def _pair_sum(x, ones_bd):
    hi, lo = _split2(x)
    return _dot(hi, ones_bd) + _dot(lo, ones_bd)


def _rwkv_kernel(pr_ref, sh0_ref, s0_ref, mu_ref, w0_ref, wl_ref, a0_ref, al_ref, g2_ref,
                 kk_ref, ka_ref, rk_ref, lnw_ref, lnb_ref, y_ref, st_ref,
                 prev_scr, s_scr, rt_scr, at_scr, bt_scr, kt_scr, bv_scr, k2_scr, v_scr,
                 cum_scr, yc_scr, *, tt):
    t_id = pl.program_id(1)
    n_pairs = RW_HEADS // 2

    @pl.when(t_id == 0)
    def _():
        s_scr[...] = s0_ref[0]
        prev_scr[0:1, :] = sh0_ref[0]

    n_chunks = tt // CHUNK
    groups = [list(range(c0, min(c0 + RW_CHUNK_GROUP, n_chunks))) for c0 in range(0, n_chunks, RW_CHUNK_GROUP)]
    gate, bonus = {}, {}

    carried_prev = prev_scr[0:1, :]
    prev_scr[0:1, :] = pr_ref[0, tt - 1:tt, :]

    li = lax.broadcasted_iota(jnp.int32, (LANES, LANES), 0)
    lj = lax.broadcasted_iota(jnp.int32, (LANES, LANES), 1)
    ones_bd = jnp.where((li // RW_N) == (lj // RW_N), 1.0, 0.0).astype(BF16)

    def head_sum(x):
        return jnp.concatenate([_pair_sum(x[:, p * LANES:(p + 1) * LANES], ones_bd)
                                for p in range(n_pairs)], axis=1)

    def prep_phase(gi):
        r0 = groups[gi][0] * CHUNK
        gr = len(groups[gi]) * CHUNK
        rows = slice(r0, r0 + gr)
        pr = pr_ref[0, rows, :]
        before = carried_prev if gi == 0 else pr_ref[0, r0 - 1:r0, :]
        row = lax.broadcasted_iota(jnp.int32, (gr, 1), 0)
        prev = jnp.where(row == 0, before, pltpu.roll(pr, 1, axis=0))
        pm = pr + (prev - pr) * mu_ref[...]
        r = pm[:, 0:512]
        k = pm[:, 512:1024]
        v = pm[:, 1024:1536]
        wa = pm[:, 1536:1664]
        g_in = pm[:, 1664:1792]
        yield
        z = w0_ref[...] + _dot(jnp.tanh(wa).astype(BF16), wl_ref[...])
        nz = -z
        w = -(jnp.maximum(nz, 0.0) + jnp.log(1.0 + jnp.exp(-jnp.abs(nz)))) - 0.5
        ld = -jnp.exp(w)
        yield
        a_sig = jax.nn.sigmoid(a0_ref[...] + _dot(wa.astype(BF16), al_ref[...]))
        gate[gi] = _dot(jax.nn.sigmoid(g_in).astype(BF16), g2_ref[...])
        yield
        kk = k * kk_ref[...]
        kk = kk * lax.rsqrt(jnp.maximum(head_sum(kk * kk), 1e-24))
        k2 = k * (1.0 + (a_sig - 1.0) * ka_ref[...])
        yield
        bonus[gi] = head_sum(r * k2 * rk_ref[...]) * v
        yield
        ti = lax.broadcasted_iota(jnp.int32, (gr, gr), 0)
        tj = lax.broadcasted_iota(jnp.int32, (gr, gr), 1)
        tri = jnp.where(((ti // CHUNK) == (tj // CHUNK)) & (tj <= ti), 1.0, 0.0).astype(BF16)
        l1 = ld.astype(BF16)
        rem = ld - l1.astype(F32)
        l2 = rem.astype(BF16)
        l3 = (rem - l2.astype(F32)).astype(BF16)
        cum = (_dot(tri, l3) + _dot(tri, l2)) + _dot(tri, l1)
        yield
        winv = jnp.exp(-cum)
        rt_scr[rows, :] = r * jnp.exp(cum)
        at_scr[rows, :] = -kk * jnp.exp(cum - ld)
        yield
        bv = kk * a_sig
        bt_scr[rows, :] = bv * winv
        kt_scr[rows, :] = k2 * winv
        bv_scr[rows, :] = bv
        k2_scr[rows, :] = k2
        v_scr[rows, :] = v
        cum_scr[rows, :] = cum

    lane = lax.broadcasted_iota(jnp.int32, (CHUNK, LANES), 1)
    even = lane < RW_N
    strict = (lj % RW_N) < (li % RW_N)
    incl = (lj % RW_N) <= (li % RW_N)
    eye = jnp.where(li == lj, 1.0, 0.0).astype(F32)
    pairs = range(n_pairs)

    def same_block(m):
        return (li // m) == (lj // m)

    def stack_f32(x):
        return jnp.concatenate([jnp.where(even, x, 0.0), jnp.where(even, 0.0, x)], axis=0)

    def stack(x):
        return stack_f32(x).astype(BF16)

    mm = lambda a, b: _dot(a.astype(BF16), b.astype(BF16))
    rows_of = lambda c: slice(c * CHUNK, (c + 1) * CHUNK)
    lanes_of = lambda p: slice(p * LANES, (p + 1) * LANES)

    def independent_phase(chunks, res):
        insts = [(c, p) for c in chunks for p in pairs]
        load = lambda scr: [scr[rows_of(c), lanes_of(p)] for c, p in insts]
        cum_c = load(cum_scr)
        cum_l = [a[CHUNK - 1:CHUNK, :] for a in cum_c]
        w2 = [jnp.exp(a - b) for a, b in zip(cum_l, cum_c)]
        rs = [stack(a) for a in load(rt_scr)]
        as_ = [stack(a) for a in load(at_scr)]
        bs = [stack(a) for a in load(bt_scr)]
        ks = [stack(a) for a in load(kt_scr)]
        v_f = [stack_f32(a) for a in load(v_scr)]
        vs = [a.astype(BF16) for a in v_f]
        b2s = [stack(a * w) for a, w in zip(load(bv_scr), w2)]
        k2s = [stack(a * w) for a, w in zip(load(k2_scr), w2)]
        yield
        bk = [jnp.concatenate([b_, k_], axis=0) for b_, k_ in zip(bs, ks)]
        a_bk = [_dot_nt(a, c_) for a, c_ in zip(as_, bk)]
        r_bk = [_dot_nt(a, c_) for a, c_ in zip(rs, bk)]
        n_m = [jnp.where(strict, x_[:, 0:LANES], 0.0) for x_ in a_bk]
        mk = [jnp.where(strict, x_[:, LANES:2 * LANES], 0.0).astype(BF16) for x_ in a_bk]
        cb = [jnp.where(incl, x_[:, 0:LANES], 0.0).astype(BF16) for x_ in r_bk]
        ck = [jnp.where(incl, x_[:, LANES:2 * LANES], 0.0).astype(BF16) for x_ in r_bk]
        yield
        n8 = [jnp.where(same_block(8), n, 0.0) for n in n_m]
        t = [eye + a for a in n8]
        p2 = [mm(a, a) for a in n8]
        mv = [_dot(a, b) for a, b in zip(mk, vs)]
        yield
        t = [a + mm(a, b) for a, b in zip(t, p2)]
        p4 = [mm(a, a) for a in p2]
        cv = [_dot(a, b) for a, b in zip(ck, vs)]
        yield
        t = [a + mm(a, b) for a, b in zip(t, p4)]
        vk = [_dot(a.T.astype(BF16), b) for a, b in zip(v_f, k2s)]
        yield
        for m in (8, 16, 32):
            off = same_block(2 * m) & jnp.logical_not(same_block(m))
            x = [mm(jnp.where(off, n, 0.0), a) for n, a in zip(n_m, t)]
            yield
            t = [a + mm(a, b) for a, b in zip(t, x)]
            yield
        for i, key in enumerate(insts):
            res[key] = dict(as_=as_[i], rs=rs[i], b2s=b2s[i], cb=cb[i], mv=mv[i], cv=cv[i], vk=vk[i],
                            tinv=t[i].astype(BF16), wl=jnp.exp(cum_l[i]))

    def dependent_phase(chunks, res):
        for c in chunks:
            rc = [res[(c, p)] for p in pairs]
            s = [s_scr[p] for p in pairs]
            sb = [a.astype(BF16) for a in s]
            x = [_dot_nt(r_['as_'], b) + r_['mv'] for r_, b in zip(rc, sb)]
            rs_s = [_dot_nt(r_['rs'], b) + r_['cv'] for r_, b in zip(rc, sb)]
            yield
            u = [_dot(r_['tinv'], a.astype(BF16)) for r_, a in zip(rc, x)]
            yield
            ys = [a + _dot(r_['cb'], b.astype(BF16)) for a, r_, b in zip(rs_s, rc, u)]
            for p in pairs:
                yc_scr[rows_of(c), lanes_of(p)] = ys[p][0:CHUNK] + ys[p][CHUNK:2 * CHUNK]
                s_scr[p] = s[p] * rc[p]['wl'] + _dot(u[p].T.astype(BF16), rc[p]['b2s']) + rc[p]['vk']
            yield

    def emit(*gens):
        gens = list(gens)
        while gens:
            for gen in list(gens):
                try:
                    next(gen)
                except StopIteration:
                    gens.remove(gen)

    def output_phase(gi):
        r0 = groups[gi][0] * CHUNK
        rows = slice(r0, r0 + len(groups[gi]) * CHUNK)
        y = yc_scr[rows, :]
        mean = head_sum(y) * (1.0 / RW_N)
        dlt = y - mean
        yield
        var = head_sum(dlt * dlt) * (1.0 / RW_N)
        yield
        yn = dlt * lax.rsqrt(var + RW_GN_EPS) * lnw_ref[...] + lnb_ref[...]
        y_ref[0, rows, :] = ((yn + bonus[gi]) * gate[gi]).astype(BF16)

    res = {}
    n_g = len(groups)
    for step in range(n_g + 3):
        live = []
        if step < n_g:
            live.append(prep_phase(step))
        if 0 <= step - 1 < n_g:
            live.append(independent_phase(groups[step - 1], res))
        if 0 <= step - 2 < n_g:
            live.append(dependent_phase(groups[step - 2], res))
        if 0 <= step - 3 < n_g:
            live.append(output_phase(step - 3))
        emit(*live)

    @pl.when(t_id == pl.num_programs(1) - 1)
    def _():
        st_ref[0] = s_scr[...]


def _rwkv(prw, sh0, s0_bd, wts):
    b, t, _ = prw.shape
    tt = min(RW_TIME_TILE, t)
    assert t % tt == 0 and tt % CHUNK == 0
    c512 = _const_spec((1, RW_C))
    scr = lambda: pltpu.VMEM((tt, RW_C), F32)
    return pl.pallas_call(
        functools.partial(_rwkv_kernel, tt=tt),
        grid=(b, t // tt),
        in_specs=[pl.BlockSpec((1, tt, RW_COLS), lambda bb, i: (bb, i, 0)),
                  pl.BlockSpec((1, 1, RW_COLS), lambda bb, i: (bb, 0, 0)),
                  pl.BlockSpec((1, 4, LANES, LANES), lambda bb, i: (bb, 0, 0, 0)),
                  _const_spec((1, RW_COLS)), c512, _const_spec((LANES, RW_C)), c512,
                  _const_spec((LANES, RW_C)), _const_spec((LANES, RW_C)),
                  c512, c512, c512, c512, c512],
        out_specs=[pl.BlockSpec((1, tt, RW_C), lambda bb, i: (bb, i, 0)),
                   pl.BlockSpec((1, 4, LANES, LANES), lambda bb, i: (bb, 0, 0, 0))],
        out_shape=[jax.ShapeDtypeStruct((b, t, RW_C), BF16),
                   jax.ShapeDtypeStruct((b, 4, LANES, LANES), F32)],
        scratch_shapes=[pltpu.VMEM((8, RW_COLS), F32), pltpu.VMEM((4, LANES, LANES), F32)]
                       + [scr() for _ in range(9)],
        compiler_params=_cparams(("parallel", "arbitrary")),
    )(prw, sh0, s0_bd, *wts)


def _state_to_bd(s):
    b = s.shape[0]
    s = s.reshape(b, 4, 2, RW_N, RW_N)
    z = jnp.zeros_like(s[:, :, 0])
    top = jnp.concatenate([s[:, :, 0], z], axis=-1)
    bot = jnp.concatenate([z, s[:, :, 1]], axis=-1)
    return jnp.concatenate([top, bot], axis=-2)


def _state_from_bd(s):
    b = s.shape[0]
    return jnp.stack([s[:, :, :RW_N, :RW_N], s[:, :, RW_N:, RW_N:]], axis=2).reshape(b, RW_HEADS, RW_N, RW_N)


_ODD_SEGS = ([(1024 + g * 64, 1088 + g * 64) for g in range(SW_KV_HEADS) for _ in range(2)]
             + [(1280 + g * 64, 1344 + g * 64) for g in range(SW_KV_HEADS) for _ in range(2)])


def _odd_in_kernel(x_ref, mods_ref, ng_ref, w_ref, b_ref, q_out, k_out, v_out, *, nb):
    h = _norm_mod(x_ref[...], ng_ref[...], mods_ref[:, 3, :], mods_ref[:, 4, :], nb).astype(BF16)
    q = _dot(h, w_ref[:, 0:1024]) + b_ref[:, 0:1024]
    q_out[...] = (q * (SW_HD ** -0.5 * LOG2E)).astype(BF16)
    k_out[...] = _dot(h, w_ref[:, 1024:1536]) + b_ref[:, 1024:1536]
    v_out[...] = _dot(h, w_ref[:, 1536:2048]) + b_ref[:, 1536:2048]


def _odd_in(x, mods, norm_g, w_perm, b_perm, rows_per_seq):
    n, d = x.shape
    tm, nb = _row_tiling(n, rows_per_seq, 512)
    row = lambda w: pl.BlockSpec((tm, w), lambda i: (i, 0))
    return pl.pallas_call(
        functools.partial(_odd_in_kernel, nb=nb),
        grid=(n // tm,),
        in_specs=[row(d), _mods_spec(tm, nb, rows_per_seq), _const_spec((1, d)),
                  _const_spec(w_perm.shape), _const_spec((1, 2048))],
        out_specs=[row(1024), row(512), row(512)],
        out_shape=[jax.ShapeDtypeStruct((n, 1024), BF16), jax.ShapeDtypeStruct((n, 512), F32),
                   jax.ShapeDtypeStruct((n, 512), F32)],
        compiler_params=_cparams(("parallel",)),
    )(x, mods, norm_g.reshape(1, d), w_perm, b_perm.reshape(1, 2048))


def _swa_kernel(sinks_ref, q_ref, kp_ref, kc_ref, vp_ref, vc_ref, o_ref, *, tq, mask_first_prev):
    nk = WINDOW + tq
    qi = lax.broadcasted_iota(jnp.int32, (tq, nk), 0)
    kj = lax.broadcasted_iota(jnp.int32, (tq, nk), 1)
    kc = kj // CHUNK - WINDOW // CHUNK
    qc = qi // CHUNK
    vis = (kc <= qc) & (kc >= qc - WINDOW // CHUNK)
    if mask_first_prev:
        vis = vis & ((kj >= WINDOW) | (pl.program_id(1) > 0))
    ndist = jnp.where(vis, -jnp.abs(qi + WINDOW - kj).astype(F32), NEG)
    keys = jnp.concatenate([kp_ref[0], kc_ref[0]], axis=0).astype(BF16)
    vals = jnp.concatenate([vp_ref[0], vc_ref[0]], axis=0)
    klane = lax.broadcasted_iota(jnp.int32, (nk, LANES), 1)
    v_slots = [jnp.where(klane < SW_HD, vals[:, g * LANES:(g + 1) * LANES], 1.0).astype(BF16)
               for g in range(SW_KV_HEADS)]
    lane = lax.broadcasted_iota(jnp.int32, (tq, LANES), 1)
    low = lane < SW_HD
    sb = min(SWA_SUB_BLOCK, tq)

    def scores(h):
        qp = q_ref[0, :, (h // 2) * LANES:(h // 2 + 1) * LANES]
        qh = jnp.where(low if h % 2 == 0 else jnp.logical_not(low), qp, jnp.zeros_like(qp))
        return _dot_nt(qh, keys[:, (h // SW_GROUP) * LANES:(h // SW_GROUP + 1) * LANES])

    outs = []
    ahead = [scores(h) for h in range(SWA_SCORES_AHEAD)]
    for h in range(SW_HEADS):
        s = ahead.pop(0)
        if h + SWA_SCORES_AHEAD < SW_HEADS:
            ahead.append(scores(h + SWA_SCORES_AHEAD))
        slope = (2.0 ** (-8.0 * (h + 1) / SW_HEADS)) * LOG2E
        sk = sinks_ref[h] * LOG2E
        es, ms = [], []
        for r1 in range(0, tq, sb):
            z = s[r1:r1 + sb] + slope * ndist[r1:r1 + sb]
            m = jnp.maximum(jnp.broadcast_to(jnp.max(z, axis=-1, keepdims=True), (sb, LANES)), sk)
            m_wide = jnp.concatenate([m] * (nk // LANES), axis=1) if nk % LANES == 0 else m[:, 0:1]
            es.append(jnp.exp2(z - m_wide).astype(BF16))
            ms.append(m)
        e = jnp.concatenate(es, axis=0) if len(es) > 1 else es[0]
        m = jnp.concatenate(ms, axis=0) if len(ms) > 1 else ms[0]
        pv = _dot(e, v_slots[h // SW_GROUP])
        outs.append(pv / (pltpu.roll(pv, SW_HD, axis=1) + jnp.exp2(sk - m)))
    for pr in range(SW_HEADS // 2):
        o_ref[0, :, pr * LANES:(pr + 1) * LANES] = jnp.where(
            low, outs[2 * pr], pltpu.roll(outs[2 * pr + 1], SW_HD, axis=1)).astype(BF16)


SWA_Q_TILE = 256
SWA_SUB_BLOCK = 32
SWA_SCORES_AHEAD = 2


def _swa(q, k_prev, k_cur, v_prev, v_cur, sinks, tq, same_array):
    b, t, _ = q.shape
    nt = t // tq
    per = tq // WINDOW
    if same_array:
        prev_map = lambda bb, i: (bb, jnp.maximum(i * per - 1, 0), 0)
    else:
        prev_map = lambda bb, i: (bb, 0, 0)
    cur = lambda w: pl.BlockSpec((1, tq, w), lambda bb, i: (bb, i, 0))
    prev = pl.BlockSpec((1, WINDOW, 512), prev_map)
    return pl.pallas_call(
        functools.partial(_swa_kernel, tq=tq, mask_first_prev=same_array),
        grid=(b, nt),
        in_specs=[pl.BlockSpec(memory_space=pltpu.SMEM), cur(1024), prev, cur(512), prev, cur(512)],
        out_specs=cur(1024),
        out_shape=jax.ShapeDtypeStruct((b, t, 1024), BF16),
        compiler_params=_cparams(("parallel", "parallel")),
    )(sinks, q, k_prev, k_cur, v_prev, v_cur)


def _undup(a):
    return a.reshape(a.shape[:-1] + (SW_KV_HEADS, 2, SW_HD))[..., 0, :]


def _dup(a):
    return jnp.concatenate([a, a], axis=-1).reshape(a.shape[:-2] + (512,))


def _prep_weights(p):
    depth = p['w_ada'].shape[0]
    n_even, n_odd = (depth + 1) // 2, depth // 2
    w = {}
    w['ffn_in'] = p['ffn_w_in'].astype(BF16)
    w['ffn_out'] = p['ffn_w_out'].astype(BF16)
    wi = p['even_w_in'].astype(BF16)
    w['even_in'] = jnp.concatenate(
        [wi[:, :, 0:1024]] + [wi[:, :, _RW_OFF + a:_RW_OFF + b] for a, b in _RW_SEGS]
        + [wi[:, :, 1024:1056], jnp.zeros((n_even, 1024, 96), BF16)], axis=2)
    w['wuq'] = jnp.pad(p['mla_w_uq'], ((0, 0), (0, 0), (0, 0), (0, 32))).reshape(n_even, 768, 1024).astype(BF16)
    w['wk'] = jnp.pad(p['mla_w_ukv'][..., :MLA_NOPE], ((0, 0), (0, 0), (0, 0), (0, 64))
                      ).reshape(n_even, 256, 1024).astype(BF16)
    wv_t = jnp.transpose(p['mla_w_ukv'][..., MLA_NOPE:], (0, 2, 3, 1))
    w['wv'] = jnp.pad(wv_t, ((0, 0), (0, 0), (0, MLA_V_SLOT - MLA_V), (0, 0))
                      ).reshape(n_even, MLA_V_ROWS, 256).astype(BF16)
    sel = np.zeros((MLA_ROPE, 1024), np.float32)
    one = np.zeros((MLA_V_ROWS, 1), np.float32)
    for h in range(MLA_HEADS):
        sel[np.arange(MLA_ROPE), h * LANES + MLA_NOPE + np.arange(MLA_ROPE)] = 1.0
        one[h * MLA_V_SLOT + MLA_V, 0] = 1.0
    w['sel'] = jnp.asarray(sel).astype(BF16)
    w['one'] = jnp.asarray(one)
    wk_t = jnp.transpose(p['mla_w_ukv'][..., :MLA_NOPE], (0, 2, 3, 1))
    w['wka'] = jnp.pad(wk_t, ((0, 0), (0, 0), (0, 64), (0, 0))).astype(BF16)
    prope = np.zeros((LANES, LANES), np.float32)
    prope[MLA_NOPE + np.arange(MLA_ROPE), np.arange(MLA_ROPE)] = 1.0
    w['prope'] = jnp.asarray(prope).astype(BF16)
    wv_h = jnp.transpose(p['mla_w_ukv'][..., MLA_NOPE:], (0, 2, 1, 3))
    wv_even = jnp.pad(wv_h[:, 0::2], ((0, 0), (0, 0), (0, 0), (0, 64)))
    wv_odd = jnp.pad(wv_h[:, 1::2], ((0, 0), (0, 0), (0, 0), (64, 0)))
    w['wvp'] = jnp.stack([wv_even, wv_odd], axis=2).reshape(n_even, MLA_HEADS, 256, LANES).astype(BF16)
    z64 = jnp.zeros((n_even, 64, RW_C), F32)
    w['wl'] = jnp.concatenate([p['rw_w2'], z64], axis=1).astype(BF16)
    w['al'] = jnp.concatenate([z64, p['rw_a2']], axis=1).astype(BF16)
    w['g2'] = p['rw_g2'].astype(BF16)
    w['mu'] = _rw_permute(p['rw_mu'])
    w['even_out'] = p['even_w_out'].astype(BF16)
    wo = p['odd_w_qkv'].astype(BF16)
    w['odd_in'] = jnp.concatenate([wo[:, :, 0:1024]] + [wo[:, :, a:b] for a, b in _ODD_SEGS], axis=2)
    bo = p['odd_b_qkv']
    w['odd_b'] = jnp.concatenate([bo[:, 0:1024]] + [bo[:, a:b] for a, b in _ODD_SEGS], axis=1)
    w['odd_out'] = p['odd_w_out'].astype(BF16)
    return w


def _trunk(x3, mods_all, start, past, p, w):
    b, t, d = x3.shape
    n = b * t
    depth = mods_all.shape[0]
    x = x3.reshape(n, d)
    rows = t
    tm_even, _ = _row_tiling(n, rows, 512)
    pos = start + jnp.arange(t)
    tabs = _rope_tables(pos, tm_even)
    even_states, odd_states = [], []
    for i in range(depth):
        mods = mods_all[i]
        j = i // 2
        x = _ffn(x, mods, p['norm_g'][i, 0], w['ffn_in'], w['ffn_out'], i, 0, 0, rows)
        if i % 2 == 0:
            q, ckv, kr, prw = _even_in(x, mods, p['norm_g'][i, 1], w['even_in'][j], p['mla_q_norm'][j],
                                       p['mla_kv_norm'][j], w['wuq'][j], tabs, rows)
            if past is None:
                kx, vtx = _kv_expand(ckv, kr, w['wk'][j], w['sel'], w['wv'][j], w['one'], b)
                tk = min(MLA_KV_TILE, t)
                tq = min(MLA_Q_TILE, t)
                att = _mla_attn(q.reshape(b, t, 1024), kx.reshape(b, t, 1024), vtx,
                                _causal_pairs(t // tq, tq // tk), tq, tk)
                s0 = jnp.zeros((b, RW_HEADS, RW_N, RW_N), F32)
                sh0 = jnp.zeros((b, RW_COLS), F32)
            else:
                s0, sh0 = past[2][j], past[3][j]
                att = _mla_decode(q.reshape(b, t, 1024), past[0], past[1], j, ckv.reshape(b, t, 256),
                                  kr.reshape(b, t, MLA_ROPE), w['wka'][j], w['prope'], w['wvp'][j])
            rw_wts = (w['mu'][j].reshape(1, RW_COLS), p['rw_w0'][j].reshape(1, RW_C), w['wl'][j],
                      p['rw_a0'][j].reshape(1, RW_C), w['al'][j], w['g2'][j],
                      p['rw_k_k'][j].reshape(1, RW_C), p['rw_k_a'][j].reshape(1, RW_C),
                      p['rw_r_k'][j].reshape(1, RW_C), p['rw_ln_w'][j].reshape(1, RW_C),
                      p['rw_ln_b'][j].reshape(1, RW_C))
            prw3 = prw.reshape(b, t, RW_COLS)
            y_rw, s_bd = _rwkv(prw3, _rw_permute(sh0).reshape(b, 1, RW_COLS), _state_to_bd(s0), rw_wts)
            mix = ((att.reshape(n, 512), w['even_out'][j][:512]), (y_rw.reshape(n, RW_C), w['even_out'][j][512:]))
            even_states.append((ckv.reshape(b, t, 256), kr.reshape(b, t, MLA_ROPE), _state_from_bd(s_bd),
                                _rw_unpermute(prw3[:, t - 1, :])))
        else:
            q, kd, vd = _odd_in(x, mods, p['norm_g'][i, 1], w['odd_in'][j], w['odd_b'][j], rows)
            q3, kd3, vd3 = q.reshape(b, t, 1024), kd.reshape(b, t, 512), vd.reshape(b, t, 512)
            if past is None:
                tq = min(SWA_Q_TILE, t)
                o = _swa(q3, kd3, kd3, vd3, vd3, p['swa_sinks'][j], tq, True)
                keep = min(WINDOW, t)
                k_new, v_new = _undup(kd3[:, t - keep:]), _undup(vd3[:, t - keep:])
            else:
                k_past, v_past = past[4][j], past[5][j]
                o = _swa(q3, _dup(k_past), kd3, _dup(v_past), vd3, p['swa_sinks'][j], t, False)
                k_new = jnp.concatenate([k_past, _undup(kd3)], axis=1)[:, t:]
                v_new = jnp.concatenate([v_past, _undup(vd3)], axis=1)[:, t:]
            mix = ((o.reshape(n, 1024), w['odd_out'][j]),)
            odd_states.append((k_new, v_new))
        fg = p['final_norm_g'] if i == depth - 1 else None
        x = _ffn(x, mods, p['norm_g'][i, 2], w['ffn_in'], w['ffn_out'], i, 1, 2, rows, final_g=fg, mix=mix)
    es = [jnp.stack([st[k] for st in even_states]) for k in range(4)]
    os_ = [jnp.stack([st[k] for st in odd_states]) for k in range(2)]
    return x.reshape(b, t, d), es + os_


def kernel(x_prompt, x_sample, cache_mla_ckv, cache_mla_krope, state_rwkv, state_rwkv_shift, cache_swa_k, cache_swa_v, c_prompt, c_sample, w_ada, b_ada, norm_g, ffn_w_in, ffn_w_out, even_w_in, even_w_out, mla_q_norm, mla_kv_norm, mla_w_uq, mla_w_ukv, rw_mu, rw_w0, rw_w2, rw_a0, rw_a2, rw_g2, rw_k_k, rw_k_a, rw_r_k, rw_ln_w, rw_ln_b, odd_w_qkv, odd_b_qkv, odd_w_out, swa_sinks, final_norm_g):
    p = dict(w_ada=w_ada, b_ada=b_ada, norm_g=norm_g, ffn_w_in=ffn_w_in, ffn_w_out=ffn_w_out,
             even_w_in=even_w_in, even_w_out=even_w_out, mla_q_norm=mla_q_norm, mla_kv_norm=mla_kv_norm,
             mla_w_uq=mla_w_uq, mla_w_ukv=mla_w_ukv, rw_mu=rw_mu, rw_w0=rw_w0, rw_w2=rw_w2, rw_a0=rw_a0,
             rw_a2=rw_a2, rw_g2=rw_g2, rw_k_k=rw_k_k, rw_k_a=rw_k_a, rw_r_k=rw_r_k, rw_ln_w=rw_ln_w,
             rw_ln_b=rw_ln_b, odd_w_qkv=odd_w_qkv, odd_b_qkv=odd_b_qkv, odd_w_out=odd_w_out,
             swa_sinks=swa_sinks, final_norm_g=final_norm_g)
    w = _prep_weights(p)
    depth = w_ada.shape[0]
    bp, bs = c_prompt.shape[0], c_sample.shape[0]
    d = c_prompt.shape[1]
    b_pad = -(-(bp + bs) // 8) * 8
    c_all = jnp.concatenate([c_prompt, c_sample, jnp.zeros((b_pad - bp - bs, d), F32)], axis=0)
    mods = _ada(c_all, w_ada, b_ada).reshape(depth, b_pad, 3 * N_SUB, d)
    y_prompt, sp = _trunk(x_prompt, mods[:, :bp], 0, None, p, w)
    past = (cache_mla_ckv, cache_mla_krope, state_rwkv, state_rwkv_shift, cache_swa_k, cache_swa_v)
    y_sample, ss = _trunk(x_sample, mods[:, bp:bp + bs], cache_mla_ckv.shape[2], past, p, w)
    return (y_prompt, y_sample, sp[0], sp[1], sp[2], sp[3], sp[4], sp[5],
            ss[0], ss[1], ss[2], ss[3], ss[4], ss[5])
```

```python
import functools
import math

import jax
import jax.numpy as jnp
import numpy as np
from jax import lax
from jax.experimental import pallas as pl
from jax.experimental.pallas import tpu as pltpu

F32 = jnp.float32
BF16 = jnp.bfloat16

CHUNK = 64
EPS = 1e-6
NEG = -1e30
N_SUB = 3
MLA_HEADS = 8
MLA_NOPE = 64
MLA_ROPE = 32
MLA_V = 64
MLA_Q_LORA = 768
MLA_KV_LORA = 256
MLA_SCALE = (MLA_NOPE + MLA_ROPE) ** -0.5
MLA_V_SLOT = 128
MLA_V_ROWS = MLA_HEADS * MLA_V_SLOT
LOG2E = math.log2(math.e)
ROPE_BASE = 10000.0
RW_HEADS = 8
RW_N = 64
RW_C = RW_HEADS * RW_N
RW_GN_EPS = 64e-5
RW_COLS = 3 * RW_C + 64 + 64 + 128
SW_HEADS = 16
SW_KV_HEADS = 4
SW_GROUP = 4
SW_HD = 64
WINDOW = 128

LANES = 128
VMEM_LIMIT = 56 * 1024 * 1024


def _cparams(sem):
    return pltpu.CompilerParams(dimension_semantics=sem, vmem_limit_bytes=VMEM_LIMIT)


def _const_spec(shape):
    nd = len(shape)
    return pl.BlockSpec(shape, lambda *_: (0,) * nd, pipeline_mode=pl.Buffered(1))


def _dot(a, b):
    return jnp.dot(a, b, preferred_element_type=F32)


def _dot_nt(a, b):
    return lax.dot_general(a, b, (((1,), (1,)), ((), ())), preferred_element_type=F32)


def _split2(x):
    hi = x.astype(BF16)
    lo = (x - hi.astype(F32)).astype(BF16)
    return hi, lo


def _norm_mod(x, g, sh, sc, nb):
    y = x * lax.rsqrt(jnp.mean(x * x, axis=-1, keepdims=True) + EPS) * g
    if nb == 1:
        return y * (1.0 + sc) + sh
    tm, d = x.shape
    y3 = y.reshape(nb, tm // nb, d)
    return (y3 * (1.0 + sc[:, None, :]) + sh[:, None, :]).reshape(tm, d)


def _gate_rows(g, y, nb):
    if nb == 1:
        return g * y
    tm, d = y.shape
    return (y.reshape(nb, tm // nb, d) * g[:, None, :]).reshape(tm, d)


def _row_tiling(n_rows, rows_per_seq, pref):
    if rows_per_seq >= pref:
        assert rows_per_seq % pref == 0
        return pref, 1
    tm = min(pref, n_rows)
    assert tm % rows_per_seq == 0 and n_rows % tm == 0
    return tm, tm // rows_per_seq


def _mods_spec(tm, nb, rows_per_seq):
    if nb == 1:
        tiles_per_seq = rows_per_seq // tm
        return pl.BlockSpec((1, 3 * N_SUB, 1024), lambda i: (i // tiles_per_seq, 0, 0))
    return pl.BlockSpec((nb, 3 * N_SUB, 1024), lambda i: (i, 0, 0))


def _ada_kernel(c_ref, w_ref, b_ref, o_ref):
    c = c_ref[...]
    cs = (c * jax.nn.sigmoid(c)).astype(BF16)
    o_ref[0] = _dot(cs, w_ref[0].astype(BF16)) + b_ref[0]


def _ada(c_all, w_ada, b_ada):
    depth, d, n = w_ada.shape
    bp = c_all.shape[0]
    tn = n // 4
    return pl.pallas_call(
        _ada_kernel,
        grid=(depth, n // tn),
        in_specs=[pl.BlockSpec((bp, d), lambda l, j: (0, 0)),
                  pl.BlockSpec((1, d, tn), lambda l, j: (l, 0, j)),
                  pl.BlockSpec((1, 1, tn), lambda l, j: (l, 0, j))],
        out_specs=pl.BlockSpec((1, bp, tn), lambda l, j: (l, 0, j)),
        out_shape=jax.ShapeDtypeStruct((depth, bp, n), F32),
        compiler_params=_cparams(("parallel", "parallel")),
    )(c_all, w_ada, b_ada.reshape(depth, 1, n))


FF_CHUNK = 256


def _ffn_kernel(x_ref, mods_ref, ng_ref, win_ref, wout_ref, *rest, sub, nb, d_ff, final, n_mix):
    mix_a, mix_w, rest = rest[:n_mix], rest[n_mix:2 * n_mix], rest[2 * n_mix:]
    if final:
        fg_ref, o_ref, a_scr = rest
    else:
        o_ref, a_scr = rest
    x = x_ref[...]
    if n_mix:
        ym = _dot(mix_a[0][...], mix_w[0][...])
        for a, w in zip(mix_a[1:], mix_w[1:]):
            ym = ym + _dot(a[...], w[...])
        x = x + _gate_rows(mods_ref[:, 5, :], ym, nb)
    sh = mods_ref[:, 3 * sub, :]
    sc = mods_ref[:, 3 * sub + 1, :]
    gt = mods_ref[:, 3 * sub + 2, :]
    h = _norm_mod(x, ng_ref[...], sh, sc, nb).astype(BF16)
    for c in range(d_ff // FF_CHUNK):
        lo = c * FF_CHUNK
        g = _dot(h, win_ref[:, lo:lo + FF_CHUNK])
        u = _dot(h, win_ref[:, d_ff + lo:d_ff + lo + FF_CHUNK])
        a_scr[:, lo:lo + FF_CHUNK] = (g * jax.nn.sigmoid(g) * u).astype(BF16)
    y = _dot(a_scr[...], wout_ref[...])
    out = x + _gate_rows(0.5 * gt, y, nb)
    if final:
        out = out * lax.rsqrt(jnp.mean(out * out, axis=-1, keepdims=True) + EPS) * fg_ref[...]
    o_ref[...] = out


def _ffn(x, mods, norm_g, w_in, w_out, layer, half, sub, rows_per_seq, final_g=None, mix=()):
    n, d = x.shape
    d_ff = w_out.shape[2]
    tm, nb = _row_tiling(n, rows_per_seq, 1024)
    final = final_g is not None
    stacked = lambda a: pl.BlockSpec((None, None) + a.shape[2:], lambda i: (layer, half, 0, 0),
                                     pipeline_mode=pl.Buffered(1))
    in_specs = [pl.BlockSpec((tm, d), lambda i: (i, 0)),
                _mods_spec(tm, nb, rows_per_seq),
                _const_spec((1, d)), stacked(w_in), stacked(w_out)]
    args = [x, mods, norm_g.reshape(1, d), w_in, w_out]
    in_specs += [pl.BlockSpec((tm, a.shape[1]), lambda i: (i, 0)) for a, _ in mix]
    in_specs += [_const_spec(wm.shape) for _, wm in mix]
    args += [a for a, _ in mix] + [wm for _, wm in mix]
    if final:
        in_specs.append(_const_spec((1, d)))
        args.append(final_g.reshape(1, d))
    return pl.pallas_call(
        functools.partial(_ffn_kernel, sub=sub, nb=nb, d_ff=d_ff, final=final, n_mix=len(mix)),
        grid=(n // tm,),
        in_specs=in_specs,
        out_specs=pl.BlockSpec((tm, d), lambda i: (i, 0)),
        out_shape=jax.ShapeDtypeStruct((n, d), F32),
        scratch_shapes=[pltpu.VMEM((tm, d_ff), BF16)],
        compiler_params=_cparams(("parallel",)),
    )(*args)


_RW_OFF = 1056
_RW_SEGS = [(0, 512), (576, 1088), (1088, 1600), (512, 576), (1600, 1664), (1664, 1792)]
_RW_INV_SEGS = [(0, 512), (1536, 1600), (512, 1024), (1024, 1536), (1600, 1664), (1664, 1792)]


def _rw_permute(a):
    return jnp.concatenate([a[..., s:e] for s, e in _RW_SEGS], axis=-1)


def _rw_unpermute(a):
    return jnp.concatenate([a[..., s:e] for s, e in _RW_INV_SEGS], axis=-1)


def _rope_slot(v, c, s1, s2):
    w = v.shape[-1]
    return v * c + pltpu.roll(v, w - 16, axis=1) * s1 + pltpu.roll(v, 16, axis=1) * s2


def _even_in_kernel(x_ref, mods_ref, ng_ref, w_ref, qn_ref, kvn_ref, wuq_ref,
                    cq_ref, s1q_ref, s2q_ref, ck_ref, s1k_ref, s2k_ref,
                    q_out, ckv_out, kr_out, prw_out, *, nb):
    x = x_ref[...]
    h = _norm_mod(x, ng_ref[...], mods_ref[:, 3, :], mods_ref[:, 4, :], nb).astype(BF16)
    cq = _dot(h, w_ref[:, 0:768])
    cqn = (cq * lax.rsqrt(jnp.mean(cq * cq, axis=-1, keepdims=True) + EPS) * qn_ref[...]).astype(BF16)
    q = _dot(cqn, wuq_ref[...])
    rep = lambda t: jnp.concatenate([t] * MLA_HEADS, axis=1)
    q = _rope_slot(q, rep(cq_ref[...]), rep(s1q_ref[...]), rep(s2q_ref[...]))
    q_out[...] = (q * (MLA_SCALE * LOG2E)).astype(BF16)
    ckv = _dot(h, w_ref[:, 768:1024])
    ckv_out[...] = ckv * lax.rsqrt(jnp.mean(ckv * ckv, axis=-1, keepdims=True) + EPS) * kvn_ref[...]
    prw_out[...] = _dot(h, w_ref[:, 1024:2816])
    krs = _dot(h, w_ref[:, 2816:2944])
    krs = _rope_slot(krs, ck_ref[...], s1k_ref[...], s2k_ref[...])
    kr_out[...] = krs[:, 0:MLA_ROPE]


def _even_in(x, mods, norm_g, w_perm, q_norm, kv_norm, wuq_slot, tabs, rows_per_seq):
    n, d = x.shape
    tm, nb = _row_tiling(n, rows_per_seq, 512)
    ttab = tabs[0].shape[0]
    ntab = ttab // tm
    tab_spec = pl.BlockSpec((tm, LANES), lambda i: (i % ntab, 0))
    row = lambda w: pl.BlockSpec((tm, w), lambda i: (i, 0))
    return pl.pallas_call(
        functools.partial(_even_in_kernel, nb=nb),
        grid=(n // tm,),
        in_specs=[row(d), _mods_spec(tm, nb, rows_per_seq), _const_spec((1, d)),
                  _const_spec(w_perm.shape), _const_spec((1, 768)), _const_spec((1, 256)),
                  _const_spec(wuq_slot.shape)] + [tab_spec] * 6,
        out_specs=[row(1024), row(256), row(MLA_ROPE), row(RW_COLS)],
        out_shape=[jax.ShapeDtypeStruct((n, 1024), BF16), jax.ShapeDtypeStruct((n, 256), F32),
                   jax.ShapeDtypeStruct((n, MLA_ROPE), F32), jax.ShapeDtypeStruct((n, RW_COLS), F32)],
        compiler_params=_cparams(("parallel",)),
    )(x, mods, norm_g.reshape(1, d), w_perm, q_norm.reshape(1, 768), kv_norm.reshape(1, 256),
      wuq_slot, *tabs)


def _rope_tables(pos, tile_to):
    half = MLA_ROPE // 2
    freqs = ROPE_BASE ** (-jnp.arange(half, dtype=F32) / half)
    ang = pos.astype(F32)[:, None] * freqs[None, :]
    cos, sin = jnp.cos(ang), jnp.sin(ang)
    t = pos.shape[0]
    z = lambda w: jnp.zeros((t, w), F32)
    o = lambda w: jnp.ones((t, w), F32)
    cq = jnp.concatenate([o(64), cos, cos, z(32)], axis=1)
    s1q = jnp.concatenate([z(64), -sin, z(48)], axis=1)
    s2q = jnp.concatenate([z(80), sin, z(32)], axis=1)
    ck = jnp.concatenate([cos, cos, z(96)], axis=1)
    s1k = jnp.concatenate([-sin, z(112)], axis=1)
    s2k = jnp.concatenate([z(16), sin, z(96)], axis=1)
    tabs = [cq, s1q, s2q, ck, s1k, s2k]
    if tile_to > t:
        tabs = [jnp.tile(a, (tile_to // t, 1)) for a in tabs]
    return tabs


def _kv_expand_kernel(ckv_ref, kr_ref, wk_ref, sel_ref, wv_ref, one_ref, k_out, v_out):
    c = ckv_ref[...].astype(BF16)
    k = _dot(c, wk_ref[...]) + _dot(kr_ref[...].astype(BF16), sel_ref[...])
    k_out[...] = k.astype(BF16)
    v_out[0] = (_dot_nt(wv_ref[...], c) + one_ref[...]).astype(BF16)


def _kv_expand(ckv, kr, wk_slot, sel, wvt_slot, one_col, batch):
    n = ckv.shape[0]
    t_k = n // batch
    tm = 1024 if t_k % 1024 == 0 else 512
    assert t_k % tm == 0
    per = t_k // tm
    row = lambda w: pl.BlockSpec((tm, w), lambda i: (i, 0))
    return pl.pallas_call(
        _kv_expand_kernel,
        grid=(n // tm,),
        in_specs=[row(256), row(MLA_ROPE), _const_spec(wk_slot.shape), _const_spec(sel.shape),
                  _const_spec(wvt_slot.shape), _const_spec(one_col.shape)],
        out_specs=[row(1024), pl.BlockSpec((1, MLA_V_ROWS, tm), lambda i: (i // per, 0, i % per))],
        out_shape=[jax.ShapeDtypeStruct((n, 1024), BF16),
                   jax.ShapeDtypeStruct((batch, MLA_V_ROWS, t_k), BF16)],
        compiler_params=_cparams(("parallel",)),
    )(ckv, kr, wk_slot, sel, wvt_slot, one_col)


def _mla_attn_kernel(qi_ref, ki_ref, fl_ref, q_ref, k_ref, vt_ref, o_ref, m_scr, acc_scr,
                     *, tq, tk):
    p_id = pl.program_id(1)
    flags = fl_ref[p_id]
    first = (flags & 1) != 0
    last = (flags & 2) != 0
    masked = (flags & 4) != 0

    @pl.when(first)
    def _():
        m_scr[...] = jnp.full(m_scr.shape, NEG, F32)
        acc_scr[...] = jnp.zeros(acc_scr.shape, F32)

    qb = min(MLA_QUERY_BLOCK, tq)

    def scores_t(h, c0):
        sl = slice(h * LANES, (h + 1) * LANES)
        return _dot_nt(k_ref[0, :, sl], q_ref[0, c0:c0 + qb, sl])

    def body(use_mask, first_query=0):
        insts = [(h, c0) for h in range(MLA_HEADS) for c0 in range(first_query, tq, qb)]
        if use_mask:
            q_chunk0 = (qi_ref[p_id] * tq) // CHUNK
            k_chunk0 = (ki_ref[p_id] * tk) // CHUNK
            kc = k_chunk0 + lax.broadcasted_iota(jnp.int32, (tk, qb), 0) // CHUNK
            qc_local = lax.broadcasted_iota(jnp.int32, (tk, qb), 1) // CHUNK
        ahead = [scores_t(*insts[i]) for i in range(min(MLA_SCORES_AHEAD, len(insts)))]
        for idx, (h, c0) in enumerate(insts):
            s = ahead.pop(0)
            if idx + MLA_SCORES_AHEAD < len(insts):
                ahead.append(scores_t(*insts[idx + MLA_SCORES_AHEAD]))
            if use_mask:
                s = jnp.where(kc <= qc_local + (q_chunk0 + c0 // CHUNK), s, NEG)
            m_prev = m_scr[h:h + 1, c0:c0 + qb]
            m_new = jnp.maximum(m_prev, jnp.max(s, axis=0, keepdims=True))
            m_scr[h:h + 1, c0:c0 + qb] = m_new
            alpha = jnp.exp2(m_prev - m_new)
            p_t = jnp.exp2(s - m_new).astype(BF16)
            vt_h = vt_ref[0, h * MLA_V_SLOT:(h + 1) * MLA_V_SLOT, :]
            acc_scr[h, :, c0:c0 + qb] = alpha * acc_scr[h, :, c0:c0 + qb] + _dot(vt_h, p_t)

    diag = flags >> 3
    for j in range(max(tq // tk, 1)):
        @pl.when(masked & (diag == j))
        def _(j=j):
            body(True, first_query=j * tk)

    @pl.when(jnp.logical_not(masked))
    def _():
        body(False)

    @pl.when(last)
    def _():
        outs = []
        for h in range(MLA_HEADS):
            a = acc_scr[h]
            outs.append(a[0:MLA_V] / a[MLA_V:MLA_V + 1])
        o_ref[0] = jnp.concatenate(outs, axis=0).T.astype(BF16)


MLA_Q_TILE = 1024
MLA_KV_TILE = 512
MLA_QUERY_BLOCK = 256
MLA_SCORES_AHEAD = 3


def _mla_attn(q, k, vt, pairs, tq, tk):
    b, t_q, _ = q.shape
    qi, ki, fl = pairs
    grid_spec = pltpu.PrefetchScalarGridSpec(
        num_scalar_prefetch=3,
        grid=(b, qi.shape[0]),
        in_specs=[pl.BlockSpec((1, tq, 1024), lambda bb, p, qi, ki, fl: (bb, qi[p], 0)),
                  pl.BlockSpec((1, tk, 1024), lambda bb, p, qi, ki, fl: (bb, ki[p], 0)),
                  pl.BlockSpec((1, MLA_V_ROWS, tk), lambda bb, p, qi, ki, fl: (bb, 0, ki[p]))],
        out_specs=pl.BlockSpec((1, tq, 512), lambda bb, p, qi, ki, fl: (bb, qi[p], 0)),
        scratch_shapes=[pltpu.VMEM((MLA_HEADS, tq), F32),
                        pltpu.VMEM((MLA_HEADS, MLA_V_SLOT, tq), F32)])
    return pl.pallas_call(
        functools.partial(_mla_attn_kernel, tq=tq, tk=tk),
        grid_spec=grid_spec,
        out_shape=jax.ShapeDtypeStruct((b, t_q, 512), BF16),
        compiler_params=_cparams(("parallel", "arbitrary")),
    )(qi, ki, fl, q, k, vt)


def _causal_pairs(nq, ratio):
    qi, ki, fl = [], [], []
    for a in range(nq):
        n_kv = (a + 1) * ratio
        for c in range(n_kv):
            qi.append(a)
            ki.append(c)
            diag = max(c - a * ratio, 0)
            fl.append((1 if c == 0 else 0) | (2 if c == n_kv - 1 else 0) | (4 if c >= a * ratio else 0)
                      | (diag << 3))
    return tuple(jnp.asarray(np.array(z, np.int32)) for z in (qi, ki, fl))


def _mla_decode_kernel(q_ref, cc_ref, kc_ref, cn_ref, kn_ref, wka_ref, prope_ref, wvp_ref, o_ref,
                       qa_scr, qr_scr, m_scr, l_scr, acc_scr, *, n_cache_blocks, t):
    k_id = pl.program_id(1)

    @pl.when(k_id == 0)
    def _():
        for h in range(MLA_HEADS):
            qs = q_ref[0, :, h * LANES:(h + 1) * LANES]
            qa_scr[h * t:(h + 1) * t, :] = _dot(qs, wka_ref[h]).astype(BF16)
            qr_scr[h * t:(h + 1) * t, :] = _dot(qs, prope_ref[...]).astype(BF16)
        m_scr[...] = jnp.full(m_scr.shape, NEG, F32)
        l_scr[...] = jnp.zeros(l_scr.shape, F32)
        acc_scr[...] = jnp.zeros(acc_scr.shape, F32)

    def update(ckv, kr):
        cb = ckv.astype(BF16)
        s = _dot_nt(qa_scr[...], cb) + _dot_nt(qr_scr[:, 0:MLA_ROPE], kr.astype(BF16))
        m_prev = m_scr[...]
        m_new = jnp.maximum(m_prev, jnp.max(s, axis=-1, keepdims=True))
        alpha = jnp.exp2(m_prev - m_new)
        p = jnp.exp2(s - m_new[:, 0:1])
        l_scr[...] = alpha * l_scr[...] + jnp.sum(p, axis=-1, keepdims=True)
        m_scr[...] = m_new
        acc_scr[...] = jnp.concatenate([alpha, alpha], axis=1) * acc_scr[...] + _dot(p.astype(BF16), cb)

    @pl.when(k_id < n_cache_blocks)
    def _():
        update(cc_ref[0], kc_ref[0])

    @pl.when(k_id == n_cache_blocks)
    def _():
        update(cn_ref[0], kn_ref[0])
        l = l_scr[...]
        o_lat = (acc_scr[...] / jnp.concatenate([l, l], axis=1)).astype(BF16)
        for pr in range(MLA_HEADS // 2):
            oe = o_lat[(2 * pr) * t:(2 * pr + 1) * t]
            oo = o_lat[(2 * pr + 1) * t:(2 * pr + 2) * t]
            o_ref[0, :, pr * LANES:(pr + 1) * LANES] = (
                _dot(oe, wvp_ref[2 * pr]) + _dot(oo, wvp_ref[2 * pr + 1])).astype(BF16)


MLA_DECODE_KV_TILE = 2048


def _mla_decode(q, ckv_cache, kr_cache, layer, ckv_new, kr_new, wka, prope, wvp):
    b, t, _ = q.shape
    n_past = ckv_cache.shape[2]
    kb = math.gcd(n_past, MLA_DECODE_KV_TILE)
    ncb = n_past // kb
    rows = MLA_HEADS * t
    per_b = lambda shape: pl.BlockSpec((1,) + shape, lambda bb, k: (bb, 0, 0))
    cache = lambda w: pl.BlockSpec((None, 1, kb, w), lambda bb, k: (layer, bb, jnp.minimum(k, ncb - 1), 0))
    return pl.pallas_call(
        functools.partial(_mla_decode_kernel, n_cache_blocks=ncb, t=t),
        grid=(b, ncb + 1),
        in_specs=[per_b((t, 1024)), cache(256), cache(MLA_ROPE), per_b((t, 256)), per_b((t, MLA_ROPE)),
                  _const_spec(wka.shape), _const_spec(prope.shape), _const_spec(wvp.shape)],
        out_specs=per_b((t, 512)),
        out_shape=jax.ShapeDtypeStruct((b, t, 512), BF16),
        scratch_shapes=[pltpu.VMEM((rows, 256), BF16), pltpu.VMEM((rows, LANES), BF16),
                        pltpu.VMEM((rows, LANES), F32), pltpu.VMEM((rows, LANES), F32),
                        pltpu.VMEM((rows, 256), F32)],
        compiler_params=_cparams(("parallel", "arbitrary")),
    )(q, ckv_cache, kr_cache, ckv_new, kr_new, wka, prope, wvp)


RW_CHUNK_GROUP = 2
RW_TIME_TILE = 512


def _pair_sum(x, ones_bd):
    hi, lo = _split2(x)
    return _dot(hi, ones_bd) + _dot(lo, ones_bd)


def _rwkv_kernel(pr_ref, sh0_ref, s0_ref, mu_ref, w0_ref, wl_ref, a0_ref, al_ref, g2_ref,
                 kk_ref, ka_ref, rk_ref, lnw_ref, lnb_ref, y_ref, st_ref,
                 prev_scr, s_scr, rt_scr, at_scr, bt_scr, kt_scr, bv_scr, k2_scr, v_scr,
                 cum_scr, yc_scr, *, tt):
    t_id = pl.program_id(1)
    n_pairs = RW_HEADS // 2

    @pl.when(t_id == 0)
    def _():
        s_scr[...] = s0_ref[0]
        prev_scr[0:1, :] = sh0_ref[0]

    n_chunks = tt // CHUNK
    groups = [list(range(c0, min(c0 + RW_CHUNK_GROUP, n_chunks))) for c0 in range(0, n_chunks, RW_CHUNK_GROUP)]
    gate, bonus = {}, {}

    carried_prev = prev_scr[0:1, :]
    prev_scr[0:1, :] = pr_ref[0, tt - 1:tt, :]

    li = lax.broadcasted_iota(jnp.int32, (LANES, LANES), 0)
    lj = lax.broadcasted_iota(jnp.int32, (LANES, LANES), 1)
    ones_bd = jnp.where((li // RW_N) == (lj // RW_N), 1.0, 0.0).astype(BF16)

    def head_sum(x):
        return jnp.concatenate([_pair_sum(x[:, p * LANES:(p + 1) * LANES], ones_bd)
                                for p in range(n_pairs)], axis=1)

    def prep_phase(gi):
        r0 = groups[gi][0] * CHUNK
        gr = len(groups[gi]) * CHUNK
        rows = slice(r0, r0 + gr)
        pr = pr_ref[0, rows, :]
        before = carried_prev if gi == 0 else pr_ref[0, r0 - 1:r0, :]
        row = lax.broadcasted_iota(jnp.int32, (gr, 1), 0)
        prev = jnp.where(row == 0, before, pltpu.roll(pr, 1, axis=0))
        pm = pr + (prev - pr) * mu_ref[...]
        r = pm[:, 0:512]
        k = pm[:, 512:1024]
        v = pm[:, 1024:1536]
        wa = pm[:, 1536:1664]
        g_in = pm[:, 1664:1792]
        yield
        z = w0_ref[...] + _dot(jnp.tanh(wa).astype(BF16), wl_ref[...])
        nz = -z
        w = -(jnp.maximum(nz, 0.0) + jnp.log(1.0 + jnp.exp(-jnp.abs(nz)))) - 0.5
        ld = -jnp.exp(w)
        yield
        a_sig = jax.nn.sigmoid(a0_ref[...] + _dot(wa.astype(BF16), al_ref[...]))
        gate[gi] = _dot(jax.nn.sigmoid(g_in).astype(BF16), g2_ref[...])
        yield
        kk = k * kk_ref[...]
        kk = kk * lax.rsqrt(jnp.maximum(head_sum(kk * kk), 1e-24))
        k2 = k * (1.0 + (a_sig - 1.0) * ka_ref[...])
        yield
        bonus[gi] = head_sum(r * k2 * rk_ref[...]) * v
        yield
        ti = lax.broadcasted_iota(jnp.int32, (gr, gr), 0)
        tj = lax.broadcasted_iota(jnp.int32, (gr, gr), 1)
        tri = jnp.where(((ti // CHUNK) == (tj // CHUNK)) & (tj <= ti), 1.0, 0.0).astype(BF16)
        l1 = ld.astype(BF16)
        rem = ld - l1.astype(F32)
        l2 = rem.astype(BF16)
        l3 = (rem - l2.astype(F32)).astype(BF16)
        cum = (_dot(tri, l3) + _dot(tri, l2)) + _dot(tri, l1)
        yield
        winv = jnp.exp(-cum)
        rt_scr[rows, :] = r * jnp.exp(cum)
        at_scr[rows, :] = -kk * jnp.exp(cum - ld)
        yield
        bv = kk * a_sig
        bt_scr[rows, :] = bv * winv
        kt_scr[rows, :] = k2 * winv
        bv_scr[rows, :] = bv
        k2_scr[rows, :] = k2
        v_scr[rows, :] = v
        cum_scr[rows, :] = cum

    lane = lax.broadcasted_iota(jnp.int32, (CHUNK, LANES), 1)
    even = lane < RW_N
    strict = (lj % RW_N) < (li % RW_N)
    incl = (lj % RW_N) <= (li % RW_N)
    eye = jnp.where(li == lj, 1.0, 0.0).astype(F32)
    pairs = range(n_pairs)

    def same_block(m):
        return (li // m) == (lj // m)

    def stack_f32(x):
        return jnp.concatenate([jnp.where(even, x, 0.0), jnp.where(even, 0.0, x)], axis=0)

    def stack(x):
        return stack_f32(x).astype(BF16)

    mm = lambda a, b: _dot(a.astype(BF16), b.astype(BF16))
    rows_of = lambda c: slice(c * CHUNK, (c + 1) * CHUNK)
    lanes_of = lambda p: slice(p * LANES, (p + 1) * LANES)

    def independent_phase(chunks, res):
        insts = [(c, p) for c in chunks for p in pairs]
        load = lambda scr: [scr[rows_of(c), lanes_of(p)] for c, p in insts]
        cum_c = load(cum_scr)
        cum_l = [a[CHUNK - 1:CHUNK, :] for a in cum_c]
        w2 = [jnp.exp(a - b) for a, b in zip(cum_l, cum_c)]
        rs = [stack(a) for a in load(rt_scr)]
        as_ = [stack(a) for a in load(at_scr)]
        bs = [stack(a) for a in load(bt_scr)]
        ks = [stack(a) for a in load(kt_scr)]
        v_f = [stack_f32(a) for a in load(v_scr)]
        vs = [a.astype(BF16) for a in v_f]
        b2s = [stack(a * w) for a, w in zip(load(bv_scr), w2)]
        k2s = [stack(a * w) for a, w in zip(load(k2_scr), w2)]
        yield
        bk = [jnp.concatenate([b_, k_], axis=0) for b_, k_ in zip(bs, ks)]
        a_bk = [_dot_nt(a, c_) for a, c_ in zip(as_, bk)]
        r_bk = [_dot_nt(a, c_) for a, c_ in zip(rs, bk)]
        n_m = [jnp.where(strict, x_[:, 0:LANES], 0.0) for x_ in a_bk]
        mk = [jnp.where(strict, x_[:, LANES:2 * LANES], 0.0).astype(BF16) for x_ in a_bk]
        cb = [jnp.where(incl, x_[:, 0:LANES], 0.0).astype(BF16) for x_ in r_bk]
        ck = [jnp.where(incl, x_[:, LANES:2 * LANES], 0.0).astype(BF16) for x_ in r_bk]
        yield
        n8 = [jnp.where(same_block(8), n, 0.0) for n in n_m]
        t = [eye + a for a in n8]
        p2 = [mm(a, a) for a in n8]
        mv = [_dot(a, b) for a, b in zip(mk, vs)]
        yield
        t = [a + mm(a, b) for a, b in zip(t, p2)]
        p4 = [mm(a, a) for a in p2]
        cv = [_dot(a, b) for a, b in zip(ck, vs)]
        yield
        t = [a + mm(a, b) for a, b in zip(t, p4)]
        vk = [_dot(a.T.astype(BF16), b) for a, b in zip(v_f, k2s)]
        yield
        for m in (8, 16, 32):
            off = same_block(2 * m) & jnp.logical_not(same_block(m))
            x = [mm(jnp.where(off, n, 0.0), a) for n, a in zip(n_m, t)]
            yield
            t = [a + mm(a, b) for a, b in zip(t, x)]
            yield
        for i, key in enumerate(insts):
            res[key] = dict(as_=as_[i], rs=rs[i], b2s=b2s[i], cb=cb[i], mv=mv[i], cv=cv[i], vk=vk[i],
                            tinv=t[i].astype(BF16), wl=jnp.exp(cum_l[i]))

    def dependent_phase(chunks, res):
        for c in chunks:
            rc = [res[(c, p)] for p in pairs]
            s = [s_scr[p] for p in pairs]
            sb = [a.astype(BF16) for a in s]
            x = [_dot_nt(r_['as_'], b) + r_['mv'] for r_, b in zip(rc, sb)]
            rs_s = [_dot_nt(r_['rs'], b) + r_['cv'] for r_, b in zip(rc, sb)]
            yield
            u = [_dot(r_['tinv'], a.astype(BF16)) for r_, a in zip(rc, x)]
            yield
            ys = [a + _dot(r_['cb'], b.astype(BF16)) for a, r_, b in zip(rs_s, rc, u)]
            for p in pairs:
                yc_scr[rows_of(c), lanes_of(p)] = ys[p][0:CHUNK] + ys[p][CHUNK:2 * CHUNK]
                s_scr[p] = s[p] * rc[p]['wl'] + _dot(u[p].T.astype(BF16), rc[p]['b2s']) + rc[p]['vk']
            yield

    def emit(*gens):
        gens = list(gens)
        while gens:
            for gen in list(gens):
                try:
                    next(gen)
                except StopIteration:
                    gens.remove(gen)

    def output_phase(gi):
        r0 = groups[gi][0] * CHUNK
        rows = slice(r0, r0 + len(groups[gi]) * CHUNK)
        y = yc_scr[rows, :]
        mean = head_sum(y) * (1.0 / RW_N)
        dlt = y - mean
        yield
        var = head_sum(dlt * dlt) * (1.0 / RW_N)
        yield
        yn = dlt * lax.rsqrt(var + RW_GN_EPS) * lnw_ref[...] + lnb_ref[...]
        y_ref[0, rows, :] = ((yn + bonus[gi]) * gate[gi]).astype(BF16)

    res = {}
    n_g = len(groups)
    for step in range(n_g + 3):
        live = []
        if step < n_g:
            live.append(prep_phase(step))
        if 0 <= step - 1 < n_g:
            live.append(independent_phase(groups[step - 1], res))
        if 0 <= step - 2 < n_g:
            live.append(dependent_phase(groups[step - 2], res))
        if 0 <= step - 3 < n_g:
            live.append(output_phase(step - 3))
        emit(*live)

    @pl.when(t_id == pl.num_programs(1) - 1)
    def _():
        st_ref[0] = s_scr[...]


def _rwkv(prw, sh0, s0_bd, wts):
    b, t, _ = prw.shape
    tt = min(RW_TIME_TILE, t)
    assert t % tt == 0 and tt % CHUNK == 0
    c512 = _const_spec((1, RW_C))
    scr = lambda: pltpu.VMEM((tt, RW_C), F32)
    return pl.pallas_call(
        functools.partial(_rwkv_kernel, tt=tt),
        grid=(b, t // tt),
        in_specs=[pl.BlockSpec((1, tt, RW_COLS), lambda bb, i: (bb, i, 0)),
                  pl.BlockSpec((1, 1, RW_COLS), lambda bb, i: (bb, 0, 0)),
                  pl.BlockSpec((1, 4, LANES, LANES), lambda bb, i: (bb, 0, 0, 0)),
                  _const_spec((1, RW_COLS)), c512, _const_spec((LANES, RW_C)), c512,
                  _const_spec((LANES, RW_C)), _const_spec((LANES, RW_C)),
                  c512, c512, c512, c512, c512],
        out_specs=[pl.BlockSpec((1, tt, RW_C), lambda bb, i: (bb, i, 0)),
                   pl.BlockSpec((1, 4, LANES, LANES), lambda bb, i: (bb, 0, 0, 0))],
        out_shape=[jax.ShapeDtypeStruct((b, t, RW_C), BF16),
                   jax.ShapeDtypeStruct((b, 4, LANES, LANES), F32)],
        scratch_shapes=[pltpu.VMEM((8, RW_COLS), F32), pltpu.VMEM((4, LANES, LANES), F32)]
                       + [scr() for _ in range(9)],
        compiler_params=_cparams(("parallel", "arbitrary")),
    )(prw, sh0, s0_bd, *wts)


def _state_to_bd(s):
    b = s.shape[0]
    s = s.reshape(b, 4, 2, RW_N, RW_N)
    z = jnp.zeros_like(s[:, :, 0])
    top = jnp.concatenate([s[:, :, 0], z], axis=-1)
    bot = jnp.concatenate([z, s[:, :, 1]], axis=-1)
    return jnp.concatenate([top, bot], axis=-2)


def _state_from_bd(s):
    b = s.shape[0]
    return jnp.stack([s[:, :, :RW_N, :RW_N], s[:, :, RW_N:, RW_N:]], axis=2).reshape(b, RW_HEADS, RW_N, RW_N)


_ODD_SEGS = ([(1024 + g * 64, 1088 + g * 64) for g in range(SW_KV_HEADS) for _ in range(2)]
             + [(1280 + g * 64, 1344 + g * 64) for g in range(SW_KV_HEADS) for _ in range(2)])


def _odd_in_kernel(x_ref, mods_ref, ng_ref, w_ref, b_ref, q_out, k_out, v_out, *, nb):
    h = _norm_mod(x_ref[...], ng_ref[...], mods_ref[:, 3, :], mods_ref[:, 4, :], nb).astype(BF16)
    q = _dot(h, w_ref[:, 0:1024]) + b_ref[:, 0:1024]
    q_out[...] = (q * (SW_HD ** -0.5 * LOG2E)).astype(BF16)
    k_out[...] = _dot(h, w_ref[:, 1024:1536]) + b_ref[:, 1024:1536]
    v_out[...] = _dot(h, w_ref[:, 1536:2048]) + b_ref[:, 1536:2048]


def _odd_in(x, mods, norm_g, w_perm, b_perm, rows_per_seq):
    n, d = x.shape
    tm, nb = _row_tiling(n, rows_per_seq, 512)
    row = lambda w: pl.BlockSpec((tm, w), lambda i: (i, 0))
    return pl.pallas_call(
        functools.partial(_odd_in_kernel, nb=nb),
        grid=(n // tm,),
        in_specs=[row(d), _mods_spec(tm, nb, rows_per_seq), _const_spec((1, d)),
                  _const_spec(w_perm.shape), _const_spec((1, 2048))],
        out_specs=[row(1024), row(512), row(512)],
        out_shape=[jax.ShapeDtypeStruct((n, 1024), BF16), jax.ShapeDtypeStruct((n, 512), F32),
                   jax.ShapeDtypeStruct((n, 512), F32)],
        compiler_params=_cparams(("parallel",)),
    )(x, mods, norm_g.reshape(1, d), w_perm, b_perm.reshape(1, 2048))


def _swa_kernel(sinks_ref, q_ref, kp_ref, kc_ref, vp_ref, vc_ref, o_ref, *, tq, mask_first_prev):
    nk = WINDOW + tq
    qi = lax.broadcasted_iota(jnp.int32, (tq, nk), 0)
    kj = lax.broadcasted_iota(jnp.int32, (tq, nk), 1)
    kc = kj // CHUNK - WINDOW // CHUNK
    qc = qi // CHUNK
    vis = (kc <= qc) & (kc >= qc - WINDOW // CHUNK)
    if mask_first_prev:
        vis = vis & ((kj >= WINDOW) | (pl.program_id(1) > 0))
    ndist = jnp.where(vis, -jnp.abs(qi + WINDOW - kj).astype(F32), NEG)
    keys = jnp.concatenate([kp_ref[0], kc_ref[0]], axis=0).astype(BF16)
    vals = jnp.concatenate([vp_ref[0], vc_ref[0]], axis=0)
    klane = lax.broadcasted_iota(jnp.int32, (nk, LANES), 1)
    v_slots = [jnp.where(klane < SW_HD, vals[:, g * LANES:(g + 1) * LANES], 1.0).astype(BF16)
               for g in range(SW_KV_HEADS)]
    lane = lax.broadcasted_iota(jnp.int32, (tq, LANES), 1)
    low = lane < SW_HD
    sb = min(SWA_SUB_BLOCK, tq)

    def scores(h):
        qp = q_ref[0, :, (h // 2) * LANES:(h // 2 + 1) * LANES]
        qh = jnp.where(low if h % 2 == 0 else jnp.logical_not(low), qp, jnp.zeros_like(qp))
        return _dot_nt(qh, keys[:, (h // SW_GROUP) * LANES:(h // SW_GROUP + 1) * LANES])

    outs = []
    ahead = [scores(h) for h in range(SWA_SCORES_AHEAD)]
    for h in range(SW_HEADS):
        s = ahead.pop(0)
        if h + SWA_SCORES_AHEAD < SW_HEADS:
            ahead.append(scores(h + SWA_SCORES_AHEAD))
        slope = (2.0 ** (-8.0 * (h + 1) / SW_HEADS)) * LOG2E
        sk = sinks_ref[h] * LOG2E
        es, ms = [], []
        for r1 in range(0, tq, sb):
            z = s[r1:r1 + sb] + slope * ndist[r1:r1 + sb]
            m = jnp.maximum(jnp.broadcast_to(jnp.max(z, axis=-1, keepdims=True), (sb, LANES)), sk)
            m_wide = jnp.concatenate([m] * (nk // LANES), axis=1) if nk % LANES == 0 else m[:, 0:1]
            es.append(jnp.exp2(z - m_wide).astype(BF16))
            ms.append(m)
        e = jnp.concatenate(es, axis=0) if len(es) > 1 else es[0]
        m = jnp.concatenate(ms, axis=0) if len(ms) > 1 else ms[0]
        pv = _dot(e, v_slots[h // SW_GROUP])
        outs.append(pv / (pltpu.roll(pv, SW_HD, axis=1) + jnp.exp2(sk - m)))
    for pr in range(SW_HEADS // 2):
        o_ref[0, :, pr * LANES:(pr + 1) * LANES] = jnp.where(
            low, outs[2 * pr], pltpu.roll(outs[2 * pr + 1], SW_HD, axis=1)).astype(BF16)


SWA_Q_TILE = 256
SWA_SUB_BLOCK = 32
SWA_SCORES_AHEAD = 2


def _swa(q, k_prev, k_cur, v_prev, v_cur, sinks, tq, same_array):
    b, t, _ = q.shape
    nt = t // tq
    per = tq // WINDOW
    if same_array:
        prev_map = lambda bb, i: (bb, jnp.maximum(i * per - 1, 0), 0)
    else:
        prev_map = lambda bb, i: (bb, 0, 0)
    cur = lambda w: pl.BlockSpec((1, tq, w), lambda bb, i: (bb, i, 0))
    prev = pl.BlockSpec((1, WINDOW, 512), prev_map)
    return pl.pallas_call(
        functools.partial(_swa_kernel, tq=tq, mask_first_prev=same_array),
        grid=(b, nt),
        in_specs=[pl.BlockSpec(memory_space=pltpu.SMEM), cur(1024), prev, cur(512), prev, cur(512)],
        out_specs=cur(1024),
        out_shape=jax.ShapeDtypeStruct((b, t, 1024), BF16),
        compiler_params=_cparams(("parallel", "parallel")),
    )(sinks, q, k_prev, k_cur, v_prev, v_cur)


def _undup(a):
    return a.reshape(a.shape[:-1] + (SW_KV_HEADS, 2, SW_HD))[..., 0, :]


def _dup(a):
    return jnp.concatenate([a, a], axis=-1).reshape(a.shape[:-2] + (512,))


def _prep_weights(p):
    depth = p['w_ada'].shape[0]
    n_even, n_odd = (depth + 1) // 2, depth // 2
    w = {}
    w['ffn_in'] = p['ffn_w_in'].astype(BF16)
    w['ffn_out'] = p['ffn_w_out'].astype(BF16)
    wi = p['even_w_in'].astype(BF16)
    w['even_in'] = jnp.concatenate(
        [wi[:, :, 0:1024]] + [wi[:, :, _RW_OFF + a:_RW_OFF + b] for a, b in _RW_SEGS]
        + [wi[:, :, 1024:1056], jnp.zeros((n_even, 1024, 96), BF16)], axis=2)
    w['wuq'] = jnp.pad(p['mla_w_uq'], ((0, 0), (0, 0), (0, 0), (0, 32))).reshape(n_even, 768, 1024).astype(BF16)
    w['wk'] = jnp.pad(p['mla_w_ukv'][..., :MLA_NOPE], ((0, 0), (0, 0), (0, 0), (0, 64))
                      ).reshape(n_even, 256, 1024).astype(BF16)
    wv_t = jnp.transpose(p['mla_w_ukv'][..., MLA_NOPE:], (0, 2, 3, 1))
    w['wv'] = jnp.pad(wv_t, ((0, 0), (0, 0), (0, MLA_V_SLOT - MLA_V), (0, 0))
                      ).reshape(n_even, MLA_V_ROWS, 256).astype(BF16)
    sel = np.zeros((MLA_ROPE, 1024), np.float32)
    one = np.zeros((MLA_V_ROWS, 1), np.float32)
    for h in range(MLA_HEADS):
        sel[np.arange(MLA_ROPE), h * LANES + MLA_NOPE + np.arange(MLA_ROPE)] = 1.0
        one[h * MLA_V_SLOT + MLA_V, 0] = 1.0
    w['sel'] = jnp.asarray(sel).astype(BF16)
    w['one'] = jnp.asarray(one)
    wk_t = jnp.transpose(p['mla_w_ukv'][..., :MLA_NOPE], (0, 2, 3, 1))
    w['wka'] = jnp.pad(wk_t, ((0, 0), (0, 0), (0, 64), (0, 0))).astype(BF16)
    prope = np.zeros((LANES, LANES), np.float32)
    prope[MLA_NOPE + np.arange(MLA_ROPE), np.arange(MLA_ROPE)] = 1.0
    w['prope'] = jnp.asarray(prope).astype(BF16)
    wv_h = jnp.transpose(p['mla_w_ukv'][..., MLA_NOPE:], (0, 2, 1, 3))
    wv_even = jnp.pad(wv_h[:, 0::2], ((0, 0), (0, 0), (0, 0), (0, 64)))
    wv_odd = jnp.pad(wv_h[:, 1::2], ((0, 0), (0, 0), (0, 0), (64, 0)))
    w['wvp'] = jnp.stack([wv_even, wv_odd], axis=2).reshape(n_even, MLA_HEADS, 256, LANES).astype(BF16)
    z64 = jnp.zeros((n_even, 64, RW_C), F32)
    w['wl'] = jnp.concatenate([p['rw_w2'], z64], axis=1).astype(BF16)
    w['al'] = jnp.concatenate([z64, p['rw_a2']], axis=1).astype(BF16)
    w['g2'] = p['rw_g2'].astype(BF16)
    w['mu'] = _rw_permute(p['rw_mu'])
    w['even_out'] = p['even_w_out'].astype(BF16)
    wo = p['odd_w_qkv'].astype(BF16)
    w['odd_in'] = jnp.concatenate([wo[:, :, 0:1024]] + [wo[:, :, a:b] for a, b in _ODD_SEGS], axis=2)
    bo = p['odd_b_qkv']
    w['odd_b'] = jnp.concatenate([bo[:, 0:1024]] + [bo[:, a:b] for a, b in _ODD_SEGS], axis=1)
    w['odd_out'] = p['odd_w_out'].astype(BF16)
    return w


def _trunk(x3, mods_all, start, past, p, w):
    b, t, d = x3.shape
    n = b * t
    depth = mods_all.shape[0]
    x = x3.reshape(n, d)
    rows = t
    tm_even, _ = _row_tiling(n, rows, 512)
    pos = start + jnp.arange(t)
    tabs = _rope_tables(pos, tm_even)
    even_states, odd_states = [], []
    for i in range(depth):
        mods = mods_all[i]
        j = i // 2
        x = _ffn(x, mods, p['norm_g'][i, 0], w['ffn_in'], w['ffn_out'], i, 0, 0, rows)
        if i % 2 == 0:
            q, ckv, kr, prw = _even_in(x, mods, p['norm_g'][i, 1], w['even_in'][j], p['mla_q_norm'][j],
                                       p['mla_kv_norm'][j], w['wuq'][j], tabs, rows)
            if past is None:
                kx, vtx = _kv_expand(ckv, kr, w['wk'][j], w['sel'], w['wv'][j], w['one'], b)
                tk = min(MLA_KV_TILE, t)
                tq = min(MLA_Q_TILE, t)
                att = _mla_attn(q.reshape(b, t, 1024), kx.reshape(b, t, 1024), vtx,
                                _causal_pairs(t // tq, tq // tk), tq, tk)
                s0 = jnp.zeros((b, RW_HEADS, RW_N, RW_N), F32)
                sh0 = jnp.zeros((b, RW_COLS), F32)
            else:
                s0, sh0 = past[2][j], past[3][j]
                att = _mla_decode(q.reshape(b, t, 1024), past[0], past[1], j, ckv.reshape(b, t, 256),
                                  kr.reshape(b, t, MLA_ROPE), w['wka'][j], w['prope'], w['wvp'][j])
            rw_wts = (w['mu'][j].reshape(1, RW_COLS), p['rw_w0'][j].reshape(1, RW_C), w['wl'][j],
                      p['rw_a0'][j].reshape(1, RW_C), w['al'][j], w['g2'][j],
                      p['rw_k_k'][j].reshape(1, RW_C), p['rw_k_a'][j].reshape(1, RW_C),
                      p['rw_r_k'][j].reshape(1, RW_C), p['rw_ln_w'][j].reshape(1, RW_C),
                      p['rw_ln_b'][j].reshape(1, RW_C))
            prw3 = prw.reshape(b, t, RW_COLS)
            y_rw, s_bd = _rwkv(prw3, _rw_permute(sh0).reshape(b, 1, RW_COLS), _state_to_bd(s0), rw_wts)
            mix = ((att.reshape(n, 512), w['even_out'][j][:512]), (y_rw.reshape(n, RW_C), w['even_out'][j][512:]))
            even_states.append((ckv.reshape(b, t, 256), kr.reshape(b, t, MLA_ROPE), _state_from_bd(s_bd),
                                _rw_unpermute(prw3[:, t - 1, :])))
        else:
            q, kd, vd = _odd_in(x, mods, p['norm_g'][i, 1], w['odd_in'][j], w['odd_b'][j], rows)
            q3, kd3, vd3 = q.reshape(b, t, 1024), kd.reshape(b, t, 512), vd.reshape(b, t, 512)
            if past is None:
                tq = min(SWA_Q_TILE, t)
                o = _swa(q3, kd3, kd3, vd3, vd3, p['swa_sinks'][j], tq, True)
                keep = min(WINDOW, t)
                k_new, v_new = _undup(kd3[:, t - keep:]), _undup(vd3[:, t - keep:])
            else:
                k_past, v_past = past[4][j], past[5][j]
                o = _swa(q3, _dup(k_past), kd3, _dup(v_past), vd3, p['swa_sinks'][j], t, False)
                k_new = jnp.concatenate([k_past, _undup(kd3)], axis=1)[:, t:]
                v_new = jnp.concatenate([v_past, _undup(vd3)], axis=1)[:, t:]
            mix = ((o.reshape(n, 1024), w['odd_out'][j]),)
            odd_states.append((k_new, v_new))
        fg = p['final_norm_g'] if i == depth - 1 else None
        x = _ffn(x, mods, p['norm_g'][i, 2], w['ffn_in'], w['ffn_out'], i, 1, 2, rows, final_g=fg, mix=mix)
    es = [jnp.stack([st[k] for st in even_states]) for k in range(4)]
    os_ = [jnp.stack([st[k] for st in odd_states]) for k in range(2)]
    return x.reshape(b, t, d), es + os_


def kernel(x_prompt, x_sample, cache_mla_ckv, cache_mla_krope, state_rwkv, state_rwkv_shift, cache_swa_k, cache_swa_v, c_prompt, c_sample, w_ada, b_ada, norm_g, ffn_w_in, ffn_w_out, even_w_in, even_w_out, mla_q_norm, mla_kv_norm, mla_w_uq, mla_w_ukv, rw_mu, rw_w0, rw_w2, rw_a0, rw_a2, rw_g2, rw_k_k, rw_k_a, rw_r_k, rw_ln_w, rw_ln_b, odd_w_qkv, odd_b_qkv, odd_w_out, swa_sinks, final_norm_g):
    p = dict(w_ada=w_ada, b_ada=b_ada, norm_g=norm_g, ffn_w_in=ffn_w_in, ffn_w_out=ffn_w_out,
             even_w_in=even_w_in, even_w_out=even_w_out, mla_q_norm=mla_q_norm, mla_kv_norm=mla_kv_norm,
             mla_w_uq=mla_w_uq, mla_w_ukv=mla_w_ukv, rw_mu=rw_mu, rw_w0=rw_w0, rw_w2=rw_w2, rw_a0=rw_a0,
             rw_a2=rw_a2, rw_g2=rw_g2, rw_k_k=rw_k_k, rw_k_a=rw_k_a, rw_r_k=rw_r_k, rw_ln_w=rw_ln_w,
             rw_ln_b=rw_ln_b, odd_w_qkv=odd_w_qkv, odd_b_qkv=odd_b_qkv, odd_w_out=odd_w_out,
             swa_sinks=swa_sinks, final_norm_g=final_norm_g)
    w = _prep_weights(p)
    depth = w_ada.shape[0]
    bp, bs = c_prompt.shape[0], c_sample.shape[0]
    d = c_prompt.shape[1]
    b_pad = -(-(bp + bs) // 8) * 8
    c_all = jnp.concatenate([c_prompt, c_sample, jnp.zeros((b_pad - bp - bs, d), F32)], axis=0)
    mods = _ada(c_all, w_ada, b_ada).reshape(depth, b_pad, 3 * N_SUB, d)
    y_prompt, sp = _trunk(x_prompt, mods[:, :bp], 0, None, p, w)
    past = (cache_mla_ckv, cache_mla_krope, state_rwkv, state_rwkv_shift, cache_swa_k, cache_swa_v)
    y_sample, ss = _trunk(x_sample, mods[:, bp:bp + bs], cache_mla_ckv.shape[2], past, p, w)
    return (y_prompt, y_sample, sp[0], sp[1], sp[2], sp[3], sp[4], sp[5],
            ss[0], ss[1], ss[2], ss[3], ss[4], ss[5])
```
